```python
import math
import jax, jax.numpy as jnp
from jax import lax
import numpy as np

D_MODEL = 2048
BATCH = 8
SEQ = 4096
DEPTH = 4

N_BRANCH = 3
W_MIX = D_MODEL // 2
S5_GROUP = 16
S5_GROUPS = W_MIX // S5_GROUP
S5_STATE = 64
S5_DT_MIN = 1e-3
S5_DT_MAX = 1e-1
RG_BLOCKS = 16
RG_BLOCK = W_MIX // RG_BLOCKS
RG_CONV = 4
RG_C = 8.0
HG_HEADS = 8
HG_DK = W_MIX // HG_HEADS
HG_DV = W_MIX // HG_HEADS
HG_CHUNK = 64
EPS = 1e-6

N_IN = 8 * W_MIX + N_BRANCH * D_MODEL
SPLITS = tuple(W_MIX * k for k in range(1, 9))

kernel_name = "hybrid_s5_rglru_hgrn2_gated_merge"


def rms_norm(x, w):
    x32 = x.astype(jnp.float32)
    y = x32 * lax.rsqrt(jnp.mean(x32 * x32, axis=-1, keepdims=True) + EPS)
    return (y * w.astype(jnp.float32)).astype(x.dtype)


def s5_mixer(u, lam_re, lam_im, log_step, b_re, b_im, c_re, c_im, d, w_glu, b_glu):
    f32 = jnp.float32
    bsz, s, _ = u.shape
    u32 = u.astype(f32).reshape(bsz, s, S5_GROUPS, S5_GROUP)
    lam_re = lam_re.astype(f32)
    lam_im = lam_im.astype(f32)
    step = jnp.exp(log_step.astype(f32))[:, None]
    mag = jnp.exp(lam_re * step)
    ang = lam_im * step
    abar_re = mag * jnp.cos(ang)
    abar_im = mag * jnp.sin(ang)
    num_re = abar_re - 1.0
    num_im = abar_im
    den = lam_re * lam_re + lam_im * lam_im
    coef_re = (num_re * lam_re + num_im * lam_im) / den
    coef_im = (num_im * lam_re - num_re * lam_im) / den
    b_re = b_re.astype(f32)
    b_im = b_im.astype(f32)
    bbar_re = coef_re[..., None] * b_re - coef_im[..., None] * b_im
    bbar_im = coef_re[..., None] * b_im + coef_im[..., None] * b_re
    bu_re = jnp.einsum('bsgh,gph->bsgp', u32, bbar_re)
    bu_im = jnp.einsum('bsgh,gph->bsgp', u32, bbar_im)
    a_re = jnp.broadcast_to(abar_re, (1, s) + abar_re.shape)
    a_im = jnp.broadcast_to(abar_im, (1, s) + abar_im.shape)

    def combine(left, right):
        ar1, ai1, br1, bi1 = left
        ar2, ai2, br2, bi2 = right
        return (ar2 * ar1 - ai2 * ai1,
                ar2 * ai1 + ai2 * ar1,
                ar2 * br1 - ai2 * bi1 + br2,
                ar2 * bi1 + ai2 * br1 + bi2)

    _, _, x_re, x_im = lax.associative_scan(combine, (a_re, a_im, bu_re, bu_im), axis=1)
    y = (jnp.einsum('bsgp,ghp->bsgh', x_re, c_re.astype(f32))
         - jnp.einsum('bsgp,ghp->bsgh', x_im, c_im.astype(f32)))
    y = y.reshape(bsz, s, W_MIX) + d.astype(f32) * u32.reshape(bsz, s, W_MIX)
    y = jax.nn.gelu(y)
    y = y * jax.nn.sigmoid(y @ w_glu.astype(f32) + b_glu.astype(f32))
    return y.astype(u.dtype)


def rg_lru_mixer(x, conv_w, conv_b, w_a, b_a, w_x, b_x, lam):
    f32 = jnp.float32
    bsz, s, _ = x.shape
    x32 = x.astype(f32)
    xc = lax.conv_general_dilated(
        x32, conv_w.astype(f32).reshape(RG_CONV, 1, W_MIX),
        window_strides=(1,), padding=[(RG_CONV - 1, 0)],
        dimension_numbers=('NWC', 'WIO', 'NWC'),
        feature_group_count=W_MIX) + conv_b.astype(f32)
    xb = xc.reshape(bsz, s, RG_BLOCKS, RG_BLOCK)
    r = jax.nn.sigmoid(jnp.einsum('bsni,nij->bsnj', xb, w_a.astype(f32)).reshape(bsz, s, W_MIX)
                       + b_a.astype(f32))
    i = jax.nn.sigmoid(jnp.einsum('bsni,nij->bsnj', xb, w_x.astype(f32)).reshape(bsz, s, W_MIX)
                       + b_x.astype(f32))
    log_a = -RG_C * r * jax.nn.softplus(-lam.astype(f32))
    a = jnp.exp(log_a)
    mult = jnp.sqrt(-jnp.expm1(2.0 * log_a))
    is_first = (jnp.arange(s) == 0)[None, :, None]
    mult = jnp.where(is_first, jnp.ones_like(mult), mult)
    b_term = mult * (i * xc)

    def combine(left, right):
        a1, b1 = left
        a2, b2 = right
        return a2 * a1, a2 * b1 + b2

    _, h = lax.associative_scan(combine, (a, b_term), axis=1)
    return h.astype(x.dtype)


def hgrn2_mixer(q, f_logit, inp, lb, norm_w):
    f32 = jnp.float32
    bsz, s, _ = q.shape
    n_chunks = s // HG_CHUNK
    q32 = jax.nn.silu(q.astype(f32))
    f = lb + (1.0 - lb) * jax.nn.sigmoid(f_logit.astype(f32))
    k32 = 1.0 - f
    g32 = jnp.log(f)
    v32 = inp.astype(f32)

    def to_chunks(t, d):
        return t.reshape(bsz, n_chunks, HG_CHUNK, HG_HEADS, d).transpose(1, 0, 3, 2, 4)

    qc = to_chunks(q32, HG_DK)
    kc = to_chunks(k32, HG_DK)
    gc = to_chunks(g32, HG_DK)
    vc = to_chunks(v32, HG_DV)
    causal = jnp.tril(jnp.ones((HG_CHUNK, HG_CHUNK), dtype=bool))

    def chunk_step(state, xs):
        qx, kx, gx, vx = xs
        G = jnp.cumsum(gx, axis=2)
        inter = jnp.einsum('bhtk,bhkv->bhtv', qx * jnp.exp(G), state)
        diff = G[:, :, :, None, :] - G[:, :, None, :, :]
        decay = jnp.exp(jnp.where(causal[None, None, :, :, None], diff, -jnp.inf))
        attn = jnp.einsum('bhtk,bhsk,bhtsk->bhts', qx, kx, decay)
        intra = jnp.einsum('bhts,bhsv->bhtv', attn, vx)
        g_last = G[:, :, -1]
        k_dec = kx * jnp.exp(g_last[:, :, None, :] - G)
        new_state = (jnp.exp(g_last)[..., None] * state
                     + jnp.einsum('bhsk,bhsv->bhkv', k_dec, vx))
        return new_state, inter + intra

    state0 = jnp.zeros((bsz, HG_HEADS, HG_DK, HG_DV), f32)
    _, o = lax.scan(chunk_step, state0, (qc, kc, gc, vc))
    o = o.transpose(1, 0, 3, 2, 4).reshape(bsz, s, HG_HEADS, HG_DV)
    o = o * lax.rsqrt(jnp.mean(o * o, axis=-1, keepdims=True) + EPS)
    o = o * norm_w.astype(f32).reshape(HG_HEADS, HG_DV)
    return o.reshape(bsz, s, W_MIX).astype(q.dtype)


def _fwd_setup_inputs(seed: int = 0) -> dict:
    key = jax.random.key(seed)
    ks = jax.random.split(key, 26)
    f32 = jnp.float32
    nrm = lambda k, shape, sc: sc * jax.random.normal(k, shape, f32)
    x = jax.random.normal(ks[0], (BATCH, SEQ, D_MODEL), f32)
    norm_w = 1.0 + nrm(ks[1], (DEPTH, D_MODEL), 0.02)
    w_in = nrm(ks[2], (DEPTH, D_MODEL, N_IN), D_MODEL ** -0.5)
    s5_lambda_re = -0.5 + nrm(ks[3], (DEPTH, S5_GROUPS, S5_STATE), 0.01)
    s5_lambda_im = (math.pi * jnp.arange(S5_STATE, dtype=f32)[None, None, :]
                    + nrm(ks[4], (DEPTH, S5_GROUPS, S5_STATE), 0.01))
    s5_log_step = jax.random.uniform(ks[5], (DEPTH, S5_GROUPS), f32,
                                     math.log(S5_DT_MIN), math.log(S5_DT_MAX))
    s5_b_re = nrm(ks[6], (DEPTH, S5_GROUPS, S5_STATE, S5_GROUP), (2 * S5_GROUP) ** -0.5)
    s5_b_im = nrm(ks[7], (DEPTH, S5_GROUPS, S5_STATE, S5_GROUP), (2 * S5_GROUP) ** -0.5)
    s5_c_re = nrm(ks[8], (DEPTH, S5_GROUPS, S5_GROUP, S5_STATE), (2 * S5_STATE) ** -0.5)
    s5_c_im = nrm(ks[9], (DEPTH, S5_GROUPS, S5_GROUP, S5_STATE), (2 * S5_STATE) ** -0.5)
    s5_d = nrm(ks[10], (DEPTH, W_MIX), 1.0)
    s5_w_glu = nrm(ks[11], (DEPTH, W_MIX, W_MIX), W_MIX ** -0.5)
    s5_b_glu = nrm(ks[12], (DEPTH, W_MIX), 0.01)
    rg_conv_w = nrm(ks[13], (DEPTH, RG_CONV, W_MIX), RG_CONV ** -0.5)
    rg_conv_b = nrm(ks[14], (DEPTH, W_MIX), 0.01)
    rg_w_a = nrm(ks[15], (DEPTH, RG_BLOCKS, RG_BLOCK, RG_BLOCK), RG_BLOCK ** -0.5)
    rg_b_a = nrm(ks[16], (DEPTH, W_MIX), 0.01)
    rg_w_x = nrm(ks[17], (DEPTH, RG_BLOCKS, RG_BLOCK, RG_BLOCK), RG_BLOCK ** -0.5)
    rg_b_x = nrm(ks[18], (DEPTH, W_MIX), 0.01)
    a0 = jax.random.uniform(ks[19], (DEPTH, W_MIX), f32, 0.9, 0.999)
    p = a0 ** (1.0 / RG_C)
    rg_lambda = jnp.log(p) - jnp.log1p(-p)
    hg_lower_bounds = nrm(ks[20], (DEPTH, W_MIX), 0.1)
    hg_norm_w = 1.0 + nrm(ks[21], (DEPTH, W_MIX), 0.02)
    w_branch = nrm(ks[22], (DEPTH, N_BRANCH, W_MIX, D_MODEL), W_MIX ** -0.5)
    w_out = nrm(ks[23], (DEPTH, D_MODEL, D_MODEL), D_MODEL ** -0.5)
    final_norm_w = 1.0 + nrm(ks[24], (D_MODEL,), 0.02)
    return {"x": x, "norm_w": norm_w, "w_in": w_in,
            "s5_lambda_re": s5_lambda_re, "s5_lambda_im": s5_lambda_im,
            "s5_log_step": s5_log_step, "s5_b_re": s5_b_re, "s5_b_im": s5_b_im,
            "s5_c_re": s5_c_re, "s5_c_im": s5_c_im, "s5_d": s5_d,
            "s5_w_glu": s5_w_glu, "s5_b_glu": s5_b_glu,
            "rg_conv_w": rg_conv_w, "rg_conv_b": rg_conv_b,
            "rg_w_a": rg_w_a, "rg_b_a": rg_b_a, "rg_w_x": rg_w_x, "rg_b_x": rg_b_x,
            "rg_lambda": rg_lambda,
            "hg_lower_bounds": hg_lower_bounds, "hg_norm_w": hg_norm_w,
            "w_branch": w_branch, "w_out": w_out, "final_norm_w": final_norm_w}


def _fwd_reference(x, norm_w, w_in, s5_lambda_re, s5_lambda_im, s5_log_step, s5_b_re, s5_b_im,
              s5_c_re, s5_c_im, s5_d, s5_w_glu, s5_b_glu, rg_conv_w, rg_conv_b,
              rg_w_a, rg_b_a, rg_w_x, rg_b_x, rg_lambda, hg_lower_bounds, hg_norm_w,
              w_branch, w_out, final_norm_w):
    bsz, s, _ = x.shape
    lb_sm = jax.nn.softmax(hg_lower_bounds.astype(jnp.float32), axis=0)
    lbs = jnp.cumsum(lb_sm, axis=0) - lb_sm[0]
    for l in range(DEPTH):
        h = rms_norm(x, norm_w[l])
        z = h @ w_in[l]
        u_a, g_a, x_b, g_b, q_c, f_c, i_c, g_c, gate_logits = jnp.split(z, SPLITS, axis=-1)
        y_a = s5_mixer(u_a, s5_lambda_re[l], s5_lambda_im[l], s5_log_step[l],
                       s5_b_re[l], s5_b_im[l], s5_c_re[l], s5_c_im[l], s5_d[l],
                       s5_w_glu[l], s5_b_glu[l]) * jax.nn.silu(g_a)
        y_b = rg_lru_mixer(x_b, rg_conv_w[l], rg_conv_b[l], rg_w_a[l], rg_b_a[l],
                           rg_w_x[l], rg_b_x[l], rg_lambda[l]) * jax.nn.silu(g_b)
        y_c = hgrn2_mixer(q_c, f_c, i_c, lbs[l], hg_norm_w[l]) * jax.nn.silu(g_c)
        ys = jnp.stack([y_a, y_b, y_c], axis=2)
        branch = jnp.einsum('bsnw,nwd->bsnd', ys, w_branch[l])
        gates = jax.nn.sigmoid(gate_logits.reshape(bsz, s, N_BRANCH, D_MODEL))
        merged = jnp.sum(gates * branch, axis=2)
        x = x + merged @ w_out[l]
    return rms_norm(x, final_norm_w)


import jax as _jax
import jax.numpy as _jnp

TWIN_FORMAT = 'train_step'
FWD_PARAMS = ['x', 'norm_w', 'w_in', 's5_lambda_re', 's5_lambda_im', 's5_log_step', 's5_b_re', 's5_b_im', 's5_c_re', 's5_c_im', 's5_d', 's5_w_glu', 's5_b_glu', 'rg_conv_w', 'rg_conv_b', 'rg_w_a', 'rg_b_a', 'rg_w_x', 'rg_b_x', 'rg_lambda', 'hg_lower_bounds', 'hg_norm_w', 'w_branch', 'w_out', 'final_norm_w']
TWIN_WEIGHTS = ['norm_w', 'w_in', 's5_lambda_re', 's5_lambda_im', 's5_log_step', 's5_b_re', 's5_b_im', 's5_c_re', 's5_c_im', 's5_d', 's5_w_glu', 's5_b_glu', 'rg_conv_w', 'rg_conv_b', 'rg_w_a', 'rg_b_a', 'rg_w_x', 'rg_b_x', 'rg_lambda', 'hg_lower_bounds', 'hg_norm_w', 'w_branch', 'w_out', 'final_norm_w']
TWIN_DIFF_INPUT = 'x'
TWIN_INPUTS = ['x', 'norm_w', 'w_in', 's5_lambda_re', 's5_lambda_im', 's5_log_step', 's5_b_re', 's5_b_im', 's5_c_re', 's5_c_im', 's5_d', 's5_w_glu', 's5_b_glu', 'rg_conv_w', 'rg_conv_b', 'rg_w_a', 'rg_b_a', 'rg_w_x', 'rg_b_x', 'rg_lambda', 'hg_lower_bounds', 'hg_norm_w', 'w_branch', 'w_out', 'final_norm_w', 'loss_target', 'm_norm_w', 'm_w_in', 'm_s5_lambda_re', 'm_s5_lambda_im', 'm_s5_log_step', 'm_s5_b_re', 'm_s5_b_im', 'm_s5_c_re', 'm_s5_c_im', 'm_s5_d', 'm_s5_w_glu', 'm_s5_b_glu', 'm_rg_conv_w', 'm_rg_conv_b', 'm_rg_w_a', 'm_rg_b_a', 'm_rg_w_x', 'm_rg_b_x', 'm_rg_lambda', 'm_hg_lower_bounds', 'm_hg_norm_w', 'm_w_branch', 'm_w_out', 'm_final_norm_w', 'v_norm_w', 'v_w_in', 'v_s5_lambda_re', 'v_s5_lambda_im', 'v_s5_log_step', 'v_s5_b_re', 'v_s5_b_im', 'v_s5_c_re', 'v_s5_c_im', 'v_s5_d', 'v_s5_w_glu', 'v_s5_b_glu', 'v_rg_conv_w', 'v_rg_conv_b', 'v_rg_w_a', 'v_rg_b_a', 'v_rg_w_x', 'v_rg_b_x', 'v_rg_lambda', 'v_hg_lower_bounds', 'v_hg_norm_w', 'v_w_branch', 'v_w_out', 'v_final_norm_w']
TWIN_OUTPUTS = ['loss', 'grad_x', 'grad_norm_w', 'grad_w_in', 'grad_s5_lambda_re', 'grad_s5_lambda_im', 'grad_s5_log_step', 'grad_s5_b_re', 'grad_s5_b_im', 'grad_s5_c_re', 'grad_s5_c_im', 'grad_s5_d', 'grad_s5_w_glu', 'grad_s5_b_glu', 'grad_rg_conv_w', 'grad_rg_conv_b', 'grad_rg_w_a', 'grad_rg_b_a', 'grad_rg_w_x', 'grad_rg_b_x', 'grad_rg_lambda', 'grad_hg_lower_bounds', 'grad_hg_norm_w', 'grad_w_branch', 'grad_w_out', 'grad_final_norm_w', 'delta_norm_w', 'delta_w_in', 'delta_s5_lambda_re', 'delta_s5_lambda_im', 'delta_s5_log_step', 'delta_s5_b_re', 'delta_s5_b_im', 'delta_s5_c_re', 'delta_s5_c_im', 'delta_s5_d', 'delta_s5_w_glu', 'delta_s5_b_glu', 'delta_rg_conv_w', 'delta_rg_conv_b', 'delta_rg_w_a', 'delta_rg_b_a', 'delta_rg_w_x', 'delta_rg_b_x', 'delta_rg_lambda', 'delta_hg_lower_bounds', 'delta_hg_norm_w', 'delta_w_branch', 'delta_w_out', 'delta_final_norm_w', 'new_m_norm_w', 'new_m_w_in', 'new_m_s5_lambda_re', 'new_m_s5_lambda_im', 'new_m_s5_log_step', 'new_m_s5_b_re', 'new_m_s5_b_im', 'new_m_s5_c_re', 'new_m_s5_c_im', 'new_m_s5_d', 'new_m_s5_w_glu', 'new_m_s5_b_glu', 'new_m_rg_conv_w', 'new_m_rg_conv_b', 'new_m_rg_w_a', 'new_m_rg_b_a', 'new_m_rg_w_x', 'new_m_rg_b_x', 'new_m_rg_lambda', 'new_m_hg_lower_bounds', 'new_m_hg_norm_w', 'new_m_w_branch', 'new_m_w_out', 'new_m_final_norm_w', 'new_v_norm_w', 'new_v_w_in', 'new_v_s5_lambda_re', 'new_v_s5_lambda_im', 'new_v_s5_log_step', 'new_v_s5_b_re', 'new_v_s5_b_im', 'new_v_s5_c_re', 'new_v_s5_c_im', 'new_v_s5_d', 'new_v_s5_w_glu', 'new_v_s5_b_glu', 'new_v_rg_conv_w', 'new_v_rg_conv_b', 'new_v_rg_w_a', 'new_v_rg_b_a', 'new_v_rg_w_x', 'new_v_rg_b_x', 'new_v_rg_lambda', 'new_v_hg_lower_bounds', 'new_v_hg_norm_w', 'new_v_w_branch', 'new_v_w_out', 'new_v_final_norm_w']
TWIN_LEAF_KINDS = {'loss': 'loss', 'grad_x': 'grad_x', 'grad_norm_w': 'grad_w', 'grad_w_in': 'grad_w', 'grad_s5_lambda_re': 'grad_w', 'grad_s5_lambda_im': 'grad_w', 'grad_s5_log_step': 'grad_w', 'grad_s5_b_re': 'grad_w', 'grad_s5_b_im': 'grad_w', 'grad_s5_c_re': 'grad_w', 'grad_s5_c_im': 'grad_w', 'grad_s5_d': 'grad_w', 'grad_s5_w_glu': 'grad_w', 'grad_s5_b_glu': 'grad_w', 'grad_rg_conv_w': 'grad_w', 'grad_rg_conv_b': 'grad_w', 'grad_rg_w_a': 'grad_w', 'grad_rg_b_a': 'grad_w', 'grad_rg_w_x': 'grad_w', 'grad_rg_b_x': 'grad_w', 'grad_rg_lambda': 'grad_w', 'grad_hg_lower_bounds': 'grad_w', 'grad_hg_norm_w': 'grad_w', 'grad_w_branch': 'grad_w', 'grad_w_out': 'grad_w', 'grad_final_norm_w': 'grad_w', 'delta_norm_w': 'delta_w', 'delta_w_in': 'delta_w', 'delta_s5_lambda_re': 'delta_w', 'delta_s5_lambda_im': 'delta_w', 'delta_s5_log_step': 'delta_w', 'delta_s5_b_re': 'delta_w', 'delta_s5_b_im': 'delta_w', 'delta_s5_c_re': 'delta_w', 'delta_s5_c_im': 'delta_w', 'delta_s5_d': 'delta_w', 'delta_s5_w_glu': 'delta_w', 'delta_s5_b_glu': 'delta_w', 'delta_rg_conv_w': 'delta_w', 'delta_rg_conv_b': 'delta_w', 'delta_rg_w_a': 'delta_w', 'delta_rg_b_a': 'delta_w', 'delta_rg_w_x': 'delta_w', 'delta_rg_b_x': 'delta_w', 'delta_rg_lambda': 'delta_w', 'delta_hg_lower_bounds': 'delta_w', 'delta_hg_norm_w': 'delta_w', 'delta_w_branch': 'delta_w', 'delta_w_out': 'delta_w', 'delta_final_norm_w': 'delta_w', 'new_m_norm_w': 'new_m', 'new_m_w_in': 'new_m', 'new_m_s5_lambda_re': 'new_m', 'new_m_s5_lambda_im': 'new_m', 'new_m_s5_log_step': 'new_m', 'new_m_s5_b_re': 'new_m', 'new_m_s5_b_im': 'new_m', 'new_m_s5_c_re': 'new_m', 'new_m_s5_c_im': 'new_m', 'new_m_s5_d': 'new_m', 'new_m_s5_w_glu': 'new_m', 'new_m_s5_b_glu': 'new_m', 'new_m_rg_conv_w': 'new_m', 'new_m_rg_conv_b': 'new_m', 'new_m_rg_w_a': 'new_m', 'new_m_rg_b_a': 'new_m', 'new_m_rg_w_x': 'new_m', 'new_m_rg_b_x': 'new_m', 'new_m_rg_lambda': 'new_m', 'new_m_hg_lower_bounds': 'new_m', 'new_m_hg_norm_w': 'new_m', 'new_m_w_branch': 'new_m', 'new_m_w_out': 'new_m', 'new_m_final_norm_w': 'new_m', 'new_v_norm_w': 'new_v', 'new_v_w_in': 'new_v', 'new_v_s5_lambda_re': 'new_v', 'new_v_s5_lambda_im': 'new_v', 'new_v_s5_log_step': 'new_v', 'new_v_s5_b_re': 'new_v', 'new_v_s5_b_im': 'new_v', 'new_v_s5_c_re': 'new_v', 'new_v_s5_c_im': 'new_v', 'new_v_s5_d': 'new_v', 'new_v_s5_w_glu': 'new_v', 'new_v_s5_b_glu': 'new_v', 'new_v_rg_conv_w': 'new_v', 'new_v_rg_conv_b': 'new_v', 'new_v_rg_w_a': 'new_v', 'new_v_rg_b_a': 'new_v', 'new_v_rg_w_x': 'new_v', 'new_v_rg_b_x': 'new_v', 'new_v_rg_lambda': 'new_v', 'new_v_hg_lower_bounds': 'new_v', 'new_v_hg_norm_w': 'new_v', 'new_v_w_branch': 'new_v', 'new_v_w_out': 'new_v', 'new_v_final_norm_w': 'new_v'}


def _forward(args):
    return _fwd_reference(*[args[k] for k in FWD_PARAMS])


def _output_shape():
    def fwd():
        inp = _fwd_setup_inputs(0)
        return _fwd_reference(*[inp[k] for k in FWD_PARAMS])
    out = _jax.eval_shape(fwd)
    return out.shape, out.dtype

N_MICROBATCH = 1
ADAM_LR = 0.001
ADAM_B1 = 0.9
ADAM_B2 = 0.999
ADAM_EPS = 1e-08
ADAM_WD = 0.01
ADAM_STEP = 10
PER_EXAMPLE_BATCH_AXIS = {'x': 0, 'loss_target': 0}
SHARED_INPUTS = []
_WEIGHT_DTYPES = {'norm_w': _jnp.float32, 'w_in': _jnp.float32, 's5_lambda_re': _jnp.float32, 's5_lambda_im': _jnp.float32, 's5_log_step': _jnp.float32, 's5_b_re': _jnp.float32, 's5_b_im': _jnp.float32, 's5_c_re': _jnp.float32, 's5_c_im': _jnp.float32, 's5_d': _jnp.float32, 's5_w_glu': _jnp.float32, 's5_b_glu': _jnp.float32, 'rg_conv_w': _jnp.float32, 'rg_conv_b': _jnp.float32, 'rg_w_a': _jnp.float32, 'rg_b_a': _jnp.float32, 'rg_w_x': _jnp.float32, 'rg_b_x': _jnp.float32, 'rg_lambda': _jnp.float32, 'hg_lower_bounds': _jnp.float32, 'hg_norm_w': _jnp.float32, 'w_branch': _jnp.float32, 'w_out': _jnp.float32, 'final_norm_w': _jnp.float32}
MOMENT_SCALE = {'norm_w': 5.188455e-02, 'w_in': 2.002152e-02, 's5_lambda_re': 6.862910e-04, 's5_lambda_im': 7.193099e-04, 's5_log_step': 4.610819e-01, 's5_b_re': 4.578739e-04, 's5_b_im': 4.571106e-04, 's5_c_re': 9.099459e-04, 's5_c_im': 9.299700e-04, 's5_d': 1.474116e-02, 's5_w_glu': 3.954524e-03, 's5_b_glu': 6.107628e-03, 'rg_conv_w': 3.348404e-02, 'rg_conv_b': 3.202397e-01, 'rg_w_a': 1.205590e-02, 'rg_b_a': 8.931847e-03, 'rg_w_x': 2.178469e-02, 'rg_b_x': 1.182167e-02, 'rg_lambda': 1.633997e-02, 'hg_lower_bounds': 3.054531e-03, 'hg_norm_w': 3.974045e-02, 'w_branch': 2.154594e-02, 'w_out': 3.742686e-02, 'final_norm_w': 1.600680e+01}


def _to_microbatches(a, axis):
    t = _jnp.moveaxis(a, axis, 0)
    t = t.reshape((N_MICROBATCH, t.shape[0] // N_MICROBATCH) + t.shape[1:])
    return _jnp.moveaxis(t, 1, axis + 1)


def setup_inputs(seed: int = 0) -> dict:
    inp = _fwd_setup_inputs(seed)
    key = _jax.random.fold_in(_jax.random.key(seed), 7919)
    shape, _ = _output_shape()
    out = dict(inp)
    out["loss_target"] = _jax.random.normal(_jax.random.fold_in(key, 0), shape, _jnp.float32)
    for i, name in enumerate(TWIN_WEIGHTS):
        w = inp[name].astype(_jnp.float32)
        if MOMENT_SCALE is None:
            s = _jnp.sqrt(_jnp.mean(_jnp.square(w)) + 1e-30)
        else:
            s = MOMENT_SCALE[name]
        km, kv = _jax.random.split(_jax.random.fold_in(key, i + 1))
        out[name] = w
        out["m_" + name] = s * _jax.random.normal(km, w.shape, _jnp.float32)
        out["v_" + name] = (s * s) * _jax.random.uniform(kv, w.shape, _jnp.float32, 0.5, 1.5)
    if N_MICROBATCH > 1:
        for name, axis in PER_EXAMPLE_BATCH_AXIS.items():
            out[name] = _to_microbatches(out[name], axis)
    return {'x': out['x'], 'norm_w': out['norm_w'], 'w_in': out['w_in'], 's5_lambda_re': out['s5_lambda_re'], 's5_lambda_im': out['s5_lambda_im'], 's5_log_step': out['s5_log_step'], 's5_b_re': out['s5_b_re'], 's5_b_im': out['s5_b_im'], 's5_c_re': out['s5_c_re'], 's5_c_im': out['s5_c_im'], 's5_d': out['s5_d'], 's5_w_glu': out['s5_w_glu'], 's5_b_glu': out['s5_b_glu'], 'rg_conv_w': out['rg_conv_w'], 'rg_conv_b': out['rg_conv_b'], 'rg_w_a': out['rg_w_a'], 'rg_b_a': out['rg_b_a'], 'rg_w_x': out['rg_w_x'], 'rg_b_x': out['rg_b_x'], 'rg_lambda': out['rg_lambda'], 'hg_lower_bounds': out['hg_lower_bounds'], 'hg_norm_w': out['hg_norm_w'], 'w_branch': out['w_branch'], 'w_out': out['w_out'], 'final_norm_w': out['final_norm_w'], 'loss_target': out['loss_target'], 'm_norm_w': out['m_norm_w'], 'm_w_in': out['m_w_in'], 'm_s5_lambda_re': out['m_s5_lambda_re'], 'm_s5_lambda_im': out['m_s5_lambda_im'], 'm_s5_log_step': out['m_s5_log_step'], 'm_s5_b_re': out['m_s5_b_re'], 'm_s5_b_im': out['m_s5_b_im'], 'm_s5_c_re': out['m_s5_c_re'], 'm_s5_c_im': out['m_s5_c_im'], 'm_s5_d': out['m_s5_d'], 'm_s5_w_glu': out['m_s5_w_glu'], 'm_s5_b_glu': out['m_s5_b_glu'], 'm_rg_conv_w': out['m_rg_conv_w'], 'm_rg_conv_b': out['m_rg_conv_b'], 'm_rg_w_a': out['m_rg_w_a'], 'm_rg_b_a': out['m_rg_b_a'], 'm_rg_w_x': out['m_rg_w_x'], 'm_rg_b_x': out['m_rg_b_x'], 'm_rg_lambda': out['m_rg_lambda'], 'm_hg_lower_bounds': out['m_hg_lower_bounds'], 'm_hg_norm_w': out['m_hg_norm_w'], 'm_w_branch': out['m_w_branch'], 'm_w_out': out['m_w_out'], 'm_final_norm_w': out['m_final_norm_w'], 'v_norm_w': out['v_norm_w'], 'v_w_in': out['v_w_in'], 'v_s5_lambda_re': out['v_s5_lambda_re'], 'v_s5_lambda_im': out['v_s5_lambda_im'], 'v_s5_log_step': out['v_s5_log_step'], 'v_s5_b_re': out['v_s5_b_re'], 'v_s5_b_im': out['v_s5_b_im'], 'v_s5_c_re': out['v_s5_c_re'], 'v_s5_c_im': out['v_s5_c_im'], 'v_s5_d': out['v_s5_d'], 'v_s5_w_glu': out['v_s5_w_glu'], 'v_s5_b_glu': out['v_s5_b_glu'], 'v_rg_conv_w': out['v_rg_conv_w'], 'v_rg_conv_b': out['v_rg_conv_b'], 'v_rg_w_a': out['v_rg_w_a'], 'v_rg_b_a': out['v_rg_b_a'], 'v_rg_w_x': out['v_rg_w_x'], 'v_rg_b_x': out['v_rg_b_x'], 'v_rg_lambda': out['v_rg_lambda'], 'v_hg_lower_bounds': out['v_hg_lower_bounds'], 'v_hg_norm_w': out['v_hg_norm_w'], 'v_w_branch': out['v_w_branch'], 'v_w_out': out['v_w_out'], 'v_final_norm_w': out['v_final_norm_w']}


def _loss(weights, diff, rest, loss_target):
    with _jax.named_scope("forward"):
        args = {**rest, TWIN_DIFF_INPUT: diff, **{k: w.astype(_WEIGHT_DTYPES[k]) for k, w in weights.items()}}
        y = _forward(args)
    with _jax.named_scope("loss_head"):
        err = _jnp.square(y.astype(_jnp.float32) - loss_target)
        return 0.5 * _jnp.sum(_jnp.mean(err, axis=-1)) if err.ndim else 0.5 * err


def _adamw(w, g, m, v):
    m = ADAM_B1 * m + (1.0 - ADAM_B1) * g
    v = ADAM_B2 * v + (1.0 - ADAM_B2) * _jnp.square(g)
    m_hat = m / (1.0 - ADAM_B1 ** ADAM_STEP)
    v_hat = v / (1.0 - ADAM_B2 ** ADAM_STEP)
    delta = -ADAM_LR * (m_hat / (_jnp.sqrt(v_hat) + ADAM_EPS) + ADAM_WD * w)
    return delta, m, v


def reference(x, norm_w, w_in, s5_lambda_re, s5_lambda_im, s5_log_step, s5_b_re, s5_b_im, s5_c_re, s5_c_im, s5_d, s5_w_glu, s5_b_glu, rg_conv_w, rg_conv_b, rg_w_a, rg_b_a, rg_w_x, rg_b_x, rg_lambda, hg_lower_bounds, hg_norm_w, w_branch, w_out, final_norm_w, loss_target, m_norm_w, m_w_in, m_s5_lambda_re, m_s5_lambda_im, m_s5_log_step, m_s5_b_re, m_s5_b_im, m_s5_c_re, m_s5_c_im, m_s5_d, m_s5_w_glu, m_s5_b_glu, m_rg_conv_w, m_rg_conv_b, m_rg_w_a, m_rg_b_a, m_rg_w_x, m_rg_b_x, m_rg_lambda, m_hg_lower_bounds, m_hg_norm_w, m_w_branch, m_w_out, m_final_norm_w, v_norm_w, v_w_in, v_s5_lambda_re, v_s5_lambda_im, v_s5_log_step, v_s5_b_re, v_s5_b_im, v_s5_c_re, v_s5_c_im, v_s5_d, v_s5_w_glu, v_s5_b_glu, v_rg_conv_w, v_rg_conv_b, v_rg_w_a, v_rg_b_a, v_rg_w_x, v_rg_b_x, v_rg_lambda, v_hg_lower_bounds, v_hg_norm_w, v_w_branch, v_w_out, v_final_norm_w):
    given = dict(x=x, norm_w=norm_w, w_in=w_in, s5_lambda_re=s5_lambda_re, s5_lambda_im=s5_lambda_im, s5_log_step=s5_log_step, s5_b_re=s5_b_re, s5_b_im=s5_b_im, s5_c_re=s5_c_re, s5_c_im=s5_c_im, s5_d=s5_d, s5_w_glu=s5_w_glu, s5_b_glu=s5_b_glu, rg_conv_w=rg_conv_w, rg_conv_b=rg_conv_b, rg_w_a=rg_w_a, rg_b_a=rg_b_a, rg_w_x=rg_w_x, rg_b_x=rg_b_x, rg_lambda=rg_lambda, hg_lower_bounds=hg_lower_bounds, hg_norm_w=hg_norm_w, w_branch=w_branch, w_out=w_out, final_norm_w=final_norm_w, loss_target=loss_target, m_norm_w=m_norm_w, m_w_in=m_w_in, m_s5_lambda_re=m_s5_lambda_re, m_s5_lambda_im=m_s5_lambda_im, m_s5_log_step=m_s5_log_step, m_s5_b_re=m_s5_b_re, m_s5_b_im=m_s5_b_im, m_s5_c_re=m_s5_c_re, m_s5_c_im=m_s5_c_im, m_s5_d=m_s5_d, m_s5_w_glu=m_s5_w_glu, m_s5_b_glu=m_s5_b_glu, m_rg_conv_w=m_rg_conv_w, m_rg_conv_b=m_rg_conv_b, m_rg_w_a=m_rg_w_a, m_rg_b_a=m_rg_b_a, m_rg_w_x=m_rg_w_x, m_rg_b_x=m_rg_b_x, m_rg_lambda=m_rg_lambda, m_hg_lower_bounds=m_hg_lower_bounds, m_hg_norm_w=m_hg_norm_w, m_w_branch=m_w_branch, m_w_out=m_w_out, m_final_norm_w=m_final_norm_w, v_norm_w=v_norm_w, v_w_in=v_w_in, v_s5_lambda_re=v_s5_lambda_re, v_s5_lambda_im=v_s5_lambda_im, v_s5_log_step=v_s5_log_step, v_s5_b_re=v_s5_b_re, v_s5_b_im=v_s5_b_im, v_s5_c_re=v_s5_c_re, v_s5_c_im=v_s5_c_im, v_s5_d=v_s5_d, v_s5_w_glu=v_s5_w_glu, v_s5_b_glu=v_s5_b_glu, v_rg_conv_w=v_rg_conv_w, v_rg_conv_b=v_rg_conv_b, v_rg_w_a=v_rg_w_a, v_rg_b_a=v_rg_b_a, v_rg_w_x=v_rg_w_x, v_rg_b_x=v_rg_b_x, v_rg_lambda=v_rg_lambda, v_hg_lower_bounds=v_hg_lower_bounds, v_hg_norm_w=v_hg_norm_w, v_w_branch=v_w_branch, v_w_out=v_w_out, v_final_norm_w=v_final_norm_w)
    weights = {n: given[n] for n in TWIN_WEIGHTS}
    shared = {n: given[n] for n in SHARED_INPUTS}
    per_example = {n: given[n] for n in ['x']}
    grad_fn = _jax.value_and_grad(_loss, argnums=(0, 1))

    def one_microbatch(ex, loss_target):
        ex = dict(ex)
        diff = ex.pop(TWIN_DIFF_INPUT)
        return grad_fn(weights, diff, {**shared, **ex}, loss_target)

    if N_MICROBATCH == 1:
        loss, (grad_w, grad_x) = one_microbatch(per_example, given["loss_target"])
    else:
        def body(carry, xs):
            loss_sum, grad_sum = carry
            l_k, (gw_k, gx_k) = one_microbatch(xs[0], xs[1])
            with _jax.named_scope("update"):
                return (loss_sum + l_k, _jax.tree.map(_jnp.add, grad_sum, gw_k)), gx_k

        init = (_jnp.zeros((), _jnp.float32), _jax.tree.map(_jnp.zeros_like, weights))
        (loss, grad_w), grad_x = _jax.lax.scan(body, init, (per_example, given["loss_target"]))
    with _jax.named_scope("update"):
        delta_w, new_m, new_v = {}, {}, {}
        for n in TWIN_WEIGHTS:
            delta_w[n], new_m[n], new_v[n] = _adamw(weights[n], grad_w[n], given["m_" + n], given["v_" + n])
    return (loss, grad_x, *[grad_w[n] for n in TWIN_WEIGHTS], *[delta_w[n] for n in TWIN_WEIGHTS],
            *[new_m[n] for n in TWIN_WEIGHTS], *[new_v[n] for n in TWIN_WEIGHTS])
```

```python
import functools
import math

import jax
import jax.numpy as jnp
from jax import lax
from jax.experimental import pallas as pl
from jax.experimental.pallas import tpu as pltpu

F32 = jnp.float32
BF16 = jnp.bfloat16
EPS = 1e-6
RG_C = 8.0
S5_GROUP = 16
S5_STATE = 64
RG_BLOCKS = 16
RG_CONV = 4
HG_HEADS = 8
N_CHIPS = 4
VMEM_LIMIT_BYTES = 56 * 1024 * 1024
NEG_BIG = -1e30

ADAM_LR = 0.001
ADAM_B1 = 0.9
ADAM_B2 = 0.999
ADAM_EPS = 1e-08
ADAM_WD = 0.01
ADAM_STEP = 10

MESH = pl.DeviceIdType.MESH


def _cparams(sem):
    return pltpu.CompilerParams(dimension_semantics=sem, vmem_limit_bytes=VMEM_LIMIT_BYTES)


_DOT_DIMS = {"nn": (((1,), (0,)), ((), ())), "nt": (((1,), (1,)), ((), ())), "tn": (((0,), (0,)), ((), ()))}


def _bdot_raw(a, b, form):
    return lax.dot_general(a.astype(BF16), b.astype(BF16), _DOT_DIMS[form], preferred_element_type=F32)


@functools.partial(jax.custom_vjp, nondiff_argnums=(2,))
def bdot(a, b, form):
    return _bdot_raw(a, b, form)


def _bdot_fwd(a, b, form):
    return _bdot_raw(a, b, form), (a, b)


def _bdot_bwd(form, res, g):
    a, b = res
    if form == "nn":
        da, db = _bdot_raw(g, b, "nt"), _bdot_raw(a, g, "tn")
    elif form == "nt":
        da, db = _bdot_raw(g, b, "nn"), _bdot_raw(g, a, "tn")
    else:
        da, db = _bdot_raw(b, g, "nt"), _bdot_raw(a, g, "nn")
    return da.astype(a.dtype), db.astype(b.dtype)


bdot.defvjp(_bdot_fwd, _bdot_bwd)


@functools.partial(jax.custom_vjp, nondiff_argnums=(1,))
def sroll(x, d):
    return pltpu.roll(x, d, 0)


def _sroll_fwd(x, d):
    return pltpu.roll(x, d, 0), None


def _sroll_bwd(d, _, g):
    return (pltpu.roll(g, g.shape[0] - d, 0),)


sroll.defvjp(_sroll_fwd, _sroll_bwd)


def _row_iota(n):
    return lax.broadcasted_iota(jnp.int32, (n, 1), 0)


def _last_row(x, ti):
    return jnp.sum(jnp.where(ti == x.shape[0] - 1, x, 0.0), axis=0, keepdims=True)


def _silu(x):
    return x * jax.nn.sigmoid(x)


def _tiled_specs(params, rows, carry_shapes, out_defs, CB, tT, nT, rev):
    tmap = (lambda t: nT - 1 - t) if rev else (lambda t: t)
    p_specs = [pl.BlockSpec((None,) + p.shape[1:], lambda cb, t: (cb, 0, 0)) for p in params]
    r_specs = [pl.BlockSpec((tT, r.shape[1] // CB), lambda cb, t: (tmap(t), cb)) for r in rows]
    o_specs = [pl.BlockSpec((tT, w // CB), lambda cb, t: (tmap(t), cb)) for (w, _) in out_defs]
    s_specs = [pl.BlockSpec((None, r, c), lambda cb, t: (tmap(t), 0, cb)) for (r, c) in carry_shapes]
    return p_specs, r_specs, o_specs, s_specs


def _tiled_fwd(f, name, params, rows, carry_shapes, out_defs, CB, tT):
    T = rows[0].shape[0]
    nT = T // tT
    n_p, n_r, n_o, n_c = len(params), len(rows), len(out_defs), len(carry_shapes)
    p_specs, r_specs, o_specs, s_specs = _tiled_specs(params, rows, carry_shapes, out_defs, CB, tT, nT, False)

    def body(*refs):
        p_refs = refs[:n_p]
        r_refs = refs[n_p:n_p + n_r]
        o_refs = refs[n_p + n_r:n_p + n_r + n_o]
        s_refs = refs[n_p + n_r + n_o:n_p + n_r + n_o + n_c]
        c_refs = refs[n_p + n_r + n_o + n_c:]
        t = pl.program_id(1)

        @pl.when(t == 0)
        def _():
            for c in c_refs:
                c[...] = jnp.zeros_like(c)

        carries = [c[...] for c in c_refs]
        for s, cv in zip(s_refs, carries):
            s[...] = cv
        outs, newc = f([p[...] for p in p_refs], [r[...] for r in r_refs], carries, t * tT)
        for o, v in zip(o_refs, outs):
            o[...] = v.astype(o.dtype)
        for c, v in zip(c_refs, newc):
            c[...] = v

    out_shape = [jax.ShapeDtypeStruct((T, w), dt) for (w, dt) in out_defs]
    out_shape += [jax.ShapeDtypeStruct((nT, r, c * CB), F32) for (r, c) in carry_shapes]
    res = pl.pallas_call(
        body, name=name + "_fwd", grid=(CB, nT),
        in_specs=p_specs + r_specs, out_specs=o_specs + s_specs, out_shape=out_shape,
        scratch_shapes=[pltpu.VMEM((r, c), F32) for (r, c) in carry_shapes],
        compiler_params=_cparams(("arbitrary", "arbitrary")),
    )(*params, *rows)
    return list(res[:n_o]), list(res[n_o:])


def _tiled_bwd(f, name, params, rows, saved, douts, carry_shapes, out_defs, CB, tT):
    T = rows[0].shape[0]
    nT = T // tT
    n_p, n_r, n_o, n_c = len(params), len(rows), len(out_defs), len(carry_shapes)
    p_specs, r_specs, o_specs, s_specs = _tiled_specs(params, rows, carry_shapes, out_defs, CB, tT, nT, True)
    out_dtypes = [dt for (_, dt) in out_defs]

    def body(*refs):
        i = 0
        p_refs = refs[i:i + n_p]; i += n_p
        r_refs = refs[i:i + n_r]; i += n_r
        s_refs = refs[i:i + n_c]; i += n_c
        g_refs = refs[i:i + n_o]; i += n_o
        dp_refs = refs[i:i + n_p]; i += n_p
        dr_refs = refs[i:i + n_r]; i += n_r
        dc_refs = refs[i:]
        t = pl.program_id(1)

        @pl.when(t == 0)
        def _():
            for c in dc_refs:
                c[...] = jnp.zeros_like(c)
            for d in dp_refs:
                d[...] = jnp.zeros_like(d)

        t0 = (nT - 1 - t) * tT

        def g(P, R, C):
            outs, newc = f(P, R, C, t0)
            return [o.astype(dt) for o, dt in zip(outs, out_dtypes)], list(newc)

        _, vjp = jax.vjp(g, [p[...] for p in p_refs], [r[...] for r in r_refs], [s[...] for s in s_refs])
        dP, dR, dC = vjp(([gr[...] for gr in g_refs], [c[...] for c in dc_refs]))
        for d, v in zip(dp_refs, dP):
            d[...] += v
        for d, v in zip(dr_refs, dR):
            d[...] = v.astype(d.dtype)
        for c, v in zip(dc_refs, dC):
            c[...] = v

    out_shape = [jax.ShapeDtypeStruct(p.shape, F32) for p in params]
    out_shape += [jax.ShapeDtypeStruct(r.shape, r.dtype) for r in rows]
    res = pl.pallas_call(
        body, name=name + "_bwd", grid=(CB, nT),
        in_specs=p_specs + r_specs + s_specs + o_specs, out_specs=p_specs + r_specs, out_shape=out_shape,
        scratch_shapes=[pltpu.VMEM((r, c), F32) for (r, c) in carry_shapes],
        compiler_params=_cparams(("arbitrary", "arbitrary")),
    )(*params, *rows, *saved, *douts)
    return list(res[:n_p]), list(res[n_p:])


def tiled_op(f, name, params, rows, carry_shapes, out_defs, CB, tT):
    @jax.custom_vjp
    def op(params, rows):
        return _tiled_fwd(f, name, params, rows, carry_shapes, out_defs, CB, tT)[0]

    def op_fwd(params, rows):
        outs, saved = _tiled_fwd(f, name, params, rows, carry_shapes, out_defs, CB, tT)
        return outs, (params, rows, saved)

    def op_bwd(res, douts):
        params, rows, saved = res
        dP, dR = _tiled_bwd(f, name, params, rows, saved, list(douts), carry_shapes, out_defs, CB, tT)
        return dP, dR

    op.defvjp(op_fwd, op_bwd)
    return op(list(params), list(rows))


def _pick(n, pref):
    for t in pref:
        if n % t == 0:
            return t
    return n


def _mm_nn(a, w, res, name):
    M, K = a.shape
    S, _, Ns = w.shape
    tm = _pick(M, (1024, 512, 256, 128))
    tn = _pick(Ns, (512, 256, 128))
    nps = Ns // tn
    has_res = res is not None

    def body(*refs):
        if has_res:
            a_ref, w_ref, r_ref, o_ref = refs
        else:
            a_ref, w_ref, o_ref = refs
        acc = _bdot_raw(a_ref[...], w_ref[...], "nn")
        if has_res:
            acc = acc + r_ref[...]
        o_ref[...] = acc

    in_specs = [pl.BlockSpec((tm, K), lambda i, j: (i, 0)),
                pl.BlockSpec((None, K, tn), lambda i, j: (j // nps, 0, j % nps))]
    args = [a, w]
    if has_res:
        in_specs.append(pl.BlockSpec((tm, tn), lambda i, j: (i, j)))
        args.append(res)
    return pl.pallas_call(
        body, name=name, grid=(M // tm, S * nps), in_specs=in_specs,
        out_specs=pl.BlockSpec((tm, tn), lambda i, j: (i, j)),
        out_shape=jax.ShapeDtypeStruct((M, S * Ns), F32),
        compiler_params=_cparams(("parallel", "arbitrary")),
    )(*args)


def _mm_nt(g, w, out_dtype, name):
    M, N = g.shape
    S, K, Ns = w.shape
    tm = _pick(M, (1024, 512, 256, 128))
    tk = _pick(K, (1024, 512, 256, 128))
    tn = _pick(Ns, (512, 256, 128))
    nps = Ns // tn
    nn = S * nps

    def body(g_ref, w_ref, o_ref, acc_ref):
        n = pl.program_id(2)

        @pl.when(n == 0)
        def _():
            acc_ref[...] = jnp.zeros_like(acc_ref)

        acc_ref[...] += _bdot_raw(g_ref[...], w_ref[...], "nt")

        @pl.when(n == nn - 1)
        def _():
            o_ref[...] = acc_ref[...].astype(o_ref.dtype)

    return pl.pallas_call(
        body, name=name, grid=(M // tm, K // tk, nn),
        in_specs=[pl.BlockSpec((tm, tn), lambda i, k, n: (i, n)),
                  pl.BlockSpec((None, tk, tn), lambda i, k, n: (n // nps, k, n % nps))],
        out_specs=pl.BlockSpec((tm, tk), lambda i, k, n: (i, k)),
        out_shape=jax.ShapeDtypeStruct((M, K), out_dtype),
        scratch_shapes=[pltpu.VMEM((tm, tk), F32)],
        compiler_params=_cparams(("parallel", "parallel", "arbitrary")),
    )(g, w)


def _mm_tn(a, g, S, out_dtype, name):
    T, K = a.shape
    N = g.shape[1]
    Ns = N // S
    tk = _pick(K, (1024, 512, 256, 128))
    tn = _pick(Ns, (1024, 512, 256, 128))
    tt = _pick(T, (512, 256, 128))
    nps = Ns // tn
    nt = T // tt

    def body(a_ref, g_ref, o_ref, acc_ref):
        t = pl.program_id(2)

        @pl.when(t == 0)
        def _():
            acc_ref[...] = jnp.zeros_like(acc_ref)

        acc_ref[...] += _bdot_raw(a_ref[...], g_ref[...], "tn")

        @pl.when(t == nt - 1)
        def _():
            o_ref[...] = acc_ref[...].astype(o_ref.dtype)

    return pl.pallas_call(
        body, name=name, grid=(K // tk, S * nps, nt),
        in_specs=[pl.BlockSpec((tt, tk), lambda k, j, t: (t, k)),
                  pl.BlockSpec((tt, tn), lambda k, j, t: (t, j))],
        out_specs=pl.BlockSpec((None, tk, tn), lambda k, j, t: (j // nps, k, j % nps)),
        out_shape=jax.ShapeDtypeStruct((S, K, Ns), out_dtype),
        scratch_shapes=[pltpu.VMEM((tk, tn), F32)],
        compiler_params=_cparams(("parallel", "parallel", "arbitrary")),
    )(a, g)


def linear(a, w, name, res=None):
    @jax.custom_vjp
    def op(a, w, res):
        return _mm_nn(a, w, res, name + "_fwd")

    def op_fwd(a, w, res):
        return _mm_nn(a, w, res, name + "_fwd"), (a, w)

    def op_bwd(saved, g):
        a, w = saved
        da = _mm_nt(g, w, a.dtype, name + "_bwd_a")
        dw = _mm_tn(a, g, w.shape[0], w.dtype, name + "_bwd_w")
        return da, dw, (None if res is None else g)

    op.defvjp(op_fwd, op_bwd)
    return op(a, w, res)


def _rms_tile(params, rows, carries, t0):
    (w,), (x,) = params, rows
    y = x * lax.rsqrt(jnp.mean(x * x, axis=-1, keepdims=True) + EPS) * w
    return [y], []


def _s5_tile(params, rows, carries, t0):
    b_re, b_im, c_re, c_im, apow_re, apow_im, d = params
    (u,) = rows
    car_re, car_im = carries
    tT = u.shape[0]
    ti = _row_iota(tT)
    pi = _row_iota(apow_re.shape[0])
    a_re, a_im = _row_sel(apow_re, pi, 0), _row_sel(apow_im, pi, 0)
    x_re = bdot(u, b_re, "nn") + jnp.where(ti == 0, a_re * car_re - a_im * car_im, 0.0)
    x_im = bdot(u, b_im, "nn") + jnp.where(ti == 0, a_re * car_im + a_im * car_re, 0.0)
    k = 0
    while (1 << k) < tT:
        sh = 1 << k
        p_re, p_im = _row_sel(apow_re, pi, k), _row_sel(apow_im, pi, k)
        s_re, s_im = sroll(x_re, sh), sroll(x_im, sh)
        m = ti >= sh
        x_re, x_im = (x_re + jnp.where(m, p_re * s_re - p_im * s_im, 0.0),
                      x_im + jnp.where(m, p_re * s_im + p_im * s_re, 0.0))
        k += 1
    y = bdot(x_re, c_re, "nn") - bdot(x_im, c_im, "nn") + d * u
    y = jax.nn.gelu(y)
    return [y], [_last_row(x_re, ti), _last_row(x_im, ti)]


def _glu_tile(params, rows, carries, t0):
    (b,), (y, zg, ga) = params, rows
    return [y * jax.nn.sigmoid(zg + b) * _silu(ga)], []


def _neg_expm1(z):
    small = -(z * (1.0 + z * (0.5 + z * (1.0 / 6.0))))
    return jnp.where(z > -0.01, small, 1.0 - jnp.exp(z))


def _rg_tile(params, rows, carries, t0):
    conv_w, conv_b, w_a, b_a, w_x, b_x, sp = params
    x, gate = rows
    x_prev, h_prev = carries
    tT = x.shape[0]
    ti = _row_iota(tT)
    ci = _row_iota(RG_CONV)
    xc = _row_sel(conv_w, ci, RG_CONV - 1) * x + conv_b
    for k in range(1, RG_CONV):
        xs = jnp.where(ti >= k, sroll(x, k), sroll(x_prev, k))
        xc = xc + _row_sel(conv_w, ci, RG_CONV - 1 - k) * xs
    r = jax.nn.sigmoid(bdot(xc, w_a, "nn") + b_a)
    i = jax.nn.sigmoid(bdot(xc, w_x, "nn") + b_x)
    log_a = -RG_C * r * sp
    a = jnp.exp(log_a)
    mult = jnp.sqrt(_neg_expm1(2.0 * log_a))
    mult = jnp.where(ti + t0 == 0, 1.0, mult)
    b = mult * (i * xc)
    b = b + jnp.where(ti == 0, a * h_prev, 0.0)
    k = 1
    while k < tT:
        m = ti >= k
        b = b + jnp.where(m, a * sroll(b, k), 0.0)
        a = jnp.where(m, a * sroll(a, k), a)
        k *= 2
    return [b * _silu(gate)], [x, _last_row(b, ti)]


def _hg_tile(params, rows, carries, t0):
    lb, nw = params
    q, fl, v, gate = rows
    (st,) = carries
    tT = q.shape[0]
    ti = _row_iota(tT)
    qs = _silu(q)
    f = lb + (1.0 - lb) * jax.nn.sigmoid(fl)
    kk = 1.0 - f
    G = jnp.log(f)
    k = 1
    while k < tT:
        G = G + jnp.where(ti >= k, sroll(G, k), 0.0)
        k *= 2
    inter = bdot(qs * jnp.exp(G), st, "nt")
    tr = lax.broadcasted_iota(jnp.int32, (tT, tT), 0)
    sc = lax.broadcasted_iota(jnp.int32, (tT, tT), 1)
    attn = jnp.zeros((tT, tT), F32)
    blk = tT
    while blk > 8:
        sub = blk // 4
        for j in range(1, 4):
            ref = jnp.zeros_like(G)
            for b in range(tT // blk):
                row = _row_sel(G, ti, b * blk + sub * j - 1)
                ref = ref + jnp.where(_div2(ti, blk) == b, row, 0.0)
            tmask = _div2(_mod2(ti, blk), sub) == j
            smask = _mod2(ti, blk) < sub * j
            qt = qs * jnp.exp(jnp.where(tmask, G - ref, NEG_BIG))
            kt = kk * jnp.exp(jnp.where(smask, ref - G, NEG_BIG))
            aj = bdot(qt, kt, "nt")
            attn = attn + jnp.where(_div2(tr, blk) == _div2(sc, blk), aj, 0.0)
        blk = sub
    intra = bdot(attn, v, "nn")
    for d in range(blk):
        if d == 0:
            kd, gd, vd = kk, G, v
        else:
            kd, gd, vd = sroll(kk, d), sroll(G, d), sroll(v, d)
        m = _mod2(ti, blk) >= d
        w = jnp.sum(qs * kd * jnp.exp(jnp.where(m, G - gd, NEG_BIG)), axis=1, keepdims=True)
        intra = intra + w * vd
    o = inter + intra
    g_last = _last_row(G, ti)
    k_dec = kk * jnp.exp(g_last - G)
    st_new = st * jnp.exp(g_last) + bdot(v, k_dec, "tn")
    o = o * lax.rsqrt(jnp.mean(o * o, axis=-1, keepdims=True) + EPS) * nw
    return [o * _silu(gate)], [st_new]


def _row_sel(x, ti, r):
    return jnp.sum(jnp.where(ti == r, x, 0.0), axis=0, keepdims=True)


def _div2(i, p):
    return lax.shift_right_logical(i, jnp.int32(p.bit_length() - 1))


def _mod2(i, p):
    return lax.bitwise_and(i, jnp.int32(p - 1))


def _merge_tile(params, rows, carries, t0):
    b0, b1, b2, g0, g1, g2 = rows
    m = jax.nn.sigmoid(g0) * b0 + jax.nn.sigmoid(g1) * b1 + jax.nn.sigmoid(g2) * b2
    return [m], []


def _loss_tile(params, rows, carries, t0):
    (w,), (x, tgt) = params, rows
    y = x * lax.rsqrt(jnp.mean(x * x, axis=-1, keepdims=True) + EPS) * w
    e = y - tgt
    return [0.5 * jnp.mean(e * e, axis=-1, keepdims=True)], []


def _block_diag(w, cb):
    n, i, j = w.shape
    g = n // cb
    w4 = w.reshape(cb, g, i, j)
    eye = jnp.eye(g, dtype=w.dtype)
    return jnp.einsum("cgij,gk->cgikj", w4, eye).reshape(cb, g * i, g * j)


def _s5_params(lam_re, lam_im, log_step, b_re, b_im, c_re, c_im, d, cb, levels):
    G, P = lam_re.shape
    step = jnp.exp(log_step)[:, None]
    mag = jnp.exp(lam_re * step)
    ang = lam_im * step
    abar_re = mag * jnp.cos(ang)
    abar_im = mag * jnp.sin(ang)
    num_re = abar_re - 1.0
    num_im = abar_im
    den = lam_re * lam_re + lam_im * lam_im
    coef_re = (num_re * lam_re + num_im * lam_im) / den
    coef_im = (num_im * lam_re - num_re * lam_im) / den
    bbar_re = coef_re[..., None] * b_re - coef_im[..., None] * b_im
    bbar_im = coef_re[..., None] * b_im + coef_im[..., None] * b_re
    bb_re = _block_diag(jnp.swapaxes(bbar_re, 1, 2), cb)
    bb_im = _block_diag(jnp.swapaxes(bbar_im, 1, 2), cb)
    cc_re = _block_diag(jnp.swapaxes(c_re, 1, 2), cb)
    cc_im = _block_diag(jnp.swapaxes(c_im, 1, 2), cb)
    pw_re, pw_im = [abar_re], [abar_im]
    for _ in range(levels - 1):
        r, i = pw_re[-1], pw_im[-1]
        pw_re.append(r * r - i * i)
        pw_im.append(2.0 * r * i)
    rows = max(8, levels)
    pad = [jnp.zeros_like(abar_re)] * (rows - levels)
    apow_re = jnp.stack(pw_re + pad, 0).reshape(rows, cb, (G // cb) * P).transpose(1, 0, 2)
    apow_im = jnp.stack(pw_im + pad, 0).reshape(rows, cb, (G // cb) * P).transpose(1, 0, 2)
    return [bb_re, bb_im, cc_re, cc_im, apow_re, apow_im, d.reshape(cb, 1, -1)]


def _vec(v, cb):
    return v.reshape(cb, 1, -1)


S5_CB, S5_TT = 4, 128
RG_CB, RG_TT = 4, 256
HG_TT = 128
ROW_TT = 256
MERGE_TT = 128


def _forward_loss(wts, x, target):
    T, D = x.shape
    L = wts["norm_w"].shape[0]
    W = wts["s5_d"].shape[1]
    lb_sm = jax.nn.softmax(wts["hg_lower_bounds"], axis=0)
    lbs = jnp.cumsum(lb_sm, axis=0) - lb_sm[0]
    row_tt = min(ROW_TT, T)
    s5_tt, rg_tt, hg_tt = min(S5_TT, T), min(RG_TT, T), min(HG_TT, T)
    for l in range(L):
        (h,) = tiled_op(_rms_tile, "rms", [wts["norm_w"][l].reshape(1, 1, D)], [x], [], [(D, BF16)], 1, row_tt)
        z = linear(h, wts["w_in"][l], "w_in")
        zs = [z[:, k * W:(k + 1) * W] for k in range(8)]
        gl = [z[:, 8 * W + n * D:8 * W + (n + 1) * D] for n in range(3)]
        u_a, g_a, x_b, g_b, q_c, f_c, i_c, g_c = zs
        levels = int(math.log2(s5_tt))
        s5p = _s5_params(wts["s5_lambda_re"][l], wts["s5_lambda_im"][l], wts["s5_log_step"][l],
                         wts["s5_b_re"][l], wts["s5_b_im"][l], wts["s5_c_re"][l], wts["s5_c_im"][l],
                         wts["s5_d"][l], S5_CB, levels)
        sc = (W // S5_GROUP) * S5_STATE // S5_CB
        (y1,) = tiled_op(_s5_tile, "s5", s5p, [u_a], [(1, sc), (1, sc)], [(W, F32)], S5_CB, s5_tt)
        zg = linear(y1, wts["s5_w_glu"][l].reshape(1, W, W), "w_glu")
        (y_a,) = tiled_op(_glu_tile, "glu", [wts["s5_b_glu"][l].reshape(1, 1, W)], [y1, zg, g_a], [],
                          [(W, BF16)], 1, row_tt)
        rgp = [wts["rg_conv_w"][l].reshape(RG_CONV, RG_CB, W // RG_CB).transpose(1, 0, 2),
               _vec(wts["rg_conv_b"][l], RG_CB), _block_diag(wts["rg_w_a"][l], RG_CB), _vec(wts["rg_b_a"][l], RG_CB),
               _block_diag(wts["rg_w_x"][l], RG_CB), _vec(wts["rg_b_x"][l], RG_CB), _vec(jax.nn.softplus(-wts["rg_lambda"][l]), RG_CB)]
        rc = W // RG_CB
        (y_b,) = tiled_op(_rg_tile, "rg", rgp, [x_b, g_b], [(rg_tt, rc), (1, rc)], [(W, BF16)], RG_CB, rg_tt)
        dk = W // HG_HEADS
        hgp = [_vec(lbs[l], HG_HEADS), _vec(wts["hg_norm_w"][l], HG_HEADS)]
        (y_c,) = tiled_op(_hg_tile, "hg", hgp, [q_c, f_c, i_c, g_c], [(dk, dk)], [(W, BF16)], HG_HEADS, hg_tt)
        br = [linear(y, wts["w_branch"][l * 3 + n], "w_br") for n, y in enumerate((y_a, y_b, y_c))]
        (mg,) = tiled_op(_merge_tile, "merge", [], br + gl, [], [(D, BF16)], 1, min(MERGE_TT, T))
        x = linear(mg, wts["w_out"][l].reshape(1, D, D), "w_out", res=x)
    (rl,) = tiled_op(_loss_tile, "loss", [wts["final_norm_w"].reshape(1, 1, D)], [x, target], [], [(1, F32)], 1, row_tt)
    return jnp.sum(rl)


_ANY = pl.BlockSpec(memory_space=pl.ANY)


def _place():
    x, y, c = lax.axis_index("x"), lax.axis_index("y"), lax.axis_index("c")
    chips = [(1 - x, y), (x, 1 - y), (1 - x, 1 - y)]
    return x, y, c, chips


def _rcopy(src, dst, send_sems, recv_sems, k, to):
    return pltpu.make_async_remote_copy(src_ref=src, dst_ref=dst, send_sem=send_sems.at[k], recv_sem=recv_sems.at[k],
                                        device_id=to, device_id_type=MESH)


def all_gather_big(w, name):
    Lp, M, N = w.shape
    Lh = Lp // 2

    def body(w_ref, o_ref, send_sems, recv_sems, local_sem):
        x, y, c, chips = _place()
        sibling = (x, y, 1 - c)
        me = 2 * x + y
        mine = pltpu.make_async_copy(w_ref, o_ref.at[:, me], local_sem)
        mine.start()

        def blk(h, s):
            return o_ref.at[pl.ds(h * Lh, Lh), s]

        first = [_rcopy(w_ref.at[pl.ds(c * Lh, Lh)], blk(c, me), send_sems, recv_sems, j, (*chip, c))
                 for j, chip in enumerate(chips)]
        for cp in first:
            cp.start()
        passed = []
        for j, chip in enumerate(chips):
            s = 2 * chip[0] + chip[1]
            _rcopy(blk(c, s), blk(c, s), send_sems, recv_sems, j, (*chip, c)).wait_recv()
            cp = _rcopy(blk(c, s), blk(c, s), send_sems, recv_sems, 3 + j, sibling)
            cp.start()
            passed.append(cp)
        for j, chip in enumerate(chips):
            s = 2 * chip[0] + chip[1]
            _rcopy(blk(1 - c, s), blk(1 - c, s), send_sems, recv_sems, 3 + j, sibling).wait_recv()
        for cp in first + passed:
            cp.wait_send()
        mine.wait()

    return pl.pallas_call(
        body, name=name, in_specs=[_ANY], out_specs=_ANY,
        out_shape=jax.ShapeDtypeStruct((Lp, N_CHIPS, M, N), w.dtype),
        scratch_shapes=[pltpu.SemaphoreType.DMA((6,)), pltpu.SemaphoreType.DMA((6,)), pltpu.SemaphoreType.DMA],
    )(w)


def _swap_half(g, name):
    Lp = g.shape[0]
    Lh = Lp // 2

    def body(g_ref, o_ref, send_sems, recv_sems):
        x, y, c, _ = _place()
        cp = _rcopy(g_ref.at[pl.ds((1 - c) * Lh, Lh)], o_ref, send_sems, recv_sems, 0, (x, y, 1 - c))
        cp.start()
        cp.wait()

    return pl.pallas_call(
        body, name=name, in_specs=[_ANY], out_specs=_ANY,
        out_shape=jax.ShapeDtypeStruct((Lh,) + g.shape[1:], g.dtype),
        scratch_shapes=[pltpu.SemaphoreType.DMA((1,)), pltpu.SemaphoreType.DMA((1,))],
    )(g)


def _scatter_chips(p, name):
    Lh, _, M, N = p.shape

    def body(p_ref, o_ref, send_sems, recv_sems):
        x, y, c, chips = _place()
        cps = [_rcopy(p_ref.at[:, 2 * chip[0] + chip[1]], o_ref.at[j], send_sems, recv_sems, j, (*chip, c))
               for j, chip in enumerate(chips)]
        for cp in cps:
            cp.start()
        for cp in cps:
            cp.wait()

    return pl.pallas_call(
        body, name=name, in_specs=[_ANY], out_specs=_ANY,
        out_shape=jax.ShapeDtypeStruct((3, Lh, M, N), p.dtype),
        scratch_shapes=[pltpu.SemaphoreType.DMA((3,)), pltpu.SemaphoreType.DMA((3,))],
    )(p)


def _join_halves(gh, name):
    Lh, M, N = gh.shape

    def body(g_ref, o_ref, send_sems, recv_sems, local_sem):
        x, y, c, _ = _place()
        mine = pltpu.make_async_copy(g_ref, o_ref.at[pl.ds(c * Lh, Lh)], local_sem)
        mine.start()
        cp = _rcopy(g_ref, o_ref.at[pl.ds(c * Lh, Lh)], send_sems, recv_sems, 0, (x, y, 1 - c))
        cp.start()
        cp.wait_send()
        _rcopy(g_ref, o_ref.at[pl.ds((1 - c) * Lh, Lh)], send_sems, recv_sems, 0, (x, y, 1 - c)).wait_recv()
        mine.wait()

    return pl.pallas_call(
        body, name=name, in_specs=[_ANY], out_specs=_ANY,
        out_shape=jax.ShapeDtypeStruct((2 * Lh, M, N), gh.dtype),
        scratch_shapes=[pltpu.SemaphoreType.DMA((1,)), pltpu.SemaphoreType.DMA((1,)), pltpu.SemaphoreType.DMA],
    )(gh)


def _ew_call(fn, ins, out_dtypes, name):
    shape = ins[0].shape
    n = shape[-1]
    ins2 = [a.reshape(-1, n) for a in ins]
    rows = ins2[0].shape[0]
    tr = _pick(rows, (256, 128, 64, 32, 16, 8)) if n <= 2048 else _pick(rows, (128, 64, 32, 16, 8))
    n_in = len(ins2)

    def body(*refs):
        outs = fn(*[r[...] for r in refs[:n_in]])
        for o, v in zip(refs[n_in:], outs):
            o[...] = v.astype(o.dtype)

    spec = pl.BlockSpec((tr, n), lambda i: (i, 0))
    res = pl.pallas_call(
        body, name=name, grid=(rows // tr,), in_specs=[spec] * n_in, out_specs=[spec] * len(out_dtypes),
        out_shape=[jax.ShapeDtypeStruct((rows, n), dt) for dt in out_dtypes],
        compiler_params=_cparams(("parallel",)),
    )(*ins2)
    return [r.reshape(shape) for r in res]


def reduce_big(g, name):
    Lp = g.shape[0]
    Lh = Lp // 2
    x, y, c = lax.axis_index("x"), lax.axis_index("y"), lax.axis_index("c")
    got = _swap_half(g, name + "_swap")
    mine = lax.dynamic_slice_in_dim(g, c * Lh, Lh, 0)
    (pair,) = _ew_call(lambda a, b: [a.astype(F32) + b.astype(F32)], [mine, got], [BF16], name + "_pair")
    recv = _scatter_chips(pair, name + "_scatter")
    own = lax.dynamic_index_in_dim(pair, 2 * x + y, 1, keepdims=False)
    (gh,) = _ew_call(lambda a, r0, r1, r2: [((a.astype(F32) + r0.astype(F32)) + r1.astype(F32)) + r2.astype(F32)],
                     [own, recv[0], recv[1], recv[2]], [F32], name + "_sum")
    return _join_halves(gh, name + "_join")


def all_reduce_small(buf, name):
    _, R, Ln = buf.shape

    def body(in_ref, out_ref, recv_ref, send_a, recv_a, send_b, recv_b):
        x, y, c = lax.axis_index("x"), lax.axis_index("y"), lax.axis_index("c")
        me = 4 * x + 2 * y + c
        peers = []
        for r in range(1, 8):
            px, py, pc = x ^ ((r >> 2) & 1), y ^ ((r >> 1) & 1), c ^ (r & 1)
            peers.append((r, (px, py, pc), 4 * px + 2 * py + pc))
        cps = [_rcopy(in_ref.at[idx], recv_ref.at[r], send_a, recv_a, r, to) for r, to, idx in peers]
        for cp in cps:
            cp.start()
        for cp in cps:
            cp.wait()
        acc = in_ref[me]
        for r in range(1, 8):
            acc = acc + recv_ref[r]
        out_ref[me] = acc
        cps = [_rcopy(out_ref.at[me], out_ref.at[me], send_b, recv_b, r, to) for r, to, idx in peers]
        for cp in cps:
            cp.start()
        for (r, to, idx), cp in zip(peers, cps):
            cp.wait_send()
            _rcopy(out_ref.at[idx], out_ref.at[idx], send_b, recv_b, r, to).wait_recv()

    vm = pl.BlockSpec(memory_space=pltpu.VMEM)
    return pl.pallas_call(
        body, name=name, in_specs=[vm], out_specs=vm,
        out_shape=jax.ShapeDtypeStruct(buf.shape, F32),
        scratch_shapes=[pltpu.VMEM(buf.shape, F32)] + [pltpu.SemaphoreType.DMA((8,))] * 4,
        compiler_params=pltpu.CompilerParams(vmem_limit_bytes=VMEM_LIMIT_BYTES),
    )(buf)


def _adamw_math(w, g, m, v):
    m = ADAM_B1 * m + (1.0 - ADAM_B1) * g
    v = ADAM_B2 * v + (1.0 - ADAM_B2) * (g * g)
    m_hat = m / (1.0 - ADAM_B1 ** ADAM_STEP)
    v_hat = v / (1.0 - ADAM_B2 ** ADAM_STEP)
    delta = -ADAM_LR * (m_hat / (jnp.sqrt(v_hat) + ADAM_EPS) + ADAM_WD * w)
    return [delta, m, v]


def adamw(w, g, m, v, name):
    return _ew_call(_adamw_math, [w, g, m, v], [F32, F32, F32], name)


_WEIGHTS = ['norm_w', 'w_in', 's5_lambda_re', 's5_lambda_im', 's5_log_step', 's5_b_re', 's5_b_im', 's5_c_re', 's5_c_im',
            's5_d', 's5_w_glu', 's5_b_glu', 'rg_conv_w', 'rg_conv_b', 'rg_w_a', 'rg_b_a', 'rg_w_x', 'rg_b_x', 'rg_lambda',
            'hg_lower_bounds', 'hg_norm_w', 'w_branch', 'w_out', 'final_norm_w']
_BIG = ('w_in', 's5_w_glu', 'w_branch', 'w_out')
_SMALL = [n for n in _WEIGHTS if n not in _BIG]
_LANES = 128
_N_DEV = 8


def _pack(arrs):
    flat = jnp.concatenate([a.reshape(-1) for a in arrs])
    unit = _N_DEV * 8 * _LANES
    total = -(-flat.shape[0] // unit) * unit
    flat = jnp.pad(flat, (0, total - flat.shape[0]))
    return flat.reshape(_N_DEV, total // (_N_DEV * _LANES), _LANES)


def _unpack(buf, shapes):
    flat = buf.reshape(-1)
    out, off = [], 0
    for s in shapes:
        n = math.prod(s)
        out.append(flat[off:off + n].reshape(s))
        off += n
    return out


def _step(a):
    x_idx, y_idx, c_idx = lax.axis_index("x"), lax.axis_index("y"), lax.axis_index("c")
    chip = 2 * x_idx + y_idx
    L = a["norm_w"].shape[0]
    W = a["s5_d"].shape[1]
    cw = a["rg_conv_w"]
    wc = cw.shape[2]
    placed = lax.dynamic_update_slice(jnp.zeros((L, RG_CONV, W), F32), cw, (0, 0, chip * wc))
    placed = placed * (c_idx == 0).astype(F32)
    conv_full = _unpack(all_reduce_small(_pack([placed]), "gather_conv"), [(L, RG_CONV, W)])[0]
    shard3 = {"w_in": a["w_in"], "s5_w_glu": a["s5_w_glu"], "w_out": a["w_out"],
              "w_branch": a["w_branch"].reshape((L * 3,) + a["w_branch"].shape[2:])}
    wts = {n: a[n] for n in _SMALL}
    wts["rg_conv_w"] = conv_full
    for n in _BIG:
        full = all_gather_big(shard3[n].astype(BF16), "gather_" + n)
        wts[n] = [full[p] for p in range(full.shape[0])]
    loss, (gw, gx) = jax.value_and_grad(_forward_loss, argnums=(0, 1))(wts, a["x"][0], a["loss_target"][0])
    loss = lax.psum(loss, ("x", "y", "c"))
    grads = {}
    for n in _BIG:
        red = reduce_big(jnp.stack(gw[n], 0), "reduce_" + n)
        grads[n] = red.reshape(a[n].shape)
    small_shapes = [gw[n].shape for n in _SMALL]
    red = _unpack(all_reduce_small(_pack([gw[n].astype(F32) for n in _SMALL]), "reduce_small"), small_shapes)
    for n, g in zip(_SMALL, red):
        grads[n] = g
    grads["rg_conv_w"] = lax.dynamic_slice_in_dim(grads["rg_conv_w"], chip * wc, wc, 2)
    delta, new_m, new_v = {}, {}, {}
    for n in _BIG:
        delta[n], new_m[n], new_v[n] = adamw(a[n], grads[n], a["m_" + n], a["v_" + n], "adamw_" + n)
    shapes = [a[n].shape for n in _SMALL]
    packed = [_pack([t[n] for n in _SMALL]) for t in
              (a, grads, {n: a["m_" + n] for n in _SMALL}, {n: a["v_" + n] for n in _SMALL})]
    for dst, buf in zip((delta, new_m, new_v), adamw(*packed, "adamw_small")):
        for n, t in zip(_SMALL, _unpack(buf, shapes)):
            dst[n] = t
    return (loss, gx[None], *[grads[n] for n in _WEIGHTS], *[delta[n] for n in _WEIGHTS],
            *[new_m[n] for n in _WEIGHTS], *[new_v[n] for n in _WEIGHTS])


_ARG_NAMES = ["x"] + _WEIGHTS + ["loss_target"] + ["m_" + n for n in _WEIGHTS] + ["v_" + n for n in _WEIGHTS]


def kernel(x, norm_w, w_in, s5_lambda_re, s5_lambda_im, s5_log_step, s5_b_re, s5_b_im, s5_c_re, s5_c_im, s5_d, s5_w_glu, s5_b_glu, rg_conv_w, rg_conv_b, rg_w_a, rg_b_a, rg_w_x, rg_b_x, rg_lambda, hg_lower_bounds, hg_norm_w, w_branch, w_out, final_norm_w, loss_target, m_norm_w, m_w_in, m_s5_lambda_re, m_s5_lambda_im, m_s5_log_step, m_s5_b_re, m_s5_b_im, m_s5_c_re, m_s5_c_im, m_s5_d, m_s5_w_glu, m_s5_b_glu, m_rg_conv_w, m_rg_conv_b, m_rg_w_a, m_rg_b_a, m_rg_w_x, m_rg_b_x, m_rg_lambda, m_hg_lower_bounds, m_hg_norm_w, m_w_branch, m_w_out, m_final_norm_w, v_norm_w, v_w_in, v_s5_lambda_re, v_s5_lambda_im, v_s5_log_step, v_s5_b_re, v_s5_b_im, v_s5_c_re, v_s5_c_im, v_s5_d, v_s5_w_glu, v_s5_b_glu, v_rg_conv_w, v_rg_conv_b, v_rg_w_a, v_rg_b_a, v_rg_w_x, v_rg_b_x, v_rg_lambda, v_hg_lower_bounds, v_hg_norm_w, v_w_branch, v_w_out, v_final_norm_w):
    vals = (x, norm_w, w_in, s5_lambda_re, s5_lambda_im, s5_log_step, s5_b_re, s5_b_im, s5_c_re, s5_c_im, s5_d, s5_w_glu, s5_b_glu, rg_conv_w, rg_conv_b, rg_w_a, rg_b_a, rg_w_x, rg_b_x, rg_lambda, hg_lower_bounds, hg_norm_w, w_branch, w_out, final_norm_w, loss_target, m_norm_w, m_w_in, m_s5_lambda_re, m_s5_lambda_im, m_s5_log_step, m_s5_b_re, m_s5_b_im, m_s5_c_re, m_s5_c_im, m_s5_d, m_s5_w_glu, m_s5_b_glu, m_rg_conv_w, m_rg_conv_b, m_rg_w_a, m_rg_b_a, m_rg_w_x, m_rg_b_x, m_rg_lambda, m_hg_lower_bounds, m_hg_norm_w, m_w_branch, m_w_out, m_final_norm_w, v_norm_w, v_w_in, v_s5_lambda_re, v_s5_lambda_im, v_s5_log_step, v_s5_b_re, v_s5_b_im, v_s5_c_re, v_s5_c_im, v_s5_d, v_s5_w_glu, v_s5_b_glu, v_rg_conv_w, v_rg_conv_b, v_rg_w_a, v_rg_b_a, v_rg_w_x, v_rg_b_x, v_rg_lambda, v_hg_lower_bounds, v_hg_norm_w, v_w_branch, v_w_out, v_final_norm_w)
    return _step(dict(zip(_ARG_NAMES, vals)))
```

```python
import functools
import math

import jax
import jax.numpy as jnp
from jax import lax
from jax.experimental import pallas as pl
from jax.experimental.pallas import tpu as pltpu

F32 = jnp.float32
BF16 = jnp.bfloat16
EPS = 1e-6
RG_C = 8.0
S5_GROUP = 16
S5_STATE = 64
RG_BLOCKS = 16
RG_CONV = 4
HG_HEADS = 8
N_CHIPS = 4
VMEM_LIMIT_BYTES = 56 * 1024 * 1024
NEG_BIG = -1e30

ADAM_LR = 0.001
ADAM_B1 = 0.9
ADAM_B2 = 0.999
ADAM_EPS = 1e-08
ADAM_WD = 0.01
ADAM_STEP = 10

MESH = pl.DeviceIdType.MESH


def _cparams(sem):
    return pltpu.CompilerParams(dimension_semantics=sem, vmem_limit_bytes=VMEM_LIMIT_BYTES)


_DOT_DIMS = {"nn": (((1,), (0,)), ((), ())), "nt": (((1,), (1,)), ((), ())), "tn": (((0,), (0,)), ((), ()))}


def _bdot_raw(a, b, form):
    return lax.dot_general(a.astype(BF16), b.astype(BF16), _DOT_DIMS[form], preferred_element_type=F32)


@functools.partial(jax.custom_vjp, nondiff_argnums=(2,))
def bdot(a, b, form):
    return _bdot_raw(a, b, form)


def _bdot_fwd(a, b, form):
    return _bdot_raw(a, b, form), (a, b)


def _bdot_bwd(form, res, g):
    a, b = res
    if form == "nn":
        da, db = _bdot_raw(g, b, "nt"), _bdot_raw(a, g, "tn")
    elif form == "nt":
        da, db = _bdot_raw(g, b, "nn"), _bdot_raw(g, a, "tn")
    else:
        da, db = _bdot_raw(b, g, "nt"), _bdot_raw(a, g, "nn")
    return da.astype(a.dtype), db.astype(b.dtype)


bdot.defvjp(_bdot_fwd, _bdot_bwd)


@functools.partial(jax.custom_vjp, nondiff_argnums=(1,))
def sroll(x, d):
    return pltpu.roll(x, d, 0)


def _sroll_fwd(x, d):
    return pltpu.roll(x, d, 0), None


def _sroll_bwd(d, _, g):
    return (pltpu.roll(g, g.shape[0] - d, 0),)


sroll.defvjp(_sroll_fwd, _sroll_bwd)


def _row_iota(n):
    return lax.broadcasted_iota(jnp.int32, (n, 1), 0)


def _last_row(x, ti):
    return jnp.sum(jnp.where(ti == x.shape[0] - 1, x, 0.0), axis=0, keepdims=True)


def _silu(x):
    return x * jax.nn.sigmoid(x)


def _tiled_specs(params, rows, carry_shapes, out_defs, CB, tT, nT, rev):
    tmap = (lambda t: nT - 1 - t) if rev else (lambda t: t)
    p_specs = [pl.BlockSpec((None,) + p.shape[1:], lambda cb, t: (cb, 0, 0)) for p in params]
    r_specs = [pl.BlockSpec((tT, r.shape[1] // CB), lambda cb, t: (tmap(t), cb)) for r in rows]
    o_specs = [pl.BlockSpec((tT, w // CB), lambda cb, t: (tmap(t), cb)) for (w, _) in out_defs]
    s_specs = [pl.BlockSpec((None, r, c), lambda cb, t: (tmap(t), 0, cb)) for (r, c) in carry_shapes]
    return p_specs, r_specs, o_specs, s_specs


def _tiled_fwd(f, name, params, rows, carry_shapes, out_defs, CB, tT):
    T = rows[0].shape[0]
    nT = T // tT
    n_p, n_r, n_o, n_c = len(params), len(rows), len(out_defs), len(carry_shapes)
    p_specs, r_specs, o_specs, s_specs = _tiled_specs(params, rows, carry_shapes, out_defs, CB, tT, nT, False)

    def body(*refs):
        p_refs = refs[:n_p]
        r_refs = refs[n_p:n_p + n_r]
        o_refs = refs[n_p + n_r:n_p + n_r + n_o]
        s_refs = refs[n_p + n_r + n_o:n_p + n_r + n_o + n_c]
        c_refs = refs[n_p + n_r + n_o + n_c:]
        t = pl.program_id(1)

        @pl.when(t == 0)
        def _():
            for c in c_refs:
                c[...] = jnp.zeros_like(c)

        carries = [c[...] for c in c_refs]
        for s, cv in zip(s_refs, carries):
            s[...] = cv
        outs, newc = f([p[...] for p in p_refs], [r[...] for r in r_refs], carries, t * tT)
        for o, v in zip(o_refs, outs):
            o[...] = v.astype(o.dtype)
        for c, v in zip(c_refs, newc):
            c[...] = v

    out_shape = [jax.ShapeDtypeStruct((T, w), dt) for (w, dt) in out_defs]
    out_shape += [jax.ShapeDtypeStruct((nT, r, c * CB), F32) for (r, c) in carry_shapes]
    res = pl.pallas_call(
        body, name=name + "_fwd", grid=(CB, nT),
        in_specs=p_specs + r_specs, out_specs=o_specs + s_specs, out_shape=out_shape,
        scratch_shapes=[pltpu.VMEM((r, c), F32) for (r, c) in carry_shapes],
        compiler_params=_cparams(("arbitrary", "arbitrary")),
    )(*params, *rows)
    return list(res[:n_o]), list(res[n_o:])


def _tiled_bwd(f, name, params, rows, saved, douts, carry_shapes, out_defs, CB, tT):
    T = rows[0].shape[0]
    nT = T // tT
    n_p, n_r, n_o, n_c = len(params), len(rows), len(out_defs), len(carry_shapes)
    p_specs, r_specs, o_specs, s_specs = _tiled_specs(params, rows, carry_shapes, out_defs, CB, tT, nT, True)
    out_dtypes = [dt for (_, dt) in out_defs]

    def body(*refs):
        i = 0
        p_refs = refs[i:i + n_p]; i += n_p
        r_refs = refs[i:i + n_r]; i += n_r
        s_refs = refs[i:i + n_c]; i += n_c
        g_refs = refs[i:i + n_o]; i += n_o
        dp_refs = refs[i:i + n_p]; i += n_p
        dr_refs = refs[i:i + n_r]; i += n_r
        dc_refs = refs[i:]
        t = pl.program_id(1)

        @pl.when(t == 0)
        def _():
            for c in dc_refs:
                c[...] = jnp.zeros_like(c)
            for d in dp_refs:
                d[...] = jnp.zeros_like(d)

        t0 = (nT - 1 - t) * tT

        def g(P, R, C):
            outs, newc = f(P, R, C, t0)
            return [o.astype(dt) for o, dt in zip(outs, out_dtypes)], list(newc)

        _, vjp = jax.vjp(g, [p[...] for p in p_refs], [r[...] for r in r_refs], [s[...] for s in s_refs])
        dP, dR, dC = vjp(([gr[...] for gr in g_refs], [c[...] for c in dc_refs]))
        for d, v in zip(dp_refs, dP):
            d[...] += v
        for d, v in zip(dr_refs, dR):
            d[...] = v.astype(d.dtype)
        for c, v in zip(dc_refs, dC):
            c[...] = v

    out_shape = [jax.ShapeDtypeStruct(p.shape, F32) for p in params]
    out_shape += [jax.ShapeDtypeStruct(r.shape, r.dtype) for r in rows]
    res = pl.pallas_call(
        body, name=name + "_bwd", grid=(CB, nT),
        in_specs=p_specs + r_specs + s_specs + o_specs, out_specs=p_specs + r_specs, out_shape=out_shape,
        scratch_shapes=[pltpu.VMEM((r, c), F32) for (r, c) in carry_shapes],
        compiler_params=_cparams(("arbitrary", "arbitrary")),
    )(*params, *rows, *saved, *douts)
    return list(res[:n_p]), list(res[n_p:])


def tiled_op(f, name, params, rows, carry_shapes, out_defs, CB, tT):
    @jax.custom_vjp
    def op(params, rows):
        return _tiled_fwd(f, name, params, rows, carry_shapes, out_defs, CB, tT)[0]

    def op_fwd(params, rows):
        outs, saved = _tiled_fwd(f, name, params, rows, carry_shapes, out_defs, CB, tT)
        return outs, (params, rows, saved)

    def op_bwd(res, douts):
        params, rows, saved = res
        dP, dR = _tiled_bwd(f, name, params, rows, saved, list(douts), carry_shapes, out_defs, CB, tT)
        return dP, dR

    op.defvjp(op_fwd, op_bwd)
    return op(list(params), list(rows))


def _pick(n, pref):
    for t in pref:
        if n % t == 0:
            return t
    return n


def _mm_nn(a, w, res, name):
    M, K = a.shape
    S, _, Ns = w.shape
    tm = _pick(M, (1024, 512, 256, 128))
    tn = _pick(Ns, (512, 256, 128))
    nps = Ns // tn
    has_res = res is not None

    def body(*refs):
        if has_res:
            a_ref, w_ref, r_ref, o_ref = refs
        else:
            a_ref, w_ref, o_ref = refs
        acc = _bdot_raw(a_ref[...], w_ref[...], "nn")
        if has_res:
            acc = acc + r_ref[...]
        o_ref[...] = acc

    in_specs = [pl.BlockSpec((tm, K), lambda i, j: (i, 0)),
                pl.BlockSpec((None, K, tn), lambda i, j: (j // nps, 0, j % nps))]
    args = [a, w]
    if has_res:
        in_specs.append(pl.BlockSpec((tm, tn), lambda i, j: (i, j)))
        args.append(res)
    return pl.pallas_call(
        body, name=name, grid=(M // tm, S * nps), in_specs=in_specs,
        out_specs=pl.BlockSpec((tm, tn), lambda i, j: (i, j)),
        out_shape=jax.ShapeDtypeStruct((M, S * Ns), F32),
        compiler_params=_cparams(("parallel", "arbitrary")),
    )(*args)


def _mm_nt(g, w, out_dtype, name):
    M, N = g.shape
    S, K, Ns = w.shape
    tm = _pick(M, (1024, 512, 256, 128))
    tk = _pick(K, (1024, 512, 256, 128))
    tn = _pick(Ns, (512, 256, 128))
    nps = Ns // tn
    nn = S * nps

    def body(g_ref, w_ref, o_ref, acc_ref):
        n = pl.program_id(2)

        @pl.when(n == 0)
        def _():
            acc_ref[...] = jnp.zeros_like(acc_ref)

        acc_ref[...] += _bdot_raw(g_ref[...], w_ref[...], "nt")

        @pl.when(n == nn - 1)
        def _():
            o_ref[...] = acc_ref[...].astype(o_ref.dtype)

    return pl.pallas_call(
        body, name=name, grid=(M // tm, K // tk, nn),
        in_specs=[pl.BlockSpec((tm, tn), lambda i, k, n: (i, n)),
                  pl.BlockSpec((None, tk, tn), lambda i, k, n: (n // nps, k, n % nps))],
        out_specs=pl.BlockSpec((tm, tk), lambda i, k, n: (i, k)),
        out_shape=jax.ShapeDtypeStruct((M, K), out_dtype),
        scratch_shapes=[pltpu.VMEM((tm, tk), F32)],
        compiler_params=_cparams(("parallel", "parallel", "arbitrary")),
    )(g, w)


def _mm_tn(a, g, S, out_dtype, name):
    T, K = a.shape
    N = g.shape[1]
    Ns = N // S
    tk = _pick(K, (2048, 1024, 512, 256, 128))
    tn = _pick(Ns, (1024, 896, 512, 256, 128))
    tt = _pick(T, (512, 256, 128))
    nps = Ns // tn
    nt = T // tt
    a_t = a.astype(BF16).T

    def body(a_ref, g_ref, o_ref, acc_ref):
        t = pl.program_id(2)

        @pl.when(t == 0)
        def _():
            acc_ref[...] = jnp.zeros_like(acc_ref)

        acc_ref[...] += _bdot_raw(a_ref[...], g_ref[...], "nn")

        @pl.when(t == nt - 1)
        def _():
            o_ref[...] = acc_ref[...].astype(o_ref.dtype)

    return pl.pallas_call(
        body, name=name, grid=(K // tk, S * nps, nt),
        in_specs=[pl.BlockSpec((tk, tt), lambda k, j, t: (k, t)),
                  pl.BlockSpec((tt, tn), lambda k, j, t: (t, j))],
        out_specs=pl.BlockSpec((None, tk, tn), lambda k, j, t: (j // nps, k, j % nps)),
        out_shape=jax.ShapeDtypeStruct((S, K, Ns), out_dtype),
        scratch_shapes=[pltpu.VMEM((tk, tn), F32)],
        compiler_params=_cparams(("parallel", "parallel", "arbitrary")),
    )(a_t, g)


def linear(a, w, name, res=None):
    @jax.custom_vjp
    def op(a, w, res):
        return _mm_nn(a, w, res, name + "_fwd")

    def op_fwd(a, w, res):
        return _mm_nn(a, w, res, name + "_fwd"), (a, w)

    def op_bwd(saved, g):
        a, w = saved
        da = _mm_nt(g, w, a.dtype, name + "_bwd_a")
        dw = _mm_tn(a, g, w.shape[0], w.dtype, name + "_bwd_w")
        return da, dw, (None if res is None else g)

    op.defvjp(op_fwd, op_bwd)
    return op(a, w, res)


def _rms_tile(params, rows, carries, t0):
    (w,), (x,) = params, rows
    y = x * lax.rsqrt(jnp.mean(x * x, axis=-1, keepdims=True) + EPS) * w
    return [y], []


def _s5_tile(params, rows, carries, t0):
    t_m, s_re, s_im, r_re, r_im, apow_re, apow_im, d = params
    (u,) = rows
    n = u.shape[0]
    ti = _row_iota(n)
    pi = _row_iota(apow_re.shape[0])
    x_re, x_im = bdot(u, s_re, "nn"), bdot(u, s_im, "nn")
    k = 0
    while (1 << k) < n:
        sh = 1 << k
        p_re, p_im = _row_sel(apow_re, pi, k), _row_sel(apow_im, pi, k)
        q_re, q_im = sroll(x_re, sh), sroll(x_im, sh)
        m = ti >= sh
        x_re, x_im = (x_re + jnp.where(m, p_re * q_re - p_im * q_im, 0.0),
                      x_im + jnp.where(m, p_re * q_im + p_im * q_re, 0.0))
        k += 1
    x_re = jnp.where(ti >= 1, sroll(x_re, 1), 0.0)
    x_im = jnp.where(ti >= 1, sroll(x_im, 1), 0.0)
    y = bdot(u, t_m, "nn") + bdot(x_re, r_re, "nn") + bdot(x_im, r_im, "nn") + d * u
    return [jax.nn.gelu(y)], []


def _glu_tile(params, rows, carries, t0):
    (b,), (y, zg, ga) = params, rows
    return [y * jax.nn.sigmoid(zg + b) * _silu(ga)], []


def _neg_expm1(z):
    small = -(z * (1.0 + z * (0.5 + z * (1.0 / 6.0))))
    return jnp.where(z > -0.01, small, 1.0 - jnp.exp(z))


def _rg_tile(params, rows, carries, t0):
    conv_w, conv_b, w_a, b_a, w_x, b_x, sp = params
    x, gate = rows
    x_prev, h_prev = carries
    tT = x.shape[0]
    ti = _row_iota(tT)
    ci = _row_iota(RG_CONV)
    xc = _row_sel(conv_w, ci, RG_CONV - 1) * x + conv_b
    for k in range(1, RG_CONV):
        xs = jnp.where(ti >= k, sroll(x, k), sroll(x_prev, k))
        xc = xc + _row_sel(conv_w, ci, RG_CONV - 1 - k) * xs
    r = jax.nn.sigmoid(bdot(xc, w_a, "nn") + b_a)
    i = jax.nn.sigmoid(bdot(xc, w_x, "nn") + b_x)
    log_a = -RG_C * r * sp
    a = jnp.exp(log_a)
    mult = jnp.sqrt(_neg_expm1(2.0 * log_a))
    mult = jnp.where(ti + t0 == 0, 1.0, mult)
    b = mult * (i * xc)
    b = b + jnp.where(ti == 0, a * h_prev, 0.0)
    k = 1
    while k < tT:
        m = ti >= k
        b = b + jnp.where(m, a * sroll(b, k), 0.0)
        a = jnp.where(m, a * sroll(a, k), a)
        k *= 2
    return [b * _silu(gate)], [x, _last_row(b, ti)]


def _hg_tile(params, rows, carries, t0):
    lb, nw = params
    q, fl, v, gate = rows
    (st,) = carries
    tT = q.shape[0]
    ti = _row_iota(tT)
    qs = _silu(q)
    f = lb + (1.0 - lb) * jax.nn.sigmoid(fl)
    kk = 1.0 - f
    G = jnp.log(f)
    k = 1
    while k < tT:
        G = G + jnp.where(ti >= k, sroll(G, k), 0.0)
        k *= 2
    inter = bdot(qs * jnp.exp(G), st, "nt")
    tr = lax.broadcasted_iota(jnp.int32, (tT, tT), 0)
    sc = lax.broadcasted_iota(jnp.int32, (tT, tT), 1)
    attn = jnp.zeros((tT, tT), F32)
    blk = tT
    while blk > 8:
        sub = blk // 4
        for j in range(1, 4):
            ref = jnp.zeros_like(G)
            for b in range(tT // blk):
                row = _row_sel(G, ti, b * blk + sub * j - 1)
                ref = ref + jnp.where(_div2(ti, blk) == b, row, 0.0)
            tmask = _div2(_mod2(ti, blk), sub) == j
            smask = _mod2(ti, blk) < sub * j
            qt = qs * jnp.exp(jnp.where(tmask, G - ref, NEG_BIG))
            kt = kk * jnp.exp(jnp.where(smask, ref - G, NEG_BIG))
            aj = bdot(qt, kt, "nt")
            attn = attn + jnp.where(_div2(tr, blk) == _div2(sc, blk), aj, 0.0)
        blk = sub
    intra = bdot(attn, v, "nn")
    for d in range(blk):
        if d == 0:
            kd, gd, vd = kk, G, v
        else:
            kd, gd, vd = sroll(kk, d), sroll(G, d), sroll(v, d)
        m = _mod2(ti, blk) >= d
        w = jnp.sum(qs * kd * jnp.exp(jnp.where(m, G - gd, NEG_BIG)), axis=1, keepdims=True)
        intra = intra + w * vd
    o = inter + intra
    g_last = _last_row(G, ti)
    k_dec = kk * jnp.exp(g_last - G)
    st_new = st * jnp.exp(g_last) + bdot(v, k_dec, "tn")
    o = o * lax.rsqrt(jnp.mean(o * o, axis=-1, keepdims=True) + EPS) * nw
    return [o * _silu(gate)], [st_new]


def _row_sel(x, ti, r):
    return jnp.sum(jnp.where(ti == r, x, 0.0), axis=0, keepdims=True)


def _div2(i, p):
    return lax.shift_right_logical(i, jnp.int32(p.bit_length() - 1))


def _mod2(i, p):
    return lax.bitwise_and(i, jnp.int32(p - 1))


def _merge_tile(params, rows, carries, t0):
    b0, b1, b2, g0, g1, g2 = rows
    m = jax.nn.sigmoid(g0) * b0 + jax.nn.sigmoid(g1) * b1 + jax.nn.sigmoid(g2) * b2
    return [m], []


def _loss_tile(params, rows, carries, t0):
    (w,), (x, tgt) = params, rows
    y = x * lax.rsqrt(jnp.mean(x * x, axis=-1, keepdims=True) + EPS) * w
    e = y - tgt
    return [0.5 * jnp.mean(e * e, axis=-1, keepdims=True)], []


def _block_diag(w, cb):
    n, i, j = w.shape
    g = n // cb
    w4 = w.reshape(cb, g, i, j)
    eye = jnp.eye(g, dtype=w.dtype)
    return jnp.einsum("cgij,gk->cgikj", w4, eye).reshape(cb, g * i, g * j)


def _s5_params(lam_re, lam_im, log_step, b_re, b_im, c_re, c_im, d, levels):
    G, P = lam_re.shape
    step = jnp.exp(log_step)[:, None]
    mag = jnp.exp(lam_re * step)
    ang = lam_im * step
    abar_re = mag * jnp.cos(ang)
    abar_im = mag * jnp.sin(ang)
    num_re = abar_re - 1.0
    num_im = abar_im
    den = lam_re * lam_re + lam_im * lam_im
    coef_re = (num_re * lam_re + num_im * lam_im) / den
    coef_im = (num_im * lam_re - num_re * lam_im) / den
    bbar_re = coef_re[..., None] * b_re - coef_im[..., None] * b_im
    bbar_im = coef_re[..., None] * b_im + coef_im[..., None] * b_re
    H = b_re.shape[2]
    Lc = S5_LC
    hi = lax.Precision.HIGHEST

    def powers(ks):
        ks = jnp.asarray(ks, F32)[:, None, None]
        m = jnp.exp(ks * (lam_re * step))
        return m * jnp.cos(ks * ang), m * jnp.sin(ks * ang)

    pw_re, pw_im = powers(list(range(Lc + 1)))
    ab_re = pw_re[..., None] * bbar_re - pw_im[..., None] * bbar_im
    ab_im = pw_re[..., None] * bbar_im + pw_im[..., None] * bbar_re
    kern = (jnp.einsum("gap,kgph->gkha", c_re, ab_re[:Lc], precision=hi)
            - jnp.einsum("gap,kgph->gkha", c_im, ab_im[:Lc], precision=hi))
    kk = jnp.arange(Lc)[:, None, None]
    jj = jnp.arange(Lc)[None, :, None]
    ii = jnp.arange(Lc)[None, None, :]
    place = (ii - jj == kk).astype(F32)
    t_m = jnp.einsum("gkha,kji->gjhia", kern, place, precision=hi).reshape(G, Lc * H, Lc * H)
    s_re = ab_re[:Lc][::-1].transpose(1, 0, 3, 2).reshape(G, Lc * H, P)
    s_im = ab_im[:Lc][::-1].transpose(1, 0, 3, 2).reshape(G, Lc * H, P)
    m_re = c_re[None] * pw_re[1:, :, None, :] - c_im[None] * pw_im[1:, :, None, :]
    m_im = c_re[None] * pw_im[1:, :, None, :] + c_im[None] * pw_re[1:, :, None, :]
    r_re = m_re.transpose(1, 3, 0, 2).reshape(G, P, Lc * H)
    r_im = -m_im.transpose(1, 3, 0, 2).reshape(G, P, Lc * H)
    rows = max(8, levels)
    ap_re, ap_im = powers([Lc * (1 << k) for k in range(levels)] + [0] * (rows - levels))
    dd = jnp.tile(d.reshape(G, 1, H), (1, 1, Lc))
    return [t_m, s_re, s_im, r_re, r_im, ap_re.transpose(1, 0, 2), ap_im.transpose(1, 0, 2), dd]


def _to_chunks(u, lc, h):
    t, w = u.shape
    return u.reshape(t // lc, lc, w // h, h).transpose(0, 2, 1, 3).reshape(t // lc, w * lc)


def _from_chunks(y, lc, h):
    n, wl = y.shape
    w = wl // lc
    return y.reshape(n, w // h, lc, h).transpose(0, 2, 1, 3).reshape(n * lc, w)


def _vec(v, cb):
    return v.reshape(cb, 1, -1)


S5_LC = 16
RG_CB, RG_TT = 4, 256
HG_TT = 128
ROW_TT = 256
MERGE_TT = 128


def _forward_loss(wts, x, target):
    T, D = x.shape
    L = wts["norm_w"].shape[0]
    W = wts["s5_d"].shape[1]
    lb_sm = jax.nn.softmax(wts["hg_lower_bounds"], axis=0)
    lbs = jnp.cumsum(lb_sm, axis=0) - lb_sm[0]
    row_tt = min(ROW_TT, T)
    rg_tt, hg_tt = min(RG_TT, T), min(HG_TT, T)
    for l in range(L):
        (h,) = tiled_op(_rms_tile, "rms", [wts["norm_w"][l].reshape(1, 1, D)], [x], [], [(D, BF16)], 1, row_tt)
        z = linear(h, wts["w_in"][l], "w_in")
        zs = [z[:, k * W:(k + 1) * W] for k in range(8)]
        gl = [z[:, 8 * W + n * D:8 * W + (n + 1) * D] for n in range(3)]
        u_a, g_a, x_b, g_b, q_c, f_c, i_c, g_c = zs
        n_chunks = T // S5_LC
        s5p = _s5_params(wts["s5_lambda_re"][l], wts["s5_lambda_im"][l], wts["s5_log_step"][l],
                         wts["s5_b_re"][l], wts["s5_b_im"][l], wts["s5_c_re"][l], wts["s5_c_im"][l],
                         wts["s5_d"][l], int(math.log2(n_chunks)))
        (y1c,) = tiled_op(_s5_tile, "s5", s5p, [_to_chunks(u_a, S5_LC, S5_GROUP)], [], [(W * S5_LC, F32)],
                          W // S5_GROUP, n_chunks)
        y1 = _from_chunks(y1c, S5_LC, S5_GROUP)
        zg = linear(y1, wts["s5_w_glu"][l].reshape(1, W, W), "w_glu")
        (y_a,) = tiled_op(_glu_tile, "glu", [wts["s5_b_glu"][l].reshape(1, 1, W)], [y1, zg, g_a], [],
                          [(W, BF16)], 1, row_tt)
        rgp = [wts["rg_conv_w"][l].reshape(RG_CONV, RG_CB, W // RG_CB).transpose(1, 0, 2),
               _vec(wts["rg_conv_b"][l], RG_CB), _block_diag(wts["rg_w_a"][l], RG_CB), _vec(wts["rg_b_a"][l], RG_CB),
               _block_diag(wts["rg_w_x"][l], RG_CB), _vec(wts["rg_b_x"][l], RG_CB), _vec(jax.nn.softplus(-wts["rg_lambda"][l]), RG_CB)]
        rc = W // RG_CB
        (y_b,) = tiled_op(_rg_tile, "rg", rgp, [x_b, g_b], [(rg_tt, rc), (1, rc)], [(W, BF16)], RG_CB, rg_tt)
        dk = W // HG_HEADS
        hgp = [_vec(lbs[l], HG_HEADS), _vec(wts["hg_norm_w"][l], HG_HEADS)]
        (y_c,) = tiled_op(_hg_tile, "hg", hgp, [q_c, f_c, i_c, g_c], [(dk, dk)], [(W, BF16)], HG_HEADS, hg_tt)
        br = [linear(y, wts["w_branch"][l * 3 + n], "w_br") for n, y in enumerate((y_a, y_b, y_c))]
        (mg,) = tiled_op(_merge_tile, "merge", [], br + gl, [], [(D, BF16)], 1, min(MERGE_TT, T))
        x = linear(mg, wts["w_out"][l].reshape(1, D, D), "w_out", res=x)
    (rl,) = tiled_op(_loss_tile, "loss", [wts["final_norm_w"].reshape(1, 1, D)], [x, target], [], [(1, F32)], 1, row_tt)
    return jnp.sum(rl)


_ANY = pl.BlockSpec(memory_space=pl.ANY)


def _place():
    x, y, c = lax.axis_index("x"), lax.axis_index("y"), lax.axis_index("c")
    chips = [(1 - x, y), (x, 1 - y), (1 - x, 1 - y)]
    return x, y, c, chips


def _rcopy(src, dst, send_sems, recv_sems, k, to):
    return pltpu.make_async_remote_copy(src_ref=src, dst_ref=dst, send_sem=send_sems.at[k], recv_sem=recv_sems.at[k],
                                        device_id=to, device_id_type=MESH)


def all_gather_big(w, name):
    Lp, M, N = w.shape
    Lh = Lp // 2

    def body(w_ref, o_ref, send_sems, recv_sems):
        x, y, c, chips = _place()
        sibling = (x, y, 1 - c)
        me = 2 * x + y

        def blk(h, s):
            return o_ref.at[pl.ds(h * Lh, Lh), s]

        first = [_rcopy(w_ref.at[pl.ds(c * Lh, Lh)], blk(c, me), send_sems, recv_sems, j, (*chip, c))
                 for j, chip in enumerate(chips)]
        for cp in first:
            cp.start()
        passed = []
        for j, chip in enumerate(chips):
            s = 2 * chip[0] + chip[1]
            _rcopy(blk(c, s), blk(c, s), send_sems, recv_sems, j, (*chip, c)).wait_recv()
            cp = _rcopy(blk(c, s), blk(c, s), send_sems, recv_sems, 3 + j, sibling)
            cp.start()
            passed.append(cp)
        for j, chip in enumerate(chips):
            s = 2 * chip[0] + chip[1]
            _rcopy(blk(1 - c, s), blk(1 - c, s), send_sems, recv_sems, 3 + j, sibling).wait_recv()
        for cp in first + passed:
            cp.wait_send()

    full = pl.pallas_call(
        body, name=name, in_specs=[_ANY], out_specs=_ANY,
        out_shape=jax.ShapeDtypeStruct((Lp, N_CHIPS, M, N), w.dtype),
        scratch_shapes=[pltpu.SemaphoreType.DMA((6,)), pltpu.SemaphoreType.DMA((6,))],
    )(w)
    chip = 2 * lax.axis_index("x") + lax.axis_index("y")
    return lax.dynamic_update_slice(full, w[:, None], (0, chip, 0, 0))


def _swap_half(g, name):
    Lp = g.shape[0]
    Lh = Lp // 2

    def body(g_ref, o_ref, send_sems, recv_sems):
        x, y, c, _ = _place()
        cp = _rcopy(g_ref.at[pl.ds((1 - c) * Lh, Lh)], o_ref, send_sems, recv_sems, 0, (x, y, 1 - c))
        cp.start()
        cp.wait()

    return pl.pallas_call(
        body, name=name, in_specs=[_ANY], out_specs=_ANY,
        out_shape=jax.ShapeDtypeStruct((Lh,) + g.shape[1:], g.dtype),
        scratch_shapes=[pltpu.SemaphoreType.DMA((1,)), pltpu.SemaphoreType.DMA((1,))],
    )(g)


def _scatter_chips(p, name):
    Lh, _, M, N = p.shape

    def body(p_ref, o_ref, send_sems, recv_sems):
        x, y, c, chips = _place()
        cps = [_rcopy(p_ref.at[:, 2 * chip[0] + chip[1]], o_ref.at[j], send_sems, recv_sems, j, (*chip, c))
               for j, chip in enumerate(chips)]
        for cp in cps:
            cp.start()
        for cp in cps:
            cp.wait()

    return pl.pallas_call(
        body, name=name, in_specs=[_ANY], out_specs=_ANY,
        out_shape=jax.ShapeDtypeStruct((3, Lh, M, N), p.dtype),
        scratch_shapes=[pltpu.SemaphoreType.DMA((3,)), pltpu.SemaphoreType.DMA((3,))],
    )(p)


def _join_halves(gh, name):
    Lh, M, N = gh.shape

    def body(g_ref, o_ref, send_sems, recv_sems):
        x, y, c, _ = _place()
        cp = _rcopy(g_ref, o_ref, send_sems, recv_sems, 0, (x, y, 1 - c))
        cp.start()
        cp.wait()

    other = pl.pallas_call(
        body, name=name, in_specs=[_ANY], out_specs=_ANY,
        out_shape=jax.ShapeDtypeStruct((Lh, M, N), gh.dtype),
        scratch_shapes=[pltpu.SemaphoreType.DMA((1,)), pltpu.SemaphoreType.DMA((1,))],
    )(gh)
    south = lax.axis_index("c") == 0
    return jnp.concatenate([jnp.where(south, gh, other), jnp.where(south, other, gh)], 0)


def _ew_call(fn, ins, out_dtypes, name):
    shape = ins[0].shape
    n = shape[-1]
    ins2 = [a.reshape(-1, n) for a in ins]
    rows = ins2[0].shape[0]
    tr = _pick(rows, (256, 128, 64, 32, 16, 8)) if n <= 2048 else _pick(rows, (128, 64, 32, 16, 8))
    n_in = len(ins2)

    def body(*refs):
        outs = fn(*[r[...] for r in refs[:n_in]])
        for o, v in zip(refs[n_in:], outs):
            o[...] = v.astype(o.dtype)

    spec = pl.BlockSpec((tr, n), lambda i: (i, 0))
    res = pl.pallas_call(
        body, name=name, grid=(rows // tr,), in_specs=[spec] * n_in, out_specs=[spec] * len(out_dtypes),
        out_shape=[jax.ShapeDtypeStruct((rows, n), dt) for dt in out_dtypes],
        compiler_params=_cparams(("parallel",)),
    )(*ins2)
    return [r.reshape(shape) for r in res]


def reduce_big(g, name):
    Lp = g.shape[0]
    Lh = Lp // 2
    x, y, c = lax.axis_index("x"), lax.axis_index("y"), lax.axis_index("c")
    got = _swap_half(g, name + "_swap")
    mine = lax.dynamic_slice_in_dim(g, c * Lh, Lh, 0)
    (pair,) = _ew_call(lambda a, b: [a.astype(F32) + b.astype(F32)], [mine, got], [BF16], name + "_pair")
    recv = _scatter_chips(pair, name + "_scatter")
    own = lax.dynamic_index_in_dim(pair, 2 * x + y, 1, keepdims=False)
    (gh,) = _ew_call(lambda a, r0, r1, r2: [((a.astype(F32) + r0.astype(F32)) + r1.astype(F32)) + r2.astype(F32)],
                     [own, recv[0], recv[1], recv[2]], [F32], name + "_sum")
    return _join_halves(gh, name + "_join")


def all_reduce_small(buf, name):
    _, R, Ln = buf.shape

    def body(in_ref, out_ref, recv_ref, send_a, recv_a, send_b, recv_b):
        x, y, c = lax.axis_index("x"), lax.axis_index("y"), lax.axis_index("c")
        me = 4 * x + 2 * y + c
        peers = []
        for r in range(1, 8):
            px, py, pc = x ^ ((r >> 2) & 1), y ^ ((r >> 1) & 1), c ^ (r & 1)
            peers.append((r, (px, py, pc), 4 * px + 2 * py + pc))
        cps = [_rcopy(in_ref.at[idx], recv_ref.at[r], send_a, recv_a, r, to) for r, to, idx in peers]
        for cp in cps:
            cp.start()
        for cp in cps:
            cp.wait()
        acc = in_ref[me]
        for r in range(1, 8):
            acc = acc + recv_ref[r]
        out_ref[me] = acc
        cps = [_rcopy(out_ref.at[me], out_ref.at[me], send_b, recv_b, r, to) for r, to, idx in peers]
        for cp in cps:
            cp.start()
        for (r, to, idx), cp in zip(peers, cps):
            cp.wait_send()
            _rcopy(out_ref.at[idx], out_ref.at[idx], send_b, recv_b, r, to).wait_recv()

    vm = pl.BlockSpec(memory_space=pltpu.VMEM)
    return pl.pallas_call(
        body, name=name, in_specs=[vm], out_specs=vm,
        out_shape=jax.ShapeDtypeStruct(buf.shape, F32),
        scratch_shapes=[pltpu.VMEM(buf.shape, F32)] + [pltpu.SemaphoreType.DMA((8,))] * 4,
        compiler_params=pltpu.CompilerParams(vmem_limit_bytes=VMEM_LIMIT_BYTES),
    )(buf)


def _adamw_math(w, g, m, v):
    m = ADAM_B1 * m + (1.0 - ADAM_B1) * g
    v = ADAM_B2 * v + (1.0 - ADAM_B2) * (g * g)
    m_hat = m / (1.0 - ADAM_B1 ** ADAM_STEP)
    v_hat = v / (1.0 - ADAM_B2 ** ADAM_STEP)
    delta = -ADAM_LR * (m_hat / (jnp.sqrt(v_hat) + ADAM_EPS) + ADAM_WD * w)
    return [delta, m, v]


def adamw(w, g, m, v, name):
    return _ew_call(_adamw_math, [w, g, m, v], [F32, F32, F32], name)


_WEIGHTS = ['norm_w', 'w_in', 's5_lambda_re', 's5_lambda_im', 's5_log_step', 's5_b_re', 's5_b_im', 's5_c_re', 's5_c_im',
            's5_d', 's5_w_glu', 's5_b_glu', 'rg_conv_w', 'rg_conv_b', 'rg_w_a', 'rg_b_a', 'rg_w_x', 'rg_b_x', 'rg_lambda',
            'hg_lower_bounds', 'hg_norm_w', 'w_branch', 'w_out', 'final_norm_w']
_BIG = ('w_in', 's5_w_glu', 'w_branch', 'w_out')
_SMALL = [n for n in _WEIGHTS if n not in _BIG]
_LANES = 128
_N_DEV = 8


def _pack(arrs):
    flat = jnp.concatenate([a.reshape(-1) for a in arrs])
    unit = _N_DEV * 8 * _LANES
    total = -(-flat.shape[0] // unit) * unit
    flat = jnp.pad(flat, (0, total - flat.shape[0]))
    return flat.reshape(_N_DEV, total // (_N_DEV * _LANES), _LANES)


def _unpack(buf, shapes):
    flat = buf.reshape(-1)
    out, off = [], 0
    for s in shapes:
        n = math.prod(s)
        out.append(flat[off:off + n].reshape(s))
        off += n
    return out


def _step(a):
    x_idx, y_idx, c_idx = lax.axis_index("x"), lax.axis_index("y"), lax.axis_index("c")
    chip = 2 * x_idx + y_idx
    L = a["norm_w"].shape[0]
    W = a["s5_d"].shape[1]
    cw = a["rg_conv_w"]
    wc = cw.shape[2]
    placed = lax.dynamic_update_slice(jnp.zeros((L, RG_CONV, W), F32), cw, (0, 0, chip * wc))
    placed = placed * (c_idx == 0).astype(F32)
    conv_full = _unpack(all_reduce_small(_pack([placed]), "gather_conv"), [(L, RG_CONV, W)])[0]
    shard3 = {"w_in": a["w_in"], "s5_w_glu": a["s5_w_glu"], "w_out": a["w_out"],
              "w_branch": a["w_branch"].reshape((L * 3,) + a["w_branch"].shape[2:])}
    wts = {n: a[n] for n in _SMALL}
    wts["rg_conv_w"] = conv_full
    for n in _BIG:
        full = all_gather_big(shard3[n].astype(BF16), "gather_" + n)
        wts[n] = [full[p] for p in range(full.shape[0])]
    loss, (gw, gx) = jax.value_and_grad(_forward_loss, argnums=(0, 1))(wts, a["x"][0], a["loss_target"][0])
    loss = lax.psum(loss, ("x", "y", "c"))
    grads = {}
    for n in _BIG:
        red = reduce_big(jnp.stack(gw[n], 0), "reduce_" + n)
        grads[n] = red.reshape(a[n].shape)
    small_shapes = [gw[n].shape for n in _SMALL]
    red = _unpack(all_reduce_small(_pack([gw[n].astype(F32) for n in _SMALL]), "reduce_small"), small_shapes)
    for n, g in zip(_SMALL, red):
        grads[n] = g
    grads["rg_conv_w"] = lax.dynamic_slice_in_dim(grads["rg_conv_w"], chip * wc, wc, 2)
    delta, new_m, new_v = {}, {}, {}
    for n in _BIG:
        delta[n], new_m[n], new_v[n] = adamw(a[n], grads[n], a["m_" + n], a["v_" + n], "adamw_" + n)
    shapes = [a[n].shape for n in _SMALL]
    packed = [_pack([t[n] for n in _SMALL]) for t in
              (a, grads, {n: a["m_" + n] for n in _SMALL}, {n: a["v_" + n] for n in _SMALL})]
    for dst, buf in zip((delta, new_m, new_v), adamw(*packed, "adamw_small")):
        for n, t in zip(_SMALL, _unpack(buf, shapes)):
            dst[n] = t
    return (loss, gx[None], *[grads[n] for n in _WEIGHTS], *[delta[n] for n in _WEIGHTS],
            *[new_m[n] for n in _WEIGHTS], *[new_v[n] for n in _WEIGHTS])


_ARG_NAMES = ["x"] + _WEIGHTS + ["loss_target"] + ["m_" + n for n in _WEIGHTS] + ["v_" + n for n in _WEIGHTS]


def kernel(x, norm_w, w_in, s5_lambda_re, s5_lambda_im, s5_log_step, s5_b_re, s5_b_im, s5_c_re, s5_c_im, s5_d, s5_w_glu, s5_b_glu, rg_conv_w, rg_conv_b, rg_w_a, rg_b_a, rg_w_x, rg_b_x, rg_lambda, hg_lower_bounds, hg_norm_w, w_branch, w_out, final_norm_w, loss_target, m_norm_w, m_w_in, m_s5_lambda_re, m_s5_lambda_im, m_s5_log_step, m_s5_b_re, m_s5_b_im, m_s5_c_re, m_s5_c_im, m_s5_d, m_s5_w_glu, m_s5_b_glu, m_rg_conv_w, m_rg_conv_b, m_rg_w_a, m_rg_b_a, m_rg_w_x, m_rg_b_x, m_rg_lambda, m_hg_lower_bounds, m_hg_norm_w, m_w_branch, m_w_out, m_final_norm_w, v_norm_w, v_w_in, v_s5_lambda_re, v_s5_lambda_im, v_s5_log_step, v_s5_b_re, v_s5_b_im, v_s5_c_re, v_s5_c_im, v_s5_d, v_s5_w_glu, v_s5_b_glu, v_rg_conv_w, v_rg_conv_b, v_rg_w_a, v_rg_b_a, v_rg_w_x, v_rg_b_x, v_rg_lambda, v_hg_lower_bounds, v_hg_norm_w, v_w_branch, v_w_out, v_final_norm_w):
    vals = (x, norm_w, w_in, s5_lambda_re, s5_lambda_im, s5_log_step, s5_b_re, s5_b_im, s5_c_re, s5_c_im, s5_d, s5_w_glu, s5_b_glu, rg_conv_w, rg_conv_b, rg_w_a, rg_b_a, rg_w_x, rg_b_x, rg_lambda, hg_lower_bounds, hg_norm_w, w_branch, w_out, final_norm_w, loss_target, m_norm_w, m_w_in, m_s5_lambda_re, m_s5_lambda_im, m_s5_log_step, m_s5_b_re, m_s5_b_im, m_s5_c_re, m_s5_c_im, m_s5_d, m_s5_w_glu, m_s5_b_glu, m_rg_conv_w, m_rg_conv_b, m_rg_w_a, m_rg_b_a, m_rg_w_x, m_rg_b_x, m_rg_lambda, m_hg_lower_bounds, m_hg_norm_w, m_w_branch, m_w_out, m_final_norm_w, v_norm_w, v_w_in, v_s5_lambda_re, v_s5_lambda_im, v_s5_log_step, v_s5_b_re, v_s5_b_im, v_s5_c_re, v_s5_c_im, v_s5_d, v_s5_w_glu, v_s5_b_glu, v_rg_conv_w, v_rg_conv_b, v_rg_w_a, v_rg_b_a, v_rg_w_x, v_rg_b_x, v_rg_lambda, v_hg_lower_bounds, v_hg_norm_w, v_w_branch, v_w_out, v_final_norm_w)
    return _step(dict(zip(_ARG_NAMES, vals)))
```

```python
import functools
import math

import jax
import jax.numpy as jnp
from jax import lax
from jax.experimental import pallas as pl
from jax.experimental.pallas import tpu as pltpu

F32 = jnp.float32
BF16 = jnp.bfloat16
EPS = 1e-6
RG_C = 8.0
S5_GROUP = 16
S5_STATE = 64
RG_BLOCKS = 16
RG_CONV = 4
HG_HEADS = 8
N_CHIPS = 4
VMEM_LIMIT_BYTES = 56 * 1024 * 1024
NEG_BIG = -1e30

ADAM_LR = 0.001
ADAM_B1 = 0.9
ADAM_B2 = 0.999
ADAM_EPS = 1e-08
ADAM_WD = 0.01
ADAM_STEP = 10

MESH = pl.DeviceIdType.MESH


def _cparams(sem):
    return pltpu.CompilerParams(dimension_semantics=sem, vmem_limit_bytes=VMEM_LIMIT_BYTES)


_DOT_DIMS = {"nn": (((1,), (0,)), ((), ())), "nt": (((1,), (1,)), ((), ())), "tn": (((0,), (0,)), ((), ()))}


def _bdot_raw(a, b, form):
    return lax.dot_general(a.astype(BF16), b.astype(BF16), _DOT_DIMS[form], preferred_element_type=F32)


@functools.partial(jax.custom_vjp, nondiff_argnums=(2,))
def bdot(a, b, form):
    return _bdot_raw(a, b, form)


def _bdot_fwd(a, b, form):
    return _bdot_raw(a, b, form), (a, b)


def _bdot_bwd(form, res, g):
    a, b = res
    if form == "nn":
        da, db = _bdot_raw(g, b, "nt"), _bdot_raw(a, g, "tn")
    elif form == "nt":
        da, db = _bdot_raw(g, b, "nn"), _bdot_raw(g, a, "tn")
    else:
        da, db = _bdot_raw(b, g, "nt"), _bdot_raw(a, g, "nn")
    return da.astype(a.dtype), db.astype(b.dtype)


bdot.defvjp(_bdot_fwd, _bdot_bwd)


@functools.partial(jax.custom_vjp, nondiff_argnums=(1,))
def sroll(x, d):
    return pltpu.roll(x, d, 0)


def _sroll_fwd(x, d):
    return pltpu.roll(x, d, 0), None


def _sroll_bwd(d, _, g):
    return (pltpu.roll(g, g.shape[0] - d, 0),)


sroll.defvjp(_sroll_fwd, _sroll_bwd)


def _row_iota(n):
    return lax.broadcasted_iota(jnp.int32, (n, 1), 0)


def _last_row(x, ti):
    return jnp.sum(jnp.where(ti == x.shape[0] - 1, x, 0.0), axis=0, keepdims=True)


def _silu(x):
    return x * jax.nn.sigmoid(x)


class _Tiling:
    def __init__(self, CB, tT, nT, ncb=None, bmap=None):
        self.CB, self.tT, self.nT = CB, tT, nT
        self.ncb = CB if ncb is None else ncb
        self.bmap = (lambda cb, t: (t, cb)) if bmap is None else bmap


def _tiled_specs(params, views, carry_shapes, out_defs, tl, rev):
    nT = tl.nT
    tmap = (lambda t: nT - 1 - t) if rev else (lambda t: t)

    def rspec(col0, width):
        bw = width // tl.ncb
        off = col0 // bw

        def imap(cb, t):
            rb, cbk = tl.bmap(cb, tmap(t))
            return (rb, off + cbk)

        return pl.BlockSpec((tl.tT, bw), imap)

    p_specs = [pl.BlockSpec((None,) + p.shape[1:], lambda cb, t: (cb, 0, 0)) for p in params]
    r_specs = [rspec(c0, w) for (c0, w) in views]
    dr_specs = [rspec(0, w) for (_, w) in views]
    o_specs = [rspec(0, w) for (w, _) in out_defs]
    s_specs = [pl.BlockSpec((None, r, c), lambda cb, t: (tmap(t), 0, cb)) for (r, c) in carry_shapes]
    return p_specs, r_specs, dr_specs, o_specs, s_specs


def _tiled_fwd(f, name, params, rows, views, carry_shapes, out_defs, tl):
    T = rows[0].shape[0]
    CB, tT, nT = tl.CB, tl.tT, tl.nT
    n_p, n_r, n_o, n_c = len(params), len(rows), len(out_defs), len(carry_shapes)
    p_specs, r_specs, _, o_specs, s_specs = _tiled_specs(params, views, carry_shapes, out_defs, tl, False)

    def body(*refs):
        p_refs = refs[:n_p]
        r_refs = refs[n_p:n_p + n_r]
        o_refs = refs[n_p + n_r:n_p + n_r + n_o]
        s_refs = refs[n_p + n_r + n_o:n_p + n_r + n_o + n_c]
        c_refs = refs[n_p + n_r + n_o + n_c:]
        t = pl.program_id(1)

        @pl.when(t == 0)
        def _():
            for c in c_refs:
                c[...] = jnp.zeros_like(c)

        carries = [c[...] for c in c_refs]
        for s, cv in zip(s_refs, carries):
            s[...] = cv
        outs, newc = f([p[...] for p in p_refs], [r[...] for r in r_refs], carries, t * tT)
        for o, v in zip(o_refs, outs):
            o[...] = v.astype(o.dtype)
        for c, v in zip(c_refs, newc):
            c[...] = v

    out_shape = [jax.ShapeDtypeStruct((T, w), dt) for (w, dt) in out_defs]
    out_shape += [jax.ShapeDtypeStruct((nT, r, c * CB), F32) for (r, c) in carry_shapes]
    res = pl.pallas_call(
        body, name=name + "_fwd", grid=(CB, nT),
        in_specs=p_specs + r_specs, out_specs=o_specs + s_specs, out_shape=out_shape,
        scratch_shapes=[pltpu.VMEM((r, c), F32) for (r, c) in carry_shapes],
        compiler_params=_cparams(("arbitrary", "arbitrary")),
    )(*params, *rows)
    return list(res[:n_o]), list(res[n_o:])


def _tiled_bwd(f, name, params, rows, views, saved, douts, carry_shapes, out_defs, tl):
    T = rows[0].shape[0]
    CB, tT, nT = tl.CB, tl.tT, tl.nT
    n_p, n_r, n_o, n_c = len(params), len(rows), len(out_defs), len(carry_shapes)
    p_specs, r_specs, dr_specs, o_specs, s_specs = _tiled_specs(params, views, carry_shapes, out_defs, tl, True)
    out_dtypes = [dt for (_, dt) in out_defs]

    def body(*refs):
        i = 0
        p_refs = refs[i:i + n_p]; i += n_p
        r_refs = refs[i:i + n_r]; i += n_r
        s_refs = refs[i:i + n_c]; i += n_c
        g_refs = refs[i:i + n_o]; i += n_o
        dp_refs = refs[i:i + n_p]; i += n_p
        dr_refs = refs[i:i + n_r]; i += n_r
        dc_refs = refs[i:]
        t = pl.program_id(1)

        @pl.when(t == 0)
        def _():
            for c in dc_refs:
                c[...] = jnp.zeros_like(c)
            for d in dp_refs:
                d[...] = jnp.zeros_like(d)

        t0 = (nT - 1 - t) * tT

        def g(P, R, C):
            outs, newc = f(P, R, C, t0)
            return [o.astype(dt) for o, dt in zip(outs, out_dtypes)], list(newc)

        _, vjp = jax.vjp(g, [p[...] for p in p_refs], [r[...] for r in r_refs], [s[...] for s in s_refs])
        dP, dR, dC = vjp(([gr[...] for gr in g_refs], [c[...] for c in dc_refs]))
        for d, v in zip(dp_refs, dP):
            d[...] += v
        for d, v in zip(dr_refs, dR):
            d[...] = v.astype(d.dtype)
        for c, v in zip(dc_refs, dC):
            c[...] = v

    out_shape = [jax.ShapeDtypeStruct(p.shape, F32) for p in params]
    out_shape += [jax.ShapeDtypeStruct((T, w), r.dtype) for r, (_, w) in zip(rows, views)]
    res = pl.pallas_call(
        body, name=name + "_bwd", grid=(CB, nT),
        in_specs=p_specs + r_specs + s_specs + o_specs, out_specs=p_specs + dr_specs, out_shape=out_shape,
        scratch_shapes=[pltpu.VMEM((r, c), F32) for (r, c) in carry_shapes],
        compiler_params=_cparams(("arbitrary", "arbitrary")),
    )(*params, *rows, *saved, *douts)
    return list(res[:n_p]), list(res[n_p:])


def tiled_op(f, name, params, rows, carry_shapes, out_defs, CB, tT, tiling=None):
    arrs = [r[0] if isinstance(r, tuple) else r for r in rows]
    views = [(r[1], r[2]) if isinstance(r, tuple) else (0, r.shape[1]) for r in rows]
    tl = tiling if tiling is not None else _Tiling(CB, tT, arrs[0].shape[0] // tT)

    @jax.custom_vjp
    def op(params, arrs):
        return _tiled_fwd(f, name, params, arrs, views, carry_shapes, out_defs, tl)[0]

    def op_fwd(params, arrs):
        outs, saved = _tiled_fwd(f, name, params, arrs, views, carry_shapes, out_defs, tl)
        return outs, (params, arrs, saved)

    def op_bwd(res, douts):
        params, arrs, saved = res
        dP, dR = _tiled_bwd(f, name, params, arrs, views, saved, list(douts), carry_shapes, out_defs, tl)
        dR = [d if w == a.shape[1] else jnp.pad(d, ((0, 0), (c0, a.shape[1] - c0 - w)))
              for d, a, (c0, w) in zip(dR, arrs, views)]
        return dP, dR

    op.defvjp(op_fwd, op_bwd)
    return op(list(params), arrs)


def _pick(n, pref):
    for t in pref:
        if n % t == 0:
            return t
    return n


def _mm_nn(a, w, res, name, out_dtype=F32):
    M, K = a.shape
    S, _, Ns = w.shape
    tm = _pick(M, (1024, 512, 256, 128))
    tn = _pick(Ns, (512, 256, 128))
    nps = Ns // tn
    has_res = res is not None

    def body(*refs):
        if has_res:
            a_ref, w_ref, r_ref, o_ref = refs
        else:
            a_ref, w_ref, o_ref = refs
        acc = _bdot_raw(a_ref[...], w_ref[...], "nn")
        if has_res:
            acc = acc + r_ref[...]
        o_ref[...] = acc.astype(o_ref.dtype)

    in_specs = [pl.BlockSpec((tm, K), lambda i, j: (i, 0)),
                pl.BlockSpec((None, K, tn), lambda i, j: (j // nps, 0, j % nps))]
    args = [a, w]
    if has_res:
        in_specs.append(pl.BlockSpec((tm, tn), lambda i, j: (i, j)))
        args.append(res)
    return pl.pallas_call(
        body, name=name, grid=(M // tm, S * nps), in_specs=in_specs,
        out_specs=pl.BlockSpec((tm, tn), lambda i, j: (i, j)),
        out_shape=jax.ShapeDtypeStruct((M, S * Ns), out_dtype),
        compiler_params=_cparams(("parallel", "arbitrary")),
    )(*args)


def _mm_nt(g, w, out_dtype, name):
    M, N = g.shape
    S, K, Ns = w.shape
    tm = _pick(M, (1024, 512, 256, 128))
    tk = _pick(K, (1024, 512, 256, 128))
    tn = _pick(Ns, (1792, 1024, 512, 256, 128))
    nps = Ns // tn
    nn = S * nps

    def body(g_ref, w_ref, o_ref, acc_ref):
        n = pl.program_id(2)

        @pl.when(n == 0)
        def _():
            acc_ref[...] = jnp.zeros_like(acc_ref)

        acc_ref[...] += _bdot_raw(g_ref[...], w_ref[...], "nt")

        @pl.when(n == nn - 1)
        def _():
            o_ref[...] = acc_ref[...].astype(o_ref.dtype)

    return pl.pallas_call(
        body, name=name, grid=(M // tm, K // tk, nn),
        in_specs=[pl.BlockSpec((tm, tn), lambda i, k, n: (i, n)),
                  pl.BlockSpec((None, tk, tn), lambda i, k, n: (n // nps, k, n % nps))],
        out_specs=pl.BlockSpec((tm, tk), lambda i, k, n: (i, k)),
        out_shape=jax.ShapeDtypeStruct((M, K), out_dtype),
        scratch_shapes=[pltpu.VMEM((tm, tk), F32)],
        compiler_params=_cparams(("parallel", "parallel", "arbitrary")),
    )(g, w)


def _mm_tn(a, g, S, out_dtype, name):
    T, K = a.shape
    N = g.shape[1]
    Ns = N // S
    tk = _pick(K, (2048, 1024, 512, 256, 128))
    tn = _pick(Ns, (1024, 896, 512, 256, 128))
    tt = _pick(T, (1024, 512, 256, 128))
    nps = Ns // tn
    nt = T // tt
    a_t = a.astype(BF16).T

    def body(a_ref, g_ref, o_ref, acc_ref):
        t = pl.program_id(2)

        @pl.when(t == 0)
        def _():
            acc_ref[...] = jnp.zeros_like(acc_ref)

        acc_ref[...] += _bdot_raw(a_ref[...], g_ref[...], "nn")

        @pl.when(t == nt - 1)
        def _():
            o_ref[...] = acc_ref[...].astype(o_ref.dtype)

    return pl.pallas_call(
        body, name=name, grid=(K // tk, S * nps, nt),
        in_specs=[pl.BlockSpec((tk, tt), lambda k, j, t: (k, t)),
                  pl.BlockSpec((tt, tn), lambda k, j, t: (t, j))],
        out_specs=pl.BlockSpec((None, tk, tn), lambda k, j, t: (j // nps, k, j % nps)),
        out_shape=jax.ShapeDtypeStruct((S, K, Ns), out_dtype),
        scratch_shapes=[pltpu.VMEM((tk, tn), F32)],
        compiler_params=_cparams(("parallel", "parallel", "arbitrary")),
    )(a_t, g)


def linear(a, w, name, res=None):
    @jax.custom_vjp
    def op(a, w, res):
        return _mm_nn(a, w, res, name + "_fwd")

    def op_fwd(a, w, res):
        return _mm_nn(a, w, res, name + "_fwd"), (a, w)

    def op_bwd(saved, g):
        a, w = saved
        da = _mm_nt(g, w, a.dtype, name + "_bwd_a")
        dw = _mm_tn(a, g, w.shape[0], w.dtype, name + "_bwd_w")
        return da, dw, (None if res is None else g)

    op.defvjp(op_fwd, op_bwd)
    return op(a, w, res)


def _rms_tile(params, rows, carries, t0):
    (w,), (x,) = params, rows
    y = x * lax.rsqrt(jnp.mean(x * x, axis=-1, keepdims=True) + EPS) * w
    return [y], []


def _s5_tile(params, rows, carries, t0):
    t_m, s_re, s_im, r_re, r_im, apow_re, apow_im, d = params
    (u,) = rows
    n = u.shape[0]
    ti = _row_iota(n)
    pi = _row_iota(apow_re.shape[0])
    x_re, x_im = bdot(u, s_re, "nn"), bdot(u, s_im, "nn")
    k = 0
    while (1 << k) < n:
        sh = 1 << k
        p_re, p_im = _row_sel(apow_re, pi, k), _row_sel(apow_im, pi, k)
        q_re, q_im = sroll(x_re, sh), sroll(x_im, sh)
        m = ti >= sh
        x_re, x_im = (x_re + jnp.where(m, p_re * q_re - p_im * q_im, 0.0),
                      x_im + jnp.where(m, p_re * q_im + p_im * q_re, 0.0))
        k += 1
    x_re = jnp.where(ti >= 1, sroll(x_re, 1), 0.0)
    x_im = jnp.where(ti >= 1, sroll(x_im, 1), 0.0)
    y = bdot(u, t_m, "nn") + bdot(x_re, r_re, "nn") + bdot(x_im, r_im, "nn") + d * u
    return [jax.nn.gelu(y)], []


def _glu_tile(params, rows, carries, t0):
    (b,), (y, zg, ga) = params, rows
    return [y * jax.nn.sigmoid(zg + b) * _silu(ga)], []


def _neg_expm1(z):
    small = -(z * (1.0 + z * (0.5 + z * (1.0 / 6.0))))
    return jnp.where(z > -0.01, small, 1.0 - jnp.exp(z))


def _rg_tile(params, rows, carries, t0):
    conv_w, conv_b, w_a, b_a, w_x, b_x, sp = params
    x, gate = rows
    x_prev, h_prev = carries
    tT = x.shape[0]
    ti = _row_iota(tT)
    ci = _row_iota(RG_CONV)
    xc = _row_sel(conv_w, ci, RG_CONV - 1) * x + conv_b
    for k in range(1, RG_CONV):
        xs = jnp.where(ti >= k, sroll(x, k), sroll(x_prev, k))
        xc = xc + _row_sel(conv_w, ci, RG_CONV - 1 - k) * xs
    r = jax.nn.sigmoid(bdot(xc, w_a, "nn") + b_a)
    i = jax.nn.sigmoid(bdot(xc, w_x, "nn") + b_x)
    log_a = -RG_C * r * sp
    a = jnp.exp(log_a)
    mult = jnp.sqrt(_neg_expm1(2.0 * log_a))
    mult = jnp.where(ti + t0 == 0, 1.0, mult)
    b = mult * (i * xc)
    b = b + jnp.where(ti == 0, a * h_prev, 0.0)
    k = 1
    while k < tT:
        m = ti >= k
        b = b + jnp.where(m, a * sroll(b, k), 0.0)
        a = jnp.where(m, a * sroll(a, k), a)
        k *= 2
    return [b * _silu(gate)], [x, _last_row(b, ti)]


def _hg_tile(params, rows, carries, t0):
    lb, nw = params
    q, fl, v, gate = rows
    (st,) = carries
    tT = q.shape[0]
    ti = _row_iota(tT)
    qs = _silu(q)
    f = lb + (1.0 - lb) * jax.nn.sigmoid(fl)
    kk = 1.0 - f
    G = jnp.log(f)
    k = 1
    while k < tT:
        G = G + jnp.where(ti >= k, sroll(G, k), 0.0)
        k *= 2
    inter = bdot(qs * jnp.exp(G), st, "nt")
    tr = lax.broadcasted_iota(jnp.int32, (tT, tT), 0)
    sc = lax.broadcasted_iota(jnp.int32, (tT, tT), 1)
    attn = jnp.zeros((tT, tT), F32)
    blk = tT
    while blk > 8:
        sub = blk // 4
        for j in range(1, 4):
            ref = jnp.zeros_like(G)
            for b in range(tT // blk):
                row = _row_sel(G, ti, b * blk + sub * j - 1)
                ref = ref + jnp.where(_div2(ti, blk) == b, row, 0.0)
            tmask = _div2(_mod2(ti, blk), sub) == j
            smask = _mod2(ti, blk) < sub * j
            qt = qs * jnp.exp(jnp.where(tmask, G - ref, NEG_BIG))
            kt = kk * jnp.exp(jnp.where(smask, ref - G, NEG_BIG))
            aj = bdot(qt, kt, "nt")
            attn = attn + jnp.where(_div2(tr, blk) == _div2(sc, blk), aj, 0.0)
        blk = sub
    intra = bdot(attn, v, "nn")
    for d in range(blk):
        if d == 0:
            kd, gd, vd = kk, G, v
        else:
            kd, gd, vd = sroll(kk, d), sroll(G, d), sroll(v, d)
        m = _mod2(ti, blk) >= d
        w = jnp.sum(qs * kd * jnp.exp(jnp.where(m, G - gd, NEG_BIG)), axis=1, keepdims=True)
        intra = intra + w * vd
    o = inter + intra
    g_last = _last_row(G, ti)
    k_dec = kk * jnp.exp(g_last - G)
    st_new = st * jnp.exp(g_last) + bdot(v, k_dec, "tn")
    o = o * lax.rsqrt(jnp.mean(o * o, axis=-1, keepdims=True) + EPS) * nw
    return [o * _silu(gate)], [st_new]


def _row_sel(x, ti, r):
    return jnp.sum(jnp.where(ti == r, x, 0.0), axis=0, keepdims=True)


def _div2(i, p):
    return lax.shift_right_logical(i, jnp.int32(p.bit_length() - 1))


def _mod2(i, p):
    return lax.bitwise_and(i, jnp.int32(p - 1))


def _merge_tile(params, rows, carries, t0):
    b0, b1, b2, g0, g1, g2 = rows
    m = jax.nn.sigmoid(g0) * b0 + jax.nn.sigmoid(g1) * b1 + jax.nn.sigmoid(g2) * b2
    return [m], []


def _loss_tile(params, rows, carries, t0):
    (w,), (x, tgt) = params, rows
    y = x * lax.rsqrt(jnp.mean(x * x, axis=-1, keepdims=True) + EPS) * w
    e = y - tgt
    return [0.5 * jnp.mean(e * e, axis=-1, keepdims=True)], []


def _block_diag(w, cb):
    n, i, j = w.shape
    g = n // cb
    w4 = w.reshape(cb, g, i, j)
    eye = jnp.eye(g, dtype=w.dtype)
    return jnp.einsum("cgij,gk->cgikj", w4, eye).reshape(cb, g * i, g * j)


def _s5_params(lam_re, lam_im, log_step, b_re, b_im, c_re, c_im, d, levels):
    G, P = lam_re.shape
    step = jnp.exp(log_step)[:, None]
    mag = jnp.exp(lam_re * step)
    ang = lam_im * step
    abar_re = mag * jnp.cos(ang)
    abar_im = mag * jnp.sin(ang)
    num_re = abar_re - 1.0
    num_im = abar_im
    den = lam_re * lam_re + lam_im * lam_im
    coef_re = (num_re * lam_re + num_im * lam_im) / den
    coef_im = (num_im * lam_re - num_re * lam_im) / den
    bbar_re = coef_re[..., None] * b_re - coef_im[..., None] * b_im
    bbar_im = coef_re[..., None] * b_im + coef_im[..., None] * b_re
    H = b_re.shape[2]
    Lc = S5_LC
    hi = lax.Precision.HIGHEST

    def powers(ks):
        ks = jnp.asarray(ks, F32)[:, None, None]
        m = jnp.exp(ks * (lam_re * step))
        return m * jnp.cos(ks * ang), m * jnp.sin(ks * ang)

    pw_re, pw_im = powers(list(range(Lc + 1)))
    ab_re = pw_re[..., None] * bbar_re - pw_im[..., None] * bbar_im
    ab_im = pw_re[..., None] * bbar_im + pw_im[..., None] * bbar_re
    kern = (jnp.einsum("gap,kgph->gkha", c_re, ab_re[:Lc], precision=hi)
            - jnp.einsum("gap,kgph->gkha", c_im, ab_im[:Lc], precision=hi))
    kk = jnp.arange(Lc)[:, None, None]
    jj = jnp.arange(Lc)[None, :, None]
    ii = jnp.arange(Lc)[None, None, :]
    place = (ii - jj == kk).astype(F32)
    t_m = jnp.einsum("gkha,kji->gjhia", kern, place, precision=hi).reshape(G, Lc * H, Lc * H)
    s_re = ab_re[:Lc][::-1].transpose(1, 0, 3, 2).reshape(G, Lc * H, P)
    s_im = ab_im[:Lc][::-1].transpose(1, 0, 3, 2).reshape(G, Lc * H, P)
    m_re = c_re[None] * pw_re[1:, :, None, :] - c_im[None] * pw_im[1:, :, None, :]
    m_im = c_re[None] * pw_im[1:, :, None, :] + c_im[None] * pw_re[1:, :, None, :]
    r_re = m_re.transpose(1, 3, 0, 2).reshape(G, P, Lc * H)
    r_im = -m_im.transpose(1, 3, 0, 2).reshape(G, P, Lc * H)
    rows = max(8, levels)
    ap_re, ap_im = powers([Lc * (1 << k) for k in range(levels)] + [0] * (rows - levels))
    dd = jnp.tile(d.reshape(G, 1, H), (1, 1, Lc))
    return [t_m, s_re, s_im, r_re, r_im, ap_re.transpose(1, 0, 2), ap_im.transpose(1, 0, 2), dd]


_LANE = 128


def _lane_perm_matrix(lc, h):
    n = lc * _LANE
    src = jnp.arange(n).reshape(lc, _LANE // h, h).transpose(1, 0, 2).reshape(n)
    return (jnp.arange(n)[:, None] == src[None, :]).astype(BF16)


def _lane_perm(x, p, p_t, name):
    @jax.custom_vjp
    def op(x):
        return _mm_nn(x, p[None], None, name, BF16)

    def op_fwd(x):
        return _mm_nn(x, p[None], None, name, BF16), None

    def op_bwd(_, g):
        return (_mm_nn(g, p_t[None], None, name + "_t", x.dtype),)

    op.defvjp(op_fwd, op_bwd)
    return op(x)


def _to_chunks(u, lc, h, perm):
    t, w = u.shape
    x = u.reshape(t // lc, lc, w // _LANE, _LANE).transpose(2, 0, 1, 3).reshape((w // _LANE) * (t // lc), lc * _LANE)
    return _lane_perm(x, perm, perm.T, "s5_to_chunks")


def _from_chunks(y, lc, w, perm):
    tiles = w // _LANE
    n_chunks = y.shape[0] // tiles
    x = _lane_perm(y, perm.T, perm, "s5_from_chunks")
    return x.reshape(tiles, n_chunks, lc, _LANE).transpose(1, 2, 0, 3).reshape(n_chunks * lc, w)


def _vec(v, cb):
    return v.reshape(cb, 1, -1)


S5_LC = 16
RG_CB, RG_TT = 4, 256
HG_TT = 128
ROW_TT = 256
MERGE_TT = 128


def _forward_loss(wts, x, target):
    T, D = x.shape
    L = wts["norm_w"].shape[0]
    W = wts["s5_d"].shape[1]
    lb_sm = jax.nn.softmax(wts["hg_lower_bounds"], axis=0)
    lbs = jnp.cumsum(lb_sm, axis=0) - lb_sm[0]
    row_tt = min(ROW_TT, T)
    rg_tt, hg_tt = min(RG_TT, T), min(HG_TT, T)
    n_chunks = T // S5_LC
    gpt = _LANE // S5_GROUP
    perm = _lane_perm_matrix(S5_LC, S5_GROUP)
    s5_tiling = _Tiling(W // S5_GROUP, n_chunks, 1, ncb=gpt, bmap=lambda cb, t: (cb // gpt, cb % gpt))
    for l in range(L):
        (h,) = tiled_op(_rms_tile, "rms", [wts["norm_w"][l].reshape(1, 1, D)], [x], [], [(D, BF16)], 1, row_tt)
        z = linear(h, wts["w_in"][l], "w_in")
        g_a, x_b, g_b, q_c, f_c, i_c, g_c = [(z, k * W, W) for k in range(1, 8)]
        gl = [(z, 8 * W + n * D, D) for n in range(3)]
        s5p = _s5_params(wts["s5_lambda_re"][l], wts["s5_lambda_im"][l], wts["s5_log_step"][l],
                         wts["s5_b_re"][l], wts["s5_b_im"][l], wts["s5_c_re"][l], wts["s5_c_im"][l],
                         wts["s5_d"][l], int(math.log2(n_chunks)))
        (y1c,) = tiled_op(_s5_tile, "s5", s5p, [_to_chunks(z[:, :W], S5_LC, S5_GROUP, perm)], [],
                          [(S5_LC * _LANE, BF16)], W // S5_GROUP, n_chunks, tiling=s5_tiling)
        y1 = _from_chunks(y1c, S5_LC, W, perm)
        zg = linear(y1, wts["s5_w_glu"][l].reshape(1, W, W), "w_glu")
        (y_a,) = tiled_op(_glu_tile, "glu", [wts["s5_b_glu"][l].reshape(1, 1, W)], [y1, zg, g_a], [],
                          [(W, BF16)], 1, row_tt)
        rgp = [wts["rg_conv_w"][l].reshape(RG_CONV, RG_CB, W // RG_CB).transpose(1, 0, 2),
               _vec(wts["rg_conv_b"][l], RG_CB), _block_diag(wts["rg_w_a"][l], RG_CB), _vec(wts["rg_b_a"][l], RG_CB),
               _block_diag(wts["rg_w_x"][l], RG_CB), _vec(wts["rg_b_x"][l], RG_CB), _vec(jax.nn.softplus(-wts["rg_lambda"][l]), RG_CB)]
        rc = W // RG_CB
        (y_b,) = tiled_op(_rg_tile, "rg", rgp, [x_b, g_b], [(rg_tt, rc), (1, rc)], [(W, BF16)], RG_CB, rg_tt)
        dk = W // HG_HEADS
        hgp = [_vec(lbs[l], HG_HEADS), _vec(wts["hg_norm_w"][l], HG_HEADS)]
        (y_c,) = tiled_op(_hg_tile, "hg", hgp, [q_c, f_c, i_c, g_c], [(dk, dk)], [(W, BF16)], HG_HEADS, hg_tt)
        br = [linear(y, wts["w_branch"][l * 3 + n], "w_br") for n, y in enumerate((y_a, y_b, y_c))]
        (mg,) = tiled_op(_merge_tile, "merge", [], br + gl, [], [(D, BF16)], 1, min(MERGE_TT, T))
        x = linear(mg, wts["w_out"][l].reshape(1, D, D), "w_out", res=x)
    (rl,) = tiled_op(_loss_tile, "loss", [wts["final_norm_w"].reshape(1, 1, D)], [x, target], [], [(1, F32)], 1, row_tt)
    return jnp.sum(rl)


_ANY = pl.BlockSpec(memory_space=pl.ANY)


def _place():
    x, y, c = lax.axis_index("x"), lax.axis_index("y"), lax.axis_index("c")
    chips = [(1 - x, y), (x, 1 - y), (1 - x, 1 - y)]
    return x, y, c, chips


def _rcopy(src, dst, send_sems, recv_sems, k, to):
    return pltpu.make_async_remote_copy(src_ref=src, dst_ref=dst, send_sem=send_sems.at[k], recv_sem=recv_sems.at[k],
                                        device_id=to, device_id_type=MESH)


def all_gather_big(w, name):
    Lp, M, N = w.shape
    Lh = Lp // 2

    def body(w_ref, o_ref, send_sems, recv_sems):
        x, y, c, chips = _place()
        sibling = (x, y, 1 - c)
        me = 2 * x + y

        def blk(h, s):
            return o_ref.at[pl.ds(h * Lh, Lh), s]

        first = [_rcopy(w_ref.at[pl.ds(c * Lh, Lh)], blk(c, me), send_sems, recv_sems, j, (*chip, c))
                 for j, chip in enumerate(chips)]
        for cp in first:
            cp.start()
        passed = []
        for j, chip in enumerate(chips):
            s = 2 * chip[0] + chip[1]
            _rcopy(blk(c, s), blk(c, s), send_sems, recv_sems, j, (*chip, c)).wait_recv()
            cp = _rcopy(blk(c, s), blk(c, s), send_sems, recv_sems, 3 + j, sibling)
            cp.start()
            passed.append(cp)
        for j, chip in enumerate(chips):
            s = 2 * chip[0] + chip[1]
            _rcopy(blk(1 - c, s), blk(1 - c, s), send_sems, recv_sems, 3 + j, sibling).wait_recv()
        for cp in first + passed:
            cp.wait_send()

    full = pl.pallas_call(
        body, name=name, in_specs=[_ANY], out_specs=_ANY,
        out_shape=jax.ShapeDtypeStruct((Lp, N_CHIPS, M, N), w.dtype),
        scratch_shapes=[pltpu.SemaphoreType.DMA((6,)), pltpu.SemaphoreType.DMA((6,))],
    )(w)
    chip = 2 * lax.axis_index("x") + lax.axis_index("y")
    return lax.dynamic_update_slice(full, w[:, None], (0, chip, 0, 0))


def _swap_half(g, name):
    Lp = g.shape[0]
    Lh = Lp // 2

    def body(g_ref, o_ref, send_sems, recv_sems):
        x, y, c, _ = _place()
        cp = _rcopy(g_ref.at[pl.ds((1 - c) * Lh, Lh)], o_ref, send_sems, recv_sems, 0, (x, y, 1 - c))
        cp.start()
        cp.wait()

    return pl.pallas_call(
        body, name=name, in_specs=[_ANY], out_specs=_ANY,
        out_shape=jax.ShapeDtypeStruct((Lh,) + g.shape[1:], g.dtype),
        scratch_shapes=[pltpu.SemaphoreType.DMA((1,)), pltpu.SemaphoreType.DMA((1,))],
    )(g)


def _scatter_chips(p, name):
    Lh, _, M, N = p.shape

    def body(p_ref, o_ref, send_sems, recv_sems):
        x, y, c, chips = _place()
        cps = [_rcopy(p_ref.at[:, 2 * chip[0] + chip[1]], o_ref.at[j], send_sems, recv_sems, j, (*chip, c))
               for j, chip in enumerate(chips)]
        for cp in cps:
            cp.start()
        for cp in cps:
            cp.wait()

    return pl.pallas_call(
        body, name=name, in_specs=[_ANY], out_specs=_ANY,
        out_shape=jax.ShapeDtypeStruct((3, Lh, M, N), p.dtype),
        scratch_shapes=[pltpu.SemaphoreType.DMA((3,)), pltpu.SemaphoreType.DMA((3,))],
    )(p)


def _join_halves(gh, name):
    Lh, M, N = gh.shape

    def body(g_ref, o_ref, send_sems, recv_sems):
        x, y, c, _ = _place()
        cp = _rcopy(g_ref, o_ref, send_sems, recv_sems, 0, (x, y, 1 - c))
        cp.start()
        cp.wait()

    other = pl.pallas_call(
        body, name=name, in_specs=[_ANY], out_specs=_ANY,
        out_shape=jax.ShapeDtypeStruct((Lh, M, N), gh.dtype),
        scratch_shapes=[pltpu.SemaphoreType.DMA((1,)), pltpu.SemaphoreType.DMA((1,))],
    )(gh)
    south = lax.axis_index("c") == 0
    return jnp.concatenate([jnp.where(south, gh, other), jnp.where(south, other, gh)], 0)


def _ew_call(fn, ins, out_dtypes, name):
    shape = ins[0].shape
    n = shape[-1]
    ins2 = [a.reshape(-1, n) for a in ins]
    rows = ins2[0].shape[0]
    tr = _pick(rows, (256, 128, 64, 32, 16, 8)) if n <= 2048 else _pick(rows, (128, 64, 32, 16, 8))
    n_in = len(ins2)

    def body(*refs):
        outs = fn(*[r[...] for r in refs[:n_in]])
        for o, v in zip(refs[n_in:], outs):
            o[...] = v.astype(o.dtype)

    spec = pl.BlockSpec((tr, n), lambda i: (i, 0))
    res = pl.pallas_call(
        body, name=name, grid=(rows // tr,), in_specs=[spec] * n_in, out_specs=[spec] * len(out_dtypes),
        out_shape=[jax.ShapeDtypeStruct((rows, n), dt) for dt in out_dtypes],
        compiler_params=_cparams(("parallel",)),
    )(*ins2)
    return [r.reshape(shape) for r in res]


def reduce_big(g, name):
    Lp = g.shape[0]
    Lh = Lp // 2
    x, y, c = lax.axis_index("x"), lax.axis_index("y"), lax.axis_index("c")
    got = _swap_half(g, name + "_swap")
    mine = lax.dynamic_slice_in_dim(g, c * Lh, Lh, 0)
    (pair,) = _ew_call(lambda a, b: [a.astype(F32) + b.astype(F32)], [mine, got], [BF16], name + "_pair")
    recv = _scatter_chips(pair, name + "_scatter")
    own = lax.dynamic_index_in_dim(pair, 2 * x + y, 1, keepdims=False)
    (gh,) = _ew_call(lambda a, r0, r1, r2: [((a.astype(F32) + r0.astype(F32)) + r1.astype(F32)) + r2.astype(F32)],
                     [own, recv[0], recv[1], recv[2]], [F32], name + "_sum")
    return _join_halves(gh, name + "_join")


def all_reduce_small(buf, name):
    _, R, Ln = buf.shape

    def body(in_ref, out_ref, recv_ref, send_a, recv_a, send_b, recv_b):
        x, y, c = lax.axis_index("x"), lax.axis_index("y"), lax.axis_index("c")
        me = 4 * x + 2 * y + c
        peers = []
        for r in range(1, 8):
            px, py, pc = x ^ ((r >> 2) & 1), y ^ ((r >> 1) & 1), c ^ (r & 1)
            peers.append((r, (px, py, pc), 4 * px + 2 * py + pc))
        cps = [_rcopy(in_ref.at[idx], recv_ref.at[r], send_a, recv_a, r, to) for r, to, idx in peers]
        for cp in cps:
            cp.start()
        for cp in cps:
            cp.wait()
        acc = in_ref[me]
        for r in range(1, 8):
            acc = acc + recv_ref[r]
        out_ref[me] = acc
        cps = [_rcopy(out_ref.at[me], out_ref.at[me], send_b, recv_b, r, to) for r, to, idx in peers]
        for cp in cps:
            cp.start()
        for (r, to, idx), cp in zip(peers, cps):
            cp.wait_send()
            _rcopy(out_ref.at[idx], out_ref.at[idx], send_b, recv_b, r, to).wait_recv()

    vm = pl.BlockSpec(memory_space=pltpu.VMEM)
    return pl.pallas_call(
        body, name=name, in_specs=[vm], out_specs=vm,
        out_shape=jax.ShapeDtypeStruct(buf.shape, F32),
        scratch_shapes=[pltpu.VMEM(buf.shape, F32)] + [pltpu.SemaphoreType.DMA((8,))] * 4,
        compiler_params=pltpu.CompilerParams(vmem_limit_bytes=VMEM_LIMIT_BYTES),
    )(buf)


def _adamw_math(w, g, m, v):
    m = ADAM_B1 * m + (1.0 - ADAM_B1) * g
    v = ADAM_B2 * v + (1.0 - ADAM_B2) * (g * g)
    m_hat = m / (1.0 - ADAM_B1 ** ADAM_STEP)
    v_hat = v / (1.0 - ADAM_B2 ** ADAM_STEP)
    delta = -ADAM_LR * (m_hat / (jnp.sqrt(v_hat) + ADAM_EPS) + ADAM_WD * w)
    return [delta, m, v]


def adamw(w, g, m, v, name):
    return _ew_call(_adamw_math, [w, g, m, v], [F32, F32, F32], name)


_WEIGHTS = ['norm_w', 'w_in', 's5_lambda_re', 's5_lambda_im', 's5_log_step', 's5_b_re', 's5_b_im', 's5_c_re', 's5_c_im',
            's5_d', 's5_w_glu', 's5_b_glu', 'rg_conv_w', 'rg_conv_b', 'rg_w_a', 'rg_b_a', 'rg_w_x', 'rg_b_x', 'rg_lambda',
            'hg_lower_bounds', 'hg_norm_w', 'w_branch', 'w_out', 'final_norm_w']
_BIG = ('w_in', 's5_w_glu', 'w_branch', 'w_out')
_SMALL = [n for n in _WEIGHTS if n not in _BIG]
_LANES = 128
_N_DEV = 8


def _pack(arrs):
    flat = jnp.concatenate([a.reshape(-1) for a in arrs])
    unit = _N_DEV * 8 * _LANES
    total = -(-flat.shape[0] // unit) * unit
    flat = jnp.pad(flat, (0, total - flat.shape[0]))
    return flat.reshape(_N_DEV, total // (_N_DEV * _LANES), _LANES)


def _unpack(buf, shapes):
    flat = buf.reshape(-1)
    out, off = [], 0
    for s in shapes:
        n = math.prod(s)
        out.append(flat[off:off + n].reshape(s))
        off += n
    return out


def _step(a):
    x_idx, y_idx, c_idx = lax.axis_index("x"), lax.axis_index("y"), lax.axis_index("c")
    chip = 2 * x_idx + y_idx
    L = a["norm_w"].shape[0]
    W = a["s5_d"].shape[1]
    cw = a["rg_conv_w"]
    wc = cw.shape[2]
    placed = lax.dynamic_update_slice(jnp.zeros((L, RG_CONV, W), F32), cw, (0, 0, chip * wc))
    placed = placed * (c_idx == 0).astype(F32)
    conv_full = _unpack(all_reduce_small(_pack([placed]), "gather_conv"), [(L, RG_CONV, W)])[0]
    shard3 = {"w_in": a["w_in"], "s5_w_glu": a["s5_w_glu"], "w_out": a["w_out"],
              "w_branch": a["w_branch"].reshape((L * 3,) + a["w_branch"].shape[2:])}
    wts = {n: a[n] for n in _SMALL}
    wts["rg_conv_w"] = conv_full
    for n in _BIG:
        full = all_gather_big(shard3[n].astype(BF16), "gather_" + n)
        wts[n] = [full[p] for p in range(full.shape[0])]
    loss, (gw, gx) = jax.value_and_grad(_forward_loss, argnums=(0, 1))(wts, a["x"][0], a["loss_target"][0])
    loss = lax.psum(loss, ("x", "y", "c"))
    grads = {}
    for n in _BIG:
        red = reduce_big(jnp.stack(gw[n], 0), "reduce_" + n)
        grads[n] = red.reshape(a[n].shape)
    small_shapes = [gw[n].shape for n in _SMALL]
    red = _unpack(all_reduce_small(_pack([gw[n].astype(F32) for n in _SMALL]), "reduce_small"), small_shapes)
    for n, g in zip(_SMALL, red):
        grads[n] = g
    grads["rg_conv_w"] = lax.dynamic_slice_in_dim(grads["rg_conv_w"], chip * wc, wc, 2)
    delta, new_m, new_v = {}, {}, {}
    for n in _BIG:
        delta[n], new_m[n], new_v[n] = adamw(a[n], grads[n], a["m_" + n], a["v_" + n], "adamw_" + n)
    shapes = [a[n].shape for n in _SMALL]
    packed = [_pack([t[n] for n in _SMALL]) for t in
              (a, grads, {n: a["m_" + n] for n in _SMALL}, {n: a["v_" + n] for n in _SMALL})]
    for dst, buf in zip((delta, new_m, new_v), adamw(*packed, "adamw_small")):
        for n, t in zip(_SMALL, _unpack(buf, shapes)):
            dst[n] = t
    return (loss, gx[None], *[grads[n] for n in _WEIGHTS], *[delta[n] for n in _WEIGHTS],
            *[new_m[n] for n in _WEIGHTS], *[new_v[n] for n in _WEIGHTS])


_ARG_NAMES = ["x"] + _WEIGHTS + ["loss_target"] + ["m_" + n for n in _WEIGHTS] + ["v_" + n for n in _WEIGHTS]


def kernel(x, norm_w, w_in, s5_lambda_re, s5_lambda_im, s5_log_step, s5_b_re, s5_b_im, s5_c_re, s5_c_im, s5_d, s5_w_glu, s5_b_glu, rg_conv_w, rg_conv_b, rg_w_a, rg_b_a, rg_w_x, rg_b_x, rg_lambda, hg_lower_bounds, hg_norm_w, w_branch, w_out, final_norm_w, loss_target, m_norm_w, m_w_in, m_s5_lambda_re, m_s5_lambda_im, m_s5_log_step, m_s5_b_re, m_s5_b_im, m_s5_c_re, m_s5_c_im, m_s5_d, m_s5_w_glu, m_s5_b_glu, m_rg_conv_w, m_rg_conv_b, m_rg_w_a, m_rg_b_a, m_rg_w_x, m_rg_b_x, m_rg_lambda, m_hg_lower_bounds, m_hg_norm_w, m_w_branch, m_w_out, m_final_norm_w, v_norm_w, v_w_in, v_s5_lambda_re, v_s5_lambda_im, v_s5_log_step, v_s5_b_re, v_s5_b_im, v_s5_c_re, v_s5_c_im, v_s5_d, v_s5_w_glu, v_s5_b_glu, v_rg_conv_w, v_rg_conv_b, v_rg_w_a, v_rg_b_a, v_rg_w_x, v_rg_b_x, v_rg_lambda, v_hg_lower_bounds, v_hg_norm_w, v_w_branch, v_w_out, v_final_norm_w):
    vals = (x, norm_w, w_in, s5_lambda_re, s5_lambda_im, s5_log_step, s5_b_re, s5_b_im, s5_c_re, s5_c_im, s5_d, s5_w_glu, s5_b_glu, rg_conv_w, rg_conv_b, rg_w_a, rg_b_a, rg_w_x, rg_b_x, rg_lambda, hg_lower_bounds, hg_norm_w, w_branch, w_out, final_norm_w, loss_target, m_norm_w, m_w_in, m_s5_lambda_re, m_s5_lambda_im, m_s5_log_step, m_s5_b_re, m_s5_b_im, m_s5_c_re, m_s5_c_im, m_s5_d, m_s5_w_glu, m_s5_b_glu, m_rg_conv_w, m_rg_conv_b, m_rg_w_a, m_rg_b_a, m_rg_w_x, m_rg_b_x, m_rg_lambda, m_hg_lower_bounds, m_hg_norm_w, m_w_branch, m_w_out, m_final_norm_w, v_norm_w, v_w_in, v_s5_lambda_re, v_s5_lambda_im, v_s5_log_step, v_s5_b_re, v_s5_b_im, v_s5_c_re, v_s5_c_im, v_s5_d, v_s5_w_glu, v_s5_b_glu, v_rg_conv_w, v_rg_conv_b, v_rg_w_a, v_rg_b_a, v_rg_w_x, v_rg_b_x, v_rg_lambda, v_hg_lower_bounds, v_hg_norm_w, v_w_branch, v_w_out, v_final_norm_w)
    return _step(dict(zip(_ARG_NAMES, vals)))
```

```python
import functools
import math

import jax
import jax.numpy as jnp
from jax import lax
from jax.experimental import pallas as pl
from jax.experimental.pallas import tpu as pltpu

F32 = jnp.float32
BF16 = jnp.bfloat16
EPS = 1e-6
RG_C = 8.0
S5_GROUP = 16
S5_STATE = 64
RG_BLOCKS = 16
RG_CONV = 4
HG_HEADS = 8
N_CHIPS = 4
VMEM_LIMIT_BYTES = 56 * 1024 * 1024
NEG_BIG = -1e30

ADAM_LR = 0.001
ADAM_B1 = 0.9
ADAM_B2 = 0.999
ADAM_EPS = 1e-08
ADAM_WD = 0.01
ADAM_STEP = 10

MESH = pl.DeviceIdType.MESH


def _cparams(sem):
    return pltpu.CompilerParams(dimension_semantics=sem, vmem_limit_bytes=VMEM_LIMIT_BYTES)


_DOT_DIMS = {"nn": (((1,), (0,)), ((), ())), "nt": (((1,), (1,)), ((), ())), "tn": (((0,), (0,)), ((), ()))}


def _bdot_raw(a, b, form):
    return lax.dot_general(a.astype(BF16), b.astype(BF16), _DOT_DIMS[form], preferred_element_type=F32)


@functools.partial(jax.custom_vjp, nondiff_argnums=(2,))
def bdot(a, b, form):
    return _bdot_raw(a, b, form)


def _bdot_fwd(a, b, form):
    return _bdot_raw(a, b, form), (a, b)


def _bdot_bwd(form, res, g):
    a, b = res
    if form == "nn":
        da, db = _bdot_raw(g, b, "nt"), _bdot_raw(a, g, "tn")
    elif form == "nt":
        da, db = _bdot_raw(g, b, "nn"), _bdot_raw(g, a, "tn")
    else:
        da, db = _bdot_raw(b, g, "nt"), _bdot_raw(a, g, "nn")
    return da.astype(a.dtype), db.astype(b.dtype)


bdot.defvjp(_bdot_fwd, _bdot_bwd)


@functools.partial(jax.custom_vjp, nondiff_argnums=(1,))
def sroll(x, d):
    return pltpu.roll(x, d, 0)


def _sroll_fwd(x, d):
    return pltpu.roll(x, d, 0), None


def _sroll_bwd(d, _, g):
    return (pltpu.roll(g, g.shape[0] - d, 0),)


sroll.defvjp(_sroll_fwd, _sroll_bwd)


def _row_iota(n):
    return lax.broadcasted_iota(jnp.int32, (n, 1), 0)


def _last_row(x, ti):
    return jnp.sum(jnp.where(ti == x.shape[0] - 1, x, 0.0), axis=0, keepdims=True)


def _silu(x):
    return x * jax.nn.sigmoid(x)


class _Tiling:
    def __init__(self, CB, tT, nT, ncb=None, bmap=None):
        self.CB, self.tT, self.nT = CB, tT, nT
        self.ncb = CB if ncb is None else ncb
        self.bmap = (lambda cb, t: (t, cb)) if bmap is None else bmap


def _tiled_specs(params, views, carry_shapes, out_defs, tl, rev):
    nT = tl.nT
    tmap = (lambda t: nT - 1 - t) if rev else (lambda t: t)

    def rspec(col0, width):
        bw = width // tl.ncb
        off = col0 // bw

        def imap(cb, t):
            rb, cbk = tl.bmap(cb, tmap(t))
            return (rb, off + cbk)

        return pl.BlockSpec((tl.tT, bw), imap)

    p_specs = [pl.BlockSpec((None,) + p.shape[1:], lambda cb, t: (cb, 0, 0)) for p in params]
    r_specs = [rspec(c0, w) for (c0, w) in views]
    dr_specs = [rspec(0, w) for (_, w) in views]
    o_specs = [rspec(0, w) for (w, _) in out_defs]
    s_specs = [pl.BlockSpec((None, r, c), lambda cb, t: (tmap(t), 0, cb)) for (r, c) in carry_shapes]
    return p_specs, r_specs, dr_specs, o_specs, s_specs


def _tiled_fwd(f, name, params, rows, views, carry_shapes, out_defs, tl):
    T = rows[0].shape[0]
    CB, tT, nT = tl.CB, tl.tT, tl.nT
    n_p, n_r, n_o, n_c = len(params), len(rows), len(out_defs), len(carry_shapes)
    p_specs, r_specs, _, o_specs, s_specs = _tiled_specs(params, views, carry_shapes, out_defs, tl, False)

    def body(*refs):
        p_refs = refs[:n_p]
        r_refs = refs[n_p:n_p + n_r]
        o_refs = refs[n_p + n_r:n_p + n_r + n_o]
        s_refs = refs[n_p + n_r + n_o:n_p + n_r + n_o + n_c]
        c_refs = refs[n_p + n_r + n_o + n_c:]
        t = pl.program_id(1)

        @pl.when(t == 0)
        def _():
            for c in c_refs:
                c[...] = jnp.zeros_like(c)

        carries = [c[...] for c in c_refs]
        for s, cv in zip(s_refs, carries):
            s[...] = cv
        outs, newc = f([p[...] for p in p_refs], [r[...] for r in r_refs], carries, t * tT)
        for o, v in zip(o_refs, outs):
            o[...] = v.astype(o.dtype)
        for c, v in zip(c_refs, newc):
            c[...] = v

    out_shape = [jax.ShapeDtypeStruct((T, w), dt) for (w, dt) in out_defs]
    out_shape += [jax.ShapeDtypeStruct((nT, r, c * CB), F32) for (r, c) in carry_shapes]
    res = pl.pallas_call(
        body, name=name + "_fwd", grid=(CB, nT),
        in_specs=p_specs + r_specs, out_specs=o_specs + s_specs, out_shape=out_shape,
        scratch_shapes=[pltpu.VMEM((r, c), F32) for (r, c) in carry_shapes],
        compiler_params=_cparams(("arbitrary", "arbitrary")),
    )(*params, *rows)
    return list(res[:n_o]), list(res[n_o:])


def _tiled_bwd(f, name, params, rows, views, saved, douts, carry_shapes, out_defs, tl):
    T = rows[0].shape[0]
    CB, tT, nT = tl.CB, tl.tT, tl.nT
    n_p, n_r, n_o, n_c = len(params), len(rows), len(out_defs), len(carry_shapes)
    p_specs, r_specs, dr_specs, o_specs, s_specs = _tiled_specs(params, views, carry_shapes, out_defs, tl, True)
    out_dtypes = [dt for (_, dt) in out_defs]

    def body(*refs):
        i = 0
        p_refs = refs[i:i + n_p]; i += n_p
        r_refs = refs[i:i + n_r]; i += n_r
        s_refs = refs[i:i + n_c]; i += n_c
        g_refs = refs[i:i + n_o]; i += n_o
        dp_refs = refs[i:i + n_p]; i += n_p
        dr_refs = refs[i:i + n_r]; i += n_r
        dc_refs = refs[i:]
        t = pl.program_id(1)

        @pl.when(t == 0)
        def _():
            for c in dc_refs:
                c[...] = jnp.zeros_like(c)
            for d in dp_refs:
                d[...] = jnp.zeros_like(d)

        t0 = (nT - 1 - t) * tT

        def g(P, R, C):
            outs, newc = f(P, R, C, t0)
            return [o.astype(dt) for o, dt in zip(outs, out_dtypes)], list(newc)

        _, vjp = jax.vjp(g, [p[...] for p in p_refs], [r[...] for r in r_refs], [s[...] for s in s_refs])
        dP, dR, dC = vjp(([gr[...] for gr in g_refs], [c[...] for c in dc_refs]))
        for d, v in zip(dp_refs, dP):
            d[...] += v
        for d, v in zip(dr_refs, dR):
            d[...] = v.astype(d.dtype)
        for c, v in zip(dc_refs, dC):
            c[...] = v

    out_shape = [jax.ShapeDtypeStruct(p.shape, F32) for p in params]
    out_shape += [jax.ShapeDtypeStruct((T, w), r.dtype) for r, (_, w) in zip(rows, views)]
    res = pl.pallas_call(
        body, name=name + "_bwd", grid=(CB, nT),
        in_specs=p_specs + r_specs + s_specs + o_specs, out_specs=p_specs + dr_specs, out_shape=out_shape,
        scratch_shapes=[pltpu.VMEM((r, c), F32) for (r, c) in carry_shapes],
        compiler_params=_cparams(("arbitrary", "arbitrary")),
    )(*params, *rows, *saved, *douts)
    return list(res[:n_p]), list(res[n_p:])


def tiled_op(f, name, params, rows, carry_shapes, out_defs, CB, tT, tiling=None):
    arrs = [r[0] if isinstance(r, tuple) else r for r in rows]
    views = [(r[1], r[2]) if isinstance(r, tuple) else (0, r.shape[1]) for r in rows]
    tl = tiling if tiling is not None else _Tiling(CB, tT, arrs[0].shape[0] // tT)

    @jax.custom_vjp
    def op(params, arrs):
        return _tiled_fwd(f, name, params, arrs, views, carry_shapes, out_defs, tl)[0]

    def op_fwd(params, arrs):
        outs, saved = _tiled_fwd(f, name, params, arrs, views, carry_shapes, out_defs, tl)
        return outs, (params, arrs, saved)

    def op_bwd(res, douts):
        params, arrs, saved = res
        dP, dR = _tiled_bwd(f, name, params, arrs, views, saved, list(douts), carry_shapes, out_defs, tl)
        dR = [d if w == a.shape[1] else jnp.pad(d, ((0, 0), (c0, a.shape[1] - c0 - w)))
              for d, a, (c0, w) in zip(dR, arrs, views)]
        return dP, dR

    op.defvjp(op_fwd, op_bwd)
    return op(list(params), arrs)


def _pick(n, pref):
    for t in pref:
        if n % t == 0:
            return t
    return n


class _Side:
    def __init__(self, ins, outs, n_sems, start, finish):
        self.ins, self.outs, self.n_sems, self.start, self.finish = ins, outs, n_sems, start, finish


def _side_parts(side):
    if side is None:
        return [], [], []
    sems = [pltpu.SemaphoreType.DMA((side.n_sems,)), pltpu.SemaphoreType.DMA((side.n_sems,))]
    return list(side.ins), list(side.outs), sems


def _mm_nn(a, w, res, name, out_dtype=F32, side=None):
    M, K = a.shape
    S, _, Ns = w.shape
    tm = _pick(M, (1024, 512, 256, 128))
    tn = _pick(Ns, (512, 256, 128))
    nps = Ns // tn
    has_res = res is not None
    n_main = 3 if has_res else 2
    s_ins, s_outs, s_sems = _side_parts(side)
    ni, nj = M // tm, S * nps

    def body(*refs):
        a_ref, w_ref = refs[0], refs[1]
        o_ref = refs[n_main + len(s_ins)]
        if side is not None:
            si = refs[n_main:n_main + len(s_ins)]
            so = refs[n_main + len(s_ins) + 1:n_main + len(s_ins) + 1 + len(s_outs)]
            send, recv = refs[-2], refs[-1]
            i, j = pl.program_id(0), pl.program_id(1)

            @pl.when((i == 0) & (j == 0))
            def _():
                side.start(si, so, send, recv)

        acc = _bdot_raw(a_ref[...], w_ref[...], "nn")
        if has_res:
            acc = acc + refs[2][...]
        o_ref[...] = acc.astype(o_ref.dtype)
        if side is not None:
            @pl.when((i == ni - 1) & (j == nj - 1))
            def _():
                side.finish(si, so, send, recv)

    in_specs = [pl.BlockSpec((tm, K), lambda i, j: (i, 0)),
                pl.BlockSpec((None, K, tn), lambda i, j: (j // nps, 0, j % nps))]
    args = [a, w]
    if has_res:
        in_specs.append(pl.BlockSpec((tm, tn), lambda i, j: (i, j)))
        args.append(res)
    out = pl.pallas_call(
        body, name=name, grid=(ni, nj), in_specs=in_specs + [_ANY] * len(s_ins),
        out_specs=[pl.BlockSpec((tm, tn), lambda i, j: (i, j))] + [_ANY] * len(s_outs),
        out_shape=[jax.ShapeDtypeStruct((M, S * Ns), out_dtype)] + s_outs,
        scratch_shapes=s_sems,
        compiler_params=_cparams(("arbitrary", "arbitrary")),
    )(*args, *s_ins)
    return out[0] if side is None else out


def _mm_nt(g, w, out_dtype, name):
    M, N = g.shape
    S, K, Ns = w.shape
    tm = _pick(M, (1024, 512, 256, 128))
    tk = _pick(K, (1024, 512, 256, 128))
    tn = _pick(Ns, (1792, 1024, 512, 256, 128))
    nps = Ns // tn
    nn = S * nps

    def body(g_ref, w_ref, o_ref, acc_ref):
        n = pl.program_id(2)

        @pl.when(n == 0)
        def _():
            acc_ref[...] = jnp.zeros_like(acc_ref)

        acc_ref[...] += _bdot_raw(g_ref[...], w_ref[...], "nt")

        @pl.when(n == nn - 1)
        def _():
            o_ref[...] = acc_ref[...].astype(o_ref.dtype)

    return pl.pallas_call(
        body, name=name, grid=(M // tm, K // tk, nn),
        in_specs=[pl.BlockSpec((tm, tn), lambda i, k, n: (i, n)),
                  pl.BlockSpec((None, tk, tn), lambda i, k, n: (n // nps, k, n % nps))],
        out_specs=pl.BlockSpec((tm, tk), lambda i, k, n: (i, k)),
        out_shape=jax.ShapeDtypeStruct((M, K), out_dtype),
        scratch_shapes=[pltpu.VMEM((tm, tk), F32)],
        compiler_params=_cparams(("parallel", "parallel", "arbitrary")),
    )(g, w)


def _mm_tn(a, g, S, out_dtype, name, side=None):
    T, K = a.shape
    N = g.shape[1]
    Ns = N // S
    tk = _pick(K, (2048, 1024, 512, 256, 128))
    tn = _pick(Ns, (1024, 896, 512, 256, 128))
    tt = _pick(T, (1024, 512, 256, 128))
    nps = Ns // tn
    nt = T // tt
    nk, nj = K // tk, S * nps
    a_t = a.astype(BF16).T
    s_ins, s_outs, s_sems = _side_parts(side)

    def body(*refs):
        a_ref, g_ref = refs[0], refs[1]
        o_ref = refs[2 + len(s_ins)]
        acc_ref = refs[3 + len(s_ins) + len(s_outs)]
        k, j, t = pl.program_id(0), pl.program_id(1), pl.program_id(2)
        if side is not None:
            si = refs[2:2 + len(s_ins)]
            so = refs[3 + len(s_ins):3 + len(s_ins) + len(s_outs)]
            send, recv = refs[-2], refs[-1]

            @pl.when((k == 0) & (j == 0) & (t == 0))
            def _():
                side.start(si, so, send, recv)

        @pl.when(t == 0)
        def _():
            acc_ref[...] = jnp.zeros_like(acc_ref)

        acc_ref[...] += _bdot_raw(a_ref[...], g_ref[...], "nn")

        @pl.when(t == nt - 1)
        def _():
            o_ref[...] = acc_ref[...].astype(o_ref.dtype)

        if side is not None:
            @pl.when((k == nk - 1) & (j == nj - 1) & (t == nt - 1))
            def _():
                side.finish(si, so, send, recv)

    out = pl.pallas_call(
        body, name=name, grid=(nk, nj, nt),
        in_specs=[pl.BlockSpec((tk, tt), lambda k, j, t: (k, t)),
                  pl.BlockSpec((tt, tn), lambda k, j, t: (t, j))] + [_ANY] * len(s_ins),
        out_specs=[pl.BlockSpec((None, tk, tn), lambda k, j, t: (j // nps, k, j % nps))] + [_ANY] * len(s_outs),
        out_shape=[jax.ShapeDtypeStruct((S, K, Ns), out_dtype)] + s_outs,
        scratch_shapes=[pltpu.VMEM((tk, tn), F32)] + s_sems,
        compiler_params=_cparams(("arbitrary", "arbitrary", "arbitrary")),
    )(a_t, g, *s_ins)
    return out[0] if side is None else out


def linear(a, w, name, res=None):
    @jax.custom_vjp
    def op(a, w, res):
        return _mm_nn(a, w, res, name + "_fwd")

    def op_fwd(a, w, res):
        return _mm_nn(a, w, res, name + "_fwd"), (a, w)

    def op_bwd(saved, g):
        a, w = saved
        da = _mm_nt(g, w, a.dtype, name + "_bwd_a")
        dw = _mm_tn(a, g, w.shape[0], w.dtype, name + "_bwd_w")
        return da, dw, (None if res is None else g)

    op.defvjp(op_fwd, op_bwd)
    return op(a, w, res)


def _rms_tile(params, rows, carries, t0):
    (w,), (x,) = params, rows
    y = x * lax.rsqrt(jnp.mean(x * x, axis=-1, keepdims=True) + EPS) * w
    return [y], []


def _s5_tile(params, rows, carries, t0):
    t_m, s_re, s_im, r_re, r_im, apow_re, apow_im, d = params
    (u,) = rows
    n = u.shape[0]
    ti = _row_iota(n)
    pi = _row_iota(apow_re.shape[0])
    x_re, x_im = bdot(u, s_re, "nn"), bdot(u, s_im, "nn")
    k = 0
    while (1 << k) < n:
        sh = 1 << k
        p_re, p_im = _row_sel(apow_re, pi, k), _row_sel(apow_im, pi, k)
        q_re, q_im = sroll(x_re, sh), sroll(x_im, sh)
        m = ti >= sh
        x_re, x_im = (x_re + jnp.where(m, p_re * q_re - p_im * q_im, 0.0),
                      x_im + jnp.where(m, p_re * q_im + p_im * q_re, 0.0))
        k += 1
    x_re = jnp.where(ti >= 1, sroll(x_re, 1), 0.0)
    x_im = jnp.where(ti >= 1, sroll(x_im, 1), 0.0)
    y = bdot(u, t_m, "nn") + bdot(x_re, r_re, "nn") + bdot(x_im, r_im, "nn") + d * u
    return [jax.nn.gelu(y)], []


def _glu_tile(params, rows, carries, t0):
    (b,), (y, zg, ga) = params, rows
    return [y * jax.nn.sigmoid(zg + b) * _silu(ga)], []


def _neg_expm1(z):
    small = -(z * (1.0 + z * (0.5 + z * (1.0 / 6.0))))
    return jnp.where(z > -0.01, small, 1.0 - jnp.exp(z))


def _rg_tile(params, rows, carries, t0):
    conv_w, conv_b, w_a, b_a, w_x, b_x, sp = params
    x, gate = rows
    x_prev, h_prev = carries
    tT = x.shape[0]
    ti = _row_iota(tT)
    ci = _row_iota(RG_CONV)
    xc = _row_sel(conv_w, ci, RG_CONV - 1) * x + conv_b
    for k in range(1, RG_CONV):
        xs = jnp.where(ti >= k, sroll(x, k), sroll(x_prev, k))
        xc = xc + _row_sel(conv_w, ci, RG_CONV - 1 - k) * xs
    r = jax.nn.sigmoid(bdot(xc, w_a, "nn") + b_a)
    i = jax.nn.sigmoid(bdot(xc, w_x, "nn") + b_x)
    log_a = -RG_C * r * sp
    a = jnp.exp(log_a)
    mult = jnp.sqrt(_neg_expm1(2.0 * log_a))
    mult = jnp.where(ti + t0 == 0, 1.0, mult)
    b = mult * (i * xc)
    b = b + jnp.where(ti == 0, a * h_prev, 0.0)
    k = 1
    while k < tT:
        m = ti >= k
        b = b + jnp.where(m, a * sroll(b, k), 0.0)
        a = jnp.where(m, a * sroll(a, k), a)
        k *= 2
    return [b * _silu(gate)], [x, _last_row(b, ti)]


def _hg_tile(params, rows, carries, t0):
    lb, nw = params
    q, fl, v, gate = rows
    (st,) = carries
    tT = q.shape[0]
    ti = _row_iota(tT)
    qs = _silu(q)
    f = lb + (1.0 - lb) * jax.nn.sigmoid(fl)
    kk = 1.0 - f
    G = jnp.log(f)
    k = 1
    while k < tT:
        G = G + jnp.where(ti >= k, sroll(G, k), 0.0)
        k *= 2
    inter = bdot(qs * jnp.exp(G), st, "nt")
    tr = lax.broadcasted_iota(jnp.int32, (tT, tT), 0)
    sc = lax.broadcasted_iota(jnp.int32, (tT, tT), 1)
    attn = jnp.zeros((tT, tT), F32)
    blk = tT
    while blk > 8:
        sub = blk // 4
        for j in range(1, 4):
            ref = jnp.zeros_like(G)
            for b in range(tT // blk):
                row = _row_sel(G, ti, b * blk + sub * j - 1)
                ref = ref + jnp.where(_div2(ti, blk) == b, row, 0.0)
            tmask = _div2(_mod2(ti, blk), sub) == j
            smask = _mod2(ti, blk) < sub * j
            qt = qs * jnp.exp(jnp.where(tmask, G - ref, NEG_BIG))
            kt = kk * jnp.exp(jnp.where(smask, ref - G, NEG_BIG))
            aj = bdot(qt, kt, "nt")
            attn = attn + jnp.where(_div2(tr, blk) == _div2(sc, blk), aj, 0.0)
        blk = sub
    intra = bdot(attn, v, "nn")
    for d in range(blk):
        if d == 0:
            kd, gd, vd = kk, G, v
        else:
            kd, gd, vd = sroll(kk, d), sroll(G, d), sroll(v, d)
        m = _mod2(ti, blk) >= d
        w = jnp.sum(qs * kd * jnp.exp(jnp.where(m, G - gd, NEG_BIG)), axis=1, keepdims=True)
        intra = intra + w * vd
    o = inter + intra
    g_last = _last_row(G, ti)
    k_dec = kk * jnp.exp(g_last - G)
    st_new = st * jnp.exp(g_last) + bdot(v, k_dec, "tn")
    o = o * lax.rsqrt(jnp.mean(o * o, axis=-1, keepdims=True) + EPS) * nw
    return [o * _silu(gate)], [st_new]


def _row_sel(x, ti, r):
    return jnp.sum(jnp.where(ti == r, x, 0.0), axis=0, keepdims=True)


def _div2(i, p):
    return lax.shift_right_logical(i, jnp.int32(p.bit_length() - 1))


def _mod2(i, p):
    return lax.bitwise_and(i, jnp.int32(p - 1))


def _merge_tile(params, rows, carries, t0):
    b0, b1, b2, g0, g1, g2 = rows
    m = jax.nn.sigmoid(g0) * b0 + jax.nn.sigmoid(g1) * b1 + jax.nn.sigmoid(g2) * b2
    return [m], []


def _loss_tile(params, rows, carries, t0):
    (w,), (x, tgt) = params, rows
    y = x * lax.rsqrt(jnp.mean(x * x, axis=-1, keepdims=True) + EPS) * w
    e = y - tgt
    return [0.5 * jnp.mean(e * e, axis=-1, keepdims=True)], []


def _block_diag(w, cb):
    n, i, j = w.shape
    g = n // cb
    w4 = w.reshape(cb, g, i, j)
    eye = jnp.eye(g, dtype=w.dtype)
    return jnp.einsum("cgij,gk->cgikj", w4, eye).reshape(cb, g * i, g * j)


def _s5_params(lam_re, lam_im, log_step, b_re, b_im, c_re, c_im, d, levels):
    G, P = lam_re.shape
    step = jnp.exp(log_step)[:, None]
    mag = jnp.exp(lam_re * step)
    ang = lam_im * step
    abar_re = mag * jnp.cos(ang)
    abar_im = mag * jnp.sin(ang)
    num_re = abar_re - 1.0
    num_im = abar_im
    den = lam_re * lam_re + lam_im * lam_im
    coef_re = (num_re * lam_re + num_im * lam_im) / den
    coef_im = (num_im * lam_re - num_re * lam_im) / den
    bbar_re = coef_re[..., None] * b_re - coef_im[..., None] * b_im
    bbar_im = coef_re[..., None] * b_im + coef_im[..., None] * b_re
    H = b_re.shape[2]
    Lc = S5_LC
    hi = lax.Precision.HIGHEST

    def powers(ks):
        ks = jnp.asarray(ks, F32)[:, None, None]
        m = jnp.exp(ks * (lam_re * step))
        return m * jnp.cos(ks * ang), m * jnp.sin(ks * ang)

    pw_re, pw_im = powers(list(range(Lc + 1)))
    ab_re = pw_re[..., None] * bbar_re - pw_im[..., None] * bbar_im
    ab_im = pw_re[..., None] * bbar_im + pw_im[..., None] * bbar_re
    kern = (jnp.einsum("gap,kgph->gkha", c_re, ab_re[:Lc], precision=hi)
            - jnp.einsum("gap,kgph->gkha", c_im, ab_im[:Lc], precision=hi))
    kk = jnp.arange(Lc)[:, None, None]
    jj = jnp.arange(Lc)[None, :, None]
    ii = jnp.arange(Lc)[None, None, :]
    place = (ii - jj == kk).astype(F32)
    t_m = jnp.einsum("gkha,kji->gjhia", kern, place, precision=hi).reshape(G, Lc * H, Lc * H)
    s_re = ab_re[:Lc][::-1].transpose(1, 0, 3, 2).reshape(G, Lc * H, P)
    s_im = ab_im[:Lc][::-1].transpose(1, 0, 3, 2).reshape(G, Lc * H, P)
    m_re = c_re[None] * pw_re[1:, :, None, :] - c_im[None] * pw_im[1:, :, None, :]
    m_im = c_re[None] * pw_im[1:, :, None, :] + c_im[None] * pw_re[1:, :, None, :]
    r_re = m_re.transpose(1, 3, 0, 2).reshape(G, P, Lc * H)
    r_im = -m_im.transpose(1, 3, 0, 2).reshape(G, P, Lc * H)
    rows = max(8, levels)
    ap_re, ap_im = powers([Lc * (1 << k) for k in range(levels)] + [0] * (rows - levels))
    dd = jnp.tile(d.reshape(G, 1, H), (1, 1, Lc))
    return [t_m, s_re, s_im, r_re, r_im, ap_re.transpose(1, 0, 2), ap_im.transpose(1, 0, 2), dd]


_LANE = 128


def _lane_perm_matrix(lc, h):
    n = lc * _LANE
    src = jnp.arange(n).reshape(lc, _LANE // h, h).transpose(1, 0, 2).reshape(n)
    return (jnp.arange(n)[:, None] == src[None, :]).astype(BF16)


def _lane_perm(x, p, p_t, name):
    @jax.custom_vjp
    def op(x):
        return _mm_nn(x, p[None], None, name, BF16)

    def op_fwd(x):
        return _mm_nn(x, p[None], None, name, BF16), None

    def op_bwd(_, g):
        return (_mm_nn(g, p_t[None], None, name + "_t", x.dtype),)

    op.defvjp(op_fwd, op_bwd)
    return op(x)


def _to_chunks(u, lc, h, perm):
    t, w = u.shape
    x = u.reshape(t // lc, lc, w // _LANE, _LANE).transpose(2, 0, 1, 3).reshape((w // _LANE) * (t // lc), lc * _LANE)
    return _lane_perm(x, perm, perm.T, "s5_to_chunks")


def _from_chunks(y, lc, w, perm):
    tiles = w // _LANE
    n_chunks = y.shape[0] // tiles
    x = _lane_perm(y, perm.T, perm, "s5_from_chunks")
    return x.reshape(tiles, n_chunks, lc, _LANE).transpose(1, 2, 0, 3).reshape(n_chunks * lc, w)


def _vec(v, cb):
    return v.reshape(cb, 1, -1)


S5_LC = 16
RG_CB, RG_TT = 4, 256
HG_TT = 128
ROW_TT = 256
MERGE_TT = 128


_LAYER_SMALL = ['s5_lambda_re', 's5_lambda_im', 's5_log_step', 's5_b_re', 's5_b_im', 's5_c_re', 's5_c_im', 's5_d',
                's5_b_glu', 'rg_conv_w', 'rg_conv_b', 'rg_w_a', 'rg_b_a', 'rg_w_x', 'rg_b_x', 'rg_lambda', 'hg_norm_w']


def _lower_bounds(hg_lower_bounds):
    lb_sm = jax.nn.softmax(hg_lower_bounds, axis=0)
    return jnp.cumsum(lb_sm, axis=0) - lb_sm[0]


def _rms_bf16(x, w):
    T, D = x.shape
    return tiled_op(_rms_tile, "rms", [w.reshape(1, 1, D)], [x], [], [(D, BF16)], 1, min(ROW_TT, T))[0]


def _loss_rows(x, w, target):
    T, D = x.shape
    return tiled_op(_loss_tile, "loss", [w.reshape(1, 1, D)], [x, target], [], [(1, F32)], 1, min(ROW_TT, T))[0]


def _layer_tail(z, x, lw):
    T, D = x.shape
    W = lw["s5_d"].shape[0]
    row_tt = min(ROW_TT, T)
    rg_tt, hg_tt = min(RG_TT, T), min(HG_TT, T)
    n_chunks = T // S5_LC
    gpt = _LANE // S5_GROUP
    perm = _lane_perm_matrix(S5_LC, S5_GROUP)
    s5_tiling = _Tiling(W // S5_GROUP, n_chunks, 1, ncb=gpt, bmap=lambda cb, t: (cb // gpt, cb % gpt))
    g_a, x_b, g_b, q_c, f_c, i_c, g_c = [(z, k * W, W) for k in range(1, 8)]
    gl = [(z, 8 * W + n * D, D) for n in range(3)]
    s5p = _s5_params(lw["s5_lambda_re"], lw["s5_lambda_im"], lw["s5_log_step"], lw["s5_b_re"], lw["s5_b_im"],
                     lw["s5_c_re"], lw["s5_c_im"], lw["s5_d"], int(math.log2(n_chunks)))
    (y1c,) = tiled_op(_s5_tile, "s5", s5p, [_to_chunks(z[:, :W], S5_LC, S5_GROUP, perm)], [],
                      [(S5_LC * _LANE, BF16)], W // S5_GROUP, n_chunks, tiling=s5_tiling)
    y1 = _from_chunks(y1c, S5_LC, W, perm)
    zg = linear(y1, lw["s5_w_glu"].reshape(1, W, W), "w_glu")
    (y_a,) = tiled_op(_glu_tile, "glu", [lw["s5_b_glu"].reshape(1, 1, W)], [y1, zg, g_a], [], [(W, BF16)], 1, row_tt)
    rgp = [lw["rg_conv_w"].reshape(RG_CONV, RG_CB, W // RG_CB).transpose(1, 0, 2),
           _vec(lw["rg_conv_b"], RG_CB), _block_diag(lw["rg_w_a"], RG_CB), _vec(lw["rg_b_a"], RG_CB),
           _block_diag(lw["rg_w_x"], RG_CB), _vec(lw["rg_b_x"], RG_CB), _vec(jax.nn.softplus(-lw["rg_lambda"]), RG_CB)]
    rc = W // RG_CB
    (y_b,) = tiled_op(_rg_tile, "rg", rgp, [x_b, g_b], [(rg_tt, rc), (1, rc)], [(W, BF16)], RG_CB, rg_tt)
    dk = W // HG_HEADS
    hgp = [_vec(lw["lbs"], HG_HEADS), _vec(lw["hg_norm_w"], HG_HEADS)]
    (y_c,) = tiled_op(_hg_tile, "hg", hgp, [q_c, f_c, i_c, g_c], [(dk, dk)], [(W, BF16)], HG_HEADS, hg_tt)
    br = [linear(y, lw["w_branch"][n], "w_br") for n, y in enumerate((y_a, y_b, y_c))]
    (mg,) = tiled_op(_merge_tile, "merge", [], br + gl, [], [(D, BF16)], 1, min(MERGE_TT, T))
    return linear(mg, lw["w_out"].reshape(1, D, D), "w_out", res=x)


_ANY = pl.BlockSpec(memory_space=pl.ANY)


def _place():
    x, y, c = lax.axis_index("x"), lax.axis_index("y"), lax.axis_index("c")
    chips = [(1 - x, y), (x, 1 - y), (1 - x, 1 - y)]
    return x, y, c, chips


def _rcopy(src, dst, send_sems, recv_sems, k, to):
    return pltpu.make_async_remote_copy(src_ref=src, dst_ref=dst, send_sem=send_sems.at[k], recv_sem=recv_sems.at[k],
                                        device_id=to, device_id_type=MESH)


def all_gather_big(w, name):
    Lp, M, N = w.shape
    Lh = Lp // 2

    def body(w_ref, o_ref, send_sems, recv_sems):
        x, y, c, chips = _place()
        sibling = (x, y, 1 - c)
        me = 2 * x + y

        def blk(h, s):
            return o_ref.at[pl.ds(h * Lh, Lh), s]

        first = [_rcopy(w_ref.at[pl.ds(c * Lh, Lh)], blk(c, me), send_sems, recv_sems, j, (*chip, c))
                 for j, chip in enumerate(chips)]
        for cp in first:
            cp.start()
        passed = []
        for j, chip in enumerate(chips):
            s = 2 * chip[0] + chip[1]
            _rcopy(blk(c, s), blk(c, s), send_sems, recv_sems, j, (*chip, c)).wait_recv()
            cp = _rcopy(blk(c, s), blk(c, s), send_sems, recv_sems, 3 + j, sibling)
            cp.start()
            passed.append(cp)
        for j, chip in enumerate(chips):
            s = 2 * chip[0] + chip[1]
            _rcopy(blk(1 - c, s), blk(1 - c, s), send_sems, recv_sems, 3 + j, sibling).wait_recv()
        for cp in first + passed:
            cp.wait_send()

    full = pl.pallas_call(
        body, name=name, in_specs=[_ANY], out_specs=_ANY,
        out_shape=jax.ShapeDtypeStruct((Lp, N_CHIPS, M, N), w.dtype),
        scratch_shapes=[pltpu.SemaphoreType.DMA((6,)), pltpu.SemaphoreType.DMA((6,))],
    )(w)
    chip = 2 * lax.axis_index("x") + lax.axis_index("y")
    return lax.dynamic_update_slice(full, w[:, None], (0, chip, 0, 0))


def _swap_half(g, name):
    Lp = g.shape[0]
    Lh = Lp // 2

    def body(g_ref, o_ref, send_sems, recv_sems):
        x, y, c, _ = _place()
        cp = _rcopy(g_ref.at[pl.ds((1 - c) * Lh, Lh)], o_ref, send_sems, recv_sems, 0, (x, y, 1 - c))
        cp.start()
        cp.wait()

    return pl.pallas_call(
        body, name=name, in_specs=[_ANY], out_specs=_ANY,
        out_shape=jax.ShapeDtypeStruct((Lh,) + g.shape[1:], g.dtype),
        scratch_shapes=[pltpu.SemaphoreType.DMA((1,)), pltpu.SemaphoreType.DMA((1,))],
    )(g)


def _scatter_chips(p, name):
    Lh, _, M, N = p.shape

    def body(p_ref, o_ref, send_sems, recv_sems):
        x, y, c, chips = _place()
        cps = [_rcopy(p_ref.at[:, 2 * chip[0] + chip[1]], o_ref.at[j], send_sems, recv_sems, j, (*chip, c))
               for j, chip in enumerate(chips)]
        for cp in cps:
            cp.start()
        for cp in cps:
            cp.wait()

    return pl.pallas_call(
        body, name=name, in_specs=[_ANY], out_specs=_ANY,
        out_shape=jax.ShapeDtypeStruct((3, Lh, M, N), p.dtype),
        scratch_shapes=[pltpu.SemaphoreType.DMA((3,)), pltpu.SemaphoreType.DMA((3,))],
    )(p)


def _join_halves(gh, name):
    Lh, M, N = gh.shape

    def body(g_ref, o_ref, send_sems, recv_sems):
        x, y, c, _ = _place()
        cp = _rcopy(g_ref, o_ref, send_sems, recv_sems, 0, (x, y, 1 - c))
        cp.start()
        cp.wait()

    other = pl.pallas_call(
        body, name=name, in_specs=[_ANY], out_specs=_ANY,
        out_shape=jax.ShapeDtypeStruct((Lh, M, N), gh.dtype),
        scratch_shapes=[pltpu.SemaphoreType.DMA((1,)), pltpu.SemaphoreType.DMA((1,))],
    )(gh)
    south = lax.axis_index("c") == 0
    return jnp.concatenate([jnp.where(south, gh, other), jnp.where(south, other, gh)], 0)


def _gather_side(w):
    shape = (N_CHIPS,) + w.shape

    def first(src, land, send, recv):
        x, y, c, chips = _place()
        me = 2 * x + y
        return [_rcopy(src.at[c], land.at[me, c], send, recv, j, (*chip, c)) for j, chip in enumerate(chips)]

    def start(ins, outs, send, recv):
        for cp in first(ins[0], outs[0], send, recv):
            cp.start()

    def finish(ins, outs, send, recv):
        land = outs[0]
        x, y, c, chips = _place()
        sibling = (x, y, 1 - c)
        passed = []
        for j, chip in enumerate(chips):
            s = 2 * chip[0] + chip[1]
            _rcopy(land.at[s, c], land.at[s, c], send, recv, j, (*chip, c)).wait_recv()
            cp = _rcopy(land.at[s, c], land.at[s, c], send, recv, 3 + j, sibling)
            cp.start()
            passed.append(cp)
        for j, chip in enumerate(chips):
            s = 2 * chip[0] + chip[1]
            _rcopy(land.at[s, 1 - c], land.at[s, 1 - c], send, recv, 3 + j, sibling).wait_recv()
        for cp in first(ins[0], land, send, recv) + passed:
            cp.wait_send()

    return _Side([w], [jax.ShapeDtypeStruct(shape, w.dtype)], 6, start, finish)


def _place_own(land, w):
    chip = 2 * lax.axis_index("x") + lax.axis_index("y")
    full = lax.dynamic_update_slice(land, w[None], (chip, 0, 0, 0))
    return full.reshape(N_CHIPS, 2 * w.shape[1], w.shape[2])


def gather_rows(w, name):
    side = _gather_side(w)

    def body(w_ref, o_ref, send, recv):
        side.start([w_ref], [o_ref], send, recv)
        side.finish([w_ref], [o_ref], send, recv)

    return pl.pallas_call(
        body, name=name, in_specs=[_ANY], out_specs=_ANY, out_shape=side.outs[0],
        scratch_shapes=[pltpu.SemaphoreType.DMA((6,)), pltpu.SemaphoreType.DMA((6,))],
    )(w)


def _scatter_side(p):
    def copies(src, dst, send, recv):
        x, y, c, chips = _place()
        return [_rcopy(src.at[2 * chip[0] + chip[1]], dst.at[j], send, recv, j, (*chip, c))
                for j, chip in enumerate(chips)]

    def start(ins, outs, send, recv):
        for cp in copies(ins[0], outs[0], send, recv):
            cp.start()

    def finish(ins, outs, send, recv):
        for cp in copies(ins[0], outs[0], send, recv):
            cp.wait()

    return _Side([p], [jax.ShapeDtypeStruct((3,) + p.shape[1:], p.dtype)], 3, start, finish)


def scatter_rows(p, name):
    side = _scatter_side(p)

    def body(p_ref, o_ref, send, recv):
        side.start([p_ref], [o_ref], send, recv)
        side.finish([p_ref], [o_ref], send, recv)

    return pl.pallas_call(
        body, name=name, in_specs=[_ANY], out_specs=_ANY, out_shape=side.outs[0],
        scratch_shapes=[pltpu.SemaphoreType.DMA((3,)), pltpu.SemaphoreType.DMA((3,))],
    )(p)


def _swap_rows(g, name):
    def body(g_ref, o_ref, send_sems, recv_sems):
        x, y, c, _ = _place()
        cp = _rcopy(g_ref.at[:, 1 - c], o_ref, send_sems, recv_sems, 0, (x, y, 1 - c))
        cp.start()
        cp.wait()

    return pl.pallas_call(
        body, name=name, in_specs=[_ANY], out_specs=_ANY,
        out_shape=jax.ShapeDtypeStruct((g.shape[0],) + g.shape[2:], g.dtype),
        scratch_shapes=[pltpu.SemaphoreType.DMA((1,)), pltpu.SemaphoreType.DMA((1,))],
    )(g)


def _swap_whole(g, name):
    def body(g_ref, o_ref, send_sems, recv_sems):
        x, y, c, _ = _place()
        cp = _rcopy(g_ref, o_ref, send_sems, recv_sems, 0, (x, y, 1 - c))
        cp.start()
        cp.wait()

    return pl.pallas_call(
        body, name=name, in_specs=[_ANY], out_specs=_ANY, out_shape=jax.ShapeDtypeStruct(g.shape, g.dtype),
        scratch_shapes=[pltpu.SemaphoreType.DMA((1,)), pltpu.SemaphoreType.DMA((1,))],
    )(g)


def _ew_call(fn, ins, out_dtypes, name):
    shape = ins[0].shape
    n = shape[-1]
    ins2 = [a.reshape(-1, n) for a in ins]
    rows = ins2[0].shape[0]
    tr = _pick(rows, (256, 128, 64, 32, 16, 8)) if n <= 2048 else _pick(rows, (128, 64, 32, 16, 8))
    n_in = len(ins2)

    def body(*refs):
        outs = fn(*[r[...] for r in refs[:n_in]])
        for o, v in zip(refs[n_in:], outs):
            o[...] = v.astype(o.dtype)

    spec = pl.BlockSpec((tr, n), lambda i: (i, 0))
    res = pl.pallas_call(
        body, name=name, grid=(rows // tr,), in_specs=[spec] * n_in, out_specs=[spec] * len(out_dtypes),
        out_shape=[jax.ShapeDtypeStruct((rows, n), dt) for dt in out_dtypes],
        compiler_params=_cparams(("parallel",)),
    )(*ins2)
    return [r.reshape(shape) for r in res]


def reduce_big(g, name):
    Lp = g.shape[0]
    Lh = Lp // 2
    x, y, c = lax.axis_index("x"), lax.axis_index("y"), lax.axis_index("c")
    got = _swap_half(g, name + "_swap")
    mine = lax.dynamic_slice_in_dim(g, c * Lh, Lh, 0)
    (pair,) = _ew_call(lambda a, b: [a.astype(F32) + b.astype(F32)], [mine, got], [BF16], name + "_pair")
    recv = _scatter_chips(pair, name + "_scatter")
    own = lax.dynamic_index_in_dim(pair, 2 * x + y, 1, keepdims=False)
    (gh,) = _ew_call(lambda a, r0, r1, r2: [((a.astype(F32) + r0.astype(F32)) + r1.astype(F32)) + r2.astype(F32)],
                     [own, recv[0], recv[1], recv[2]], [F32], name + "_sum")
    return _join_halves(gh, name + "_join")


def all_reduce_small(buf, name):
    _, R, Ln = buf.shape

    def body(in_ref, out_ref, recv_ref, send_a, recv_a, send_b, recv_b):
        x, y, c = lax.axis_index("x"), lax.axis_index("y"), lax.axis_index("c")
        me = 4 * x + 2 * y + c
        peers = []
        for r in range(1, 8):
            px, py, pc = x ^ ((r >> 2) & 1), y ^ ((r >> 1) & 1), c ^ (r & 1)
            peers.append((r, (px, py, pc), 4 * px + 2 * py + pc))
        cps = [_rcopy(in_ref.at[idx], recv_ref.at[r], send_a, recv_a, r, to) for r, to, idx in peers]
        for cp in cps:
            cp.start()
        for cp in cps:
            cp.wait()
        acc = in_ref[me]
        for r in range(1, 8):
            acc = acc + recv_ref[r]
        out_ref[me] = acc
        cps = [_rcopy(out_ref.at[me], out_ref.at[me], send_b, recv_b, r, to) for r, to, idx in peers]
        for cp in cps:
            cp.start()
        for (r, to, idx), cp in zip(peers, cps):
            cp.wait_send()
            _rcopy(out_ref.at[idx], out_ref.at[idx], send_b, recv_b, r, to).wait_recv()

    vm = pl.BlockSpec(memory_space=pltpu.VMEM)
    return pl.pallas_call(
        body, name=name, in_specs=[vm], out_specs=vm,
        out_shape=jax.ShapeDtypeStruct(buf.shape, F32),
        scratch_shapes=[pltpu.VMEM(buf.shape, F32)] + [pltpu.SemaphoreType.DMA((8,))] * 4,
        compiler_params=pltpu.CompilerParams(vmem_limit_bytes=VMEM_LIMIT_BYTES),
    )(buf)


def _adamw_math(w, g, m, v):
    m = ADAM_B1 * m + (1.0 - ADAM_B1) * g
    v = ADAM_B2 * v + (1.0 - ADAM_B2) * (g * g)
    m_hat = m / (1.0 - ADAM_B1 ** ADAM_STEP)
    v_hat = v / (1.0 - ADAM_B2 ** ADAM_STEP)
    delta = -ADAM_LR * (m_hat / (jnp.sqrt(v_hat) + ADAM_EPS) + ADAM_WD * w)
    return [delta, m, v]


def adamw(w, g, m, v, name):
    return _ew_call(_adamw_math, [w, g, m, v], [F32, F32, F32], name)


_WEIGHTS = ['norm_w', 'w_in', 's5_lambda_re', 's5_lambda_im', 's5_log_step', 's5_b_re', 's5_b_im', 's5_c_re', 's5_c_im',
            's5_d', 's5_w_glu', 's5_b_glu', 'rg_conv_w', 'rg_conv_b', 'rg_w_a', 'rg_b_a', 'rg_w_x', 'rg_b_x', 'rg_lambda',
            'hg_lower_bounds', 'hg_norm_w', 'w_branch', 'w_out', 'final_norm_w']
_BIG = ('w_in', 's5_w_glu', 'w_branch', 'w_out')
_SMALL = [n for n in _WEIGHTS if n not in _BIG]
_LANES = 128
_N_DEV = 8


def _pack(arrs):
    flat = jnp.concatenate([a.reshape(-1) for a in arrs])
    unit = _N_DEV * 8 * _LANES
    total = -(-flat.shape[0] // unit) * unit
    flat = jnp.pad(flat, (0, total - flat.shape[0]))
    return flat.reshape(_N_DEV, total // (_N_DEV * _LANES), _LANES)


def _unpack(buf, shapes):
    flat = buf.reshape(-1)
    out, off = [], 0
    for s in shapes:
        n = math.prod(s)
        out.append(flat[off:off + n].reshape(s))
        off += n
    return out


def _step(a):
    x_idx, y_idx, c_idx = lax.axis_index("x"), lax.axis_index("y"), lax.axis_index("c")
    chip = 2 * x_idx + y_idx
    L = a["norm_w"].shape[0]
    W = a["s5_d"].shape[1]
    cw = a["rg_conv_w"]
    wc = cw.shape[2]
    placed = lax.dynamic_update_slice(jnp.zeros((L, RG_CONV, W), F32), cw, (0, 0, chip * wc))
    placed = placed * (c_idx == 0).astype(F32)
    conv_full = _unpack(all_reduce_small(_pack([placed]), "gather_conv"), [(L, RG_CONV, W)])[0]
    shard3 = {"s5_w_glu": a["s5_w_glu"], "w_out": a["w_out"],
              "w_branch": a["w_branch"].reshape((L * 3,) + a["w_branch"].shape[2:])}
    wts = {n: a[n] for n in _SMALL}
    wts["rg_conv_w"] = conv_full
    rest = {n: all_gather_big(shard3[n].astype(BF16), "gather_" + n) for n in shard3}
    lbs, vjp_lbs = jax.vjp(_lower_bounds, a["hg_lower_bounds"])
    D = a["w_in"].shape[1]
    w_in = a["w_in"].astype(BF16).reshape(L, 2, D // 2, a["w_in"].shape[2])
    x = a["x"][0]
    land = gather_rows(w_in[0], "gather_w_in_first")
    saved = []
    for l in range(L):
        h, vjp_rms = jax.vjp(_rms_bf16, x, a["norm_w"][l])
        w_l = _place_own(land, w_in[l])
        if l + 1 < L:
            z, land = _mm_nn(h, w_l, None, "w_in_fwd_gather", side=_gather_side(w_in[l + 1]))
        else:
            z = _mm_nn(h, w_l, None, "w_in_fwd")
        lw = {n: wts[n][l] for n in _LAYER_SMALL}
        lw.update(lbs=lbs[l], s5_w_glu=rest["s5_w_glu"][l], w_out=rest["w_out"][l],
                  w_branch=[rest["w_branch"][3 * l + n] for n in range(3)])
        x_next, vjp_tail = jax.vjp(_layer_tail, z, x, lw)
        saved.append((h, w_l, vjp_rms, vjp_tail))
        x = x_next
    rows, vjp_loss = jax.vjp(_loss_rows, x, a["final_norm_w"], a["loss_target"][0])
    loss = lax.psum(jnp.sum(rows), ("x", "y", "c"))
    dx, d_final, _ = vjp_loss(jnp.ones_like(rows))
    gw = {n: [None] * L for n in _LAYER_SMALL + ["lbs", "norm_w", "s5_w_glu", "w_out"]}
    gw["w_branch"] = [None] * (3 * L)
    pairs, recvs = [None] * L, [None] * L
    for l in reversed(range(L)):
        h, w_l, vjp_rms, vjp_tail = saved[l]
        dz, dx_res, dlw = vjp_tail(dx)
        dh = _mm_nt(dz, w_l, h.dtype, "w_in_bwd_a")
        if l + 1 < L:
            dw, recvs[l + 1] = _mm_tn(h, dz, N_CHIPS, BF16, "w_in_bwd_w_scatter", side=_scatter_side(pairs[l + 1]))
        else:
            dw = _mm_tn(h, dz, N_CHIPS, BF16, "w_in_bwd_w")
        dx_rms, gw["norm_w"][l] = vjp_rms(dh)
        dx = dx_rms + dx_res
        for n in _LAYER_SMALL + ["lbs", "s5_w_glu", "w_out"]:
            gw[n][l] = dlw[n]
        for n in range(3):
            gw["w_branch"][3 * l + n] = dlw["w_branch"][n]
        dw = dw.reshape(N_CHIPS, 2, D // 2, dw.shape[2])
        got = _swap_rows(dw, "reduce_w_in_swap")
        mine = lax.dynamic_index_in_dim(dw, c_idx, 1, keepdims=False)
        (pairs[l],) = _ew_call(lambda p, q: [p.astype(F32) + q.astype(F32)], [mine, got], [BF16], "reduce_w_in_pair")
    recvs[0] = scatter_rows(pairs[0], "reduce_w_in_scatter")
    halves = []
    for l in range(L):
        own = lax.dynamic_index_in_dim(pairs[l], chip, 0, keepdims=False)
        (gh,) = _ew_call(lambda p, r0, r1, r2: [((p.astype(F32) + r0.astype(F32)) + r1.astype(F32)) + r2.astype(F32)],
                         [own, recvs[l][0], recvs[l][1], recvs[l][2]], [F32], "reduce_w_in_sum")
        halves.append(gh)
    gh = jnp.stack(halves, 0)
    other = _swap_whole(gh, "reduce_w_in_join")
    south = c_idx == 0
    grads = {"w_in": jnp.stack([jnp.where(south, gh, other), jnp.where(south, other, gh)], 1).reshape(a["w_in"].shape)}
    gx = dx
    (d_lb,) = vjp_lbs(jnp.stack(gw["lbs"], 0))
    gw = {n: (jnp.stack(v, 0) if n not in ("s5_w_glu", "w_out", "w_branch") else v) for n, v in gw.items()}
    gw["hg_lower_bounds"] = d_lb
    gw["final_norm_w"] = d_final
    for n in ("s5_w_glu", "w_branch", "w_out"):
        red = reduce_big(jnp.stack(gw[n], 0), "reduce_" + n)
        grads[n] = red.reshape(a[n].shape)
    small_shapes = [gw[n].shape for n in _SMALL]
    red = _unpack(all_reduce_small(_pack([gw[n].astype(F32) for n in _SMALL]), "reduce_small"), small_shapes)
    for n, g in zip(_SMALL, red):
        grads[n] = g
    grads["rg_conv_w"] = lax.dynamic_slice_in_dim(grads["rg_conv_w"], chip * wc, wc, 2)
    delta, new_m, new_v = {}, {}, {}
    for n in _BIG:
        delta[n], new_m[n], new_v[n] = adamw(a[n], grads[n], a["m_" + n], a["v_" + n], "adamw_" + n)
    shapes = [a[n].shape for n in _SMALL]
    packed = [_pack([t[n] for n in _SMALL]) for t in
              (a, grads, {n: a["m_" + n] for n in _SMALL}, {n: a["v_" + n] for n in _SMALL})]
    for dst, buf in zip((delta, new_m, new_v), adamw(*packed, "adamw_small")):
        for n, t in zip(_SMALL, _unpack(buf, shapes)):
            dst[n] = t
    return (loss, gx[None], *[grads[n] for n in _WEIGHTS], *[delta[n] for n in _WEIGHTS],
            *[new_m[n] for n in _WEIGHTS], *[new_v[n] for n in _WEIGHTS])


_ARG_NAMES = ["x"] + _WEIGHTS + ["loss_target"] + ["m_" + n for n in _WEIGHTS] + ["v_" + n for n in _WEIGHTS]


def kernel(x, norm_w, w_in, s5_lambda_re, s5_lambda_im, s5_log_step, s5_b_re, s5_b_im, s5_c_re, s5_c_im, s5_d, s5_w_glu, s5_b_glu, rg_conv_w, rg_conv_b, rg_w_a, rg_b_a, rg_w_x, rg_b_x, rg_lambda, hg_lower_bounds, hg_norm_w, w_branch, w_out, final_norm_w, loss_target, m_norm_w, m_w_in, m_s5_lambda_re, m_s5_lambda_im, m_s5_log_step, m_s5_b_re, m_s5_b_im, m_s5_c_re, m_s5_c_im, m_s5_d, m_s5_w_glu, m_s5_b_glu, m_rg_conv_w, m_rg_conv_b, m_rg_w_a, m_rg_b_a, m_rg_w_x, m_rg_b_x, m_rg_lambda, m_hg_lower_bounds, m_hg_norm_w, m_w_branch, m_w_out, m_final_norm_w, v_norm_w, v_w_in, v_s5_lambda_re, v_s5_lambda_im, v_s5_log_step, v_s5_b_re, v_s5_b_im, v_s5_c_re, v_s5_c_im, v_s5_d, v_s5_w_glu, v_s5_b_glu, v_rg_conv_w, v_rg_conv_b, v_rg_w_a, v_rg_b_a, v_rg_w_x, v_rg_b_x, v_rg_lambda, v_hg_lower_bounds, v_hg_norm_w, v_w_branch, v_w_out, v_final_norm_w):
    vals = (x, norm_w, w_in, s5_lambda_re, s5_lambda_im, s5_log_step, s5_b_re, s5_b_im, s5_c_re, s5_c_im, s5_d, s5_w_glu, s5_b_glu, rg_conv_w, rg_conv_b, rg_w_a, rg_b_a, rg_w_x, rg_b_x, rg_lambda, hg_lower_bounds, hg_norm_w, w_branch, w_out, final_norm_w, loss_target, m_norm_w, m_w_in, m_s5_lambda_re, m_s5_lambda_im, m_s5_log_step, m_s5_b_re, m_s5_b_im, m_s5_c_re, m_s5_c_im, m_s5_d, m_s5_w_glu, m_s5_b_glu, m_rg_conv_w, m_rg_conv_b, m_rg_w_a, m_rg_b_a, m_rg_w_x, m_rg_b_x, m_rg_lambda, m_hg_lower_bounds, m_hg_norm_w, m_w_branch, m_w_out, m_final_norm_w, v_norm_w, v_w_in, v_s5_lambda_re, v_s5_lambda_im, v_s5_log_step, v_s5_b_re, v_s5_b_im, v_s5_c_re, v_s5_c_im, v_s5_d, v_s5_w_glu, v_s5_b_glu, v_rg_conv_w, v_rg_conv_b, v_rg_w_a, v_rg_b_a, v_rg_w_x, v_rg_b_x, v_rg_lambda, v_hg_lower_bounds, v_hg_norm_w, v_w_branch, v_w_out, v_final_norm_w)
    return _step(dict(zip(_ARG_NAMES, vals)))
```

```python
import functools
import math

import jax
import jax.numpy as jnp
from jax import lax
from jax.experimental import pallas as pl
from jax.experimental.pallas import tpu as pltpu

F32 = jnp.float32
BF16 = jnp.bfloat16
EPS = 1e-6
RG_C = 8.0
S5_GROUP = 16
S5_STATE = 64
RG_BLOCKS = 16
RG_CONV = 4
HG_HEADS = 8
N_CHIPS = 4
VMEM_LIMIT_BYTES = 56 * 1024 * 1024
NEG_BIG = -1e30

ADAM_LR = 0.001
ADAM_B1 = 0.9
ADAM_B2 = 0.999
ADAM_EPS = 1e-08
ADAM_WD = 0.01
ADAM_STEP = 10

MESH = pl.DeviceIdType.MESH


def _cparams(sem):
    return pltpu.CompilerParams(dimension_semantics=sem, vmem_limit_bytes=VMEM_LIMIT_BYTES)


_DOT_DIMS = {"nn": (((1,), (0,)), ((), ())), "nt": (((1,), (1,)), ((), ())), "tn": (((0,), (0,)), ((), ()))}


def _bdot_raw(a, b, form):
    return lax.dot_general(a.astype(BF16), b.astype(BF16), _DOT_DIMS[form], preferred_element_type=F32)


@functools.partial(jax.custom_vjp, nondiff_argnums=(2,))
def bdot(a, b, form):
    return _bdot_raw(a, b, form)


def _bdot_fwd(a, b, form):
    return _bdot_raw(a, b, form), (a, b)


def _bdot_bwd(form, res, g):
    a, b = res
    if form == "nn":
        da, db = _bdot_raw(g, b, "nt"), _bdot_raw(a, g, "tn")
    elif form == "nt":
        da, db = _bdot_raw(g, b, "nn"), _bdot_raw(g, a, "tn")
    else:
        da, db = _bdot_raw(b, g, "nt"), _bdot_raw(a, g, "nn")
    return da.astype(a.dtype), db.astype(b.dtype)


bdot.defvjp(_bdot_fwd, _bdot_bwd)


@functools.partial(jax.custom_vjp, nondiff_argnums=(1,))
def sroll(x, d):
    return pltpu.roll(x, d, 0)


def _sroll_fwd(x, d):
    return pltpu.roll(x, d, 0), None


def _sroll_bwd(d, _, g):
    return (pltpu.roll(g, g.shape[0] - d, 0),)


sroll.defvjp(_sroll_fwd, _sroll_bwd)


def _row_iota(n):
    return lax.broadcasted_iota(jnp.int32, (n, 1), 0)


def _last_row(x, ti):
    return jnp.sum(jnp.where(ti == x.shape[0] - 1, x, 0.0), axis=0, keepdims=True)


def _silu(x):
    return x * jax.nn.sigmoid(x)


class _Tiling:
    def __init__(self, CB, tT, nT, ncb=None, bmap=None):
        self.CB, self.tT, self.nT = CB, tT, nT
        self.ncb = CB if ncb is None else ncb
        self.bmap = (lambda cb, t: (t, cb)) if bmap is None else bmap


def _tiled_specs(params, views, carry_shapes, out_defs, tl, rev):
    nT = tl.nT
    tmap = (lambda t: nT - 1 - t) if rev else (lambda t: t)

    def rspec(col0, width):
        bw = width // tl.ncb
        off = col0 // bw

        def imap(cb, t):
            rb, cbk = tl.bmap(cb, tmap(t))
            return (rb, off + cbk)

        return pl.BlockSpec((tl.tT, bw), imap)

    p_specs = [pl.BlockSpec((None,) + p.shape[1:], lambda cb, t: (cb, 0, 0)) for p in params]
    r_specs = [rspec(c0, w) for (c0, w) in views]
    dr_specs = [rspec(0, w) for (_, w) in views]
    o_specs = [rspec(0, w) for (w, _) in out_defs]
    s_specs = [pl.BlockSpec((None, r, c), lambda cb, t: (tmap(t), 0, cb)) for (r, c) in carry_shapes]
    return p_specs, r_specs, dr_specs, o_specs, s_specs


def _tiled_fwd(f, name, params, rows, views, carry_shapes, out_defs, tl, side=None):
    T = rows[0].shape[0]
    CB, tT, nT = tl.CB, tl.tT, tl.nT
    n_p, n_r, n_o, n_c = len(params), len(rows), len(out_defs), len(carry_shapes)
    p_specs, r_specs, _, o_specs, s_specs = _tiled_specs(params, views, carry_shapes, out_defs, tl, False)
    x_ins, x_outs, x_sems = _side_parts(side)
    n_xi, n_xo = len(x_ins), len(x_outs)

    def body(*refs):
        i = 0
        p_refs = refs[i:i + n_p]; i += n_p
        r_refs = refs[i:i + n_r]; i += n_r
        xi_refs = refs[i:i + n_xi]; i += n_xi
        o_refs = refs[i:i + n_o]; i += n_o
        s_refs = refs[i:i + n_c]; i += n_c
        xo_refs = refs[i:i + n_xo]; i += n_xo
        c_refs = refs[i:i + n_c]
        cb, t = pl.program_id(0), pl.program_id(1)
        if side is not None:
            @pl.when((cb == 0) & (t == 0))
            def _():
                side.start(xi_refs, xo_refs, refs[-2], refs[-1])

        @pl.when(t == 0)
        def _():
            for c in c_refs:
                c[...] = jnp.zeros_like(c)

        carries = [c[...] for c in c_refs]
        for s, cv in zip(s_refs, carries):
            s[...] = cv
        outs, newc = f([p[...] for p in p_refs], [r[...] for r in r_refs], carries, t * tT)
        for o, v in zip(o_refs, outs):
            o[...] = v.astype(o.dtype)
        for c, v in zip(c_refs, newc):
            c[...] = v
        if side is not None:
            @pl.when((cb == CB - 1) & (t == nT - 1))
            def _():
                side.finish(xi_refs, xo_refs, refs[-2], refs[-1])

    out_shape = [jax.ShapeDtypeStruct((T, w), dt) for (w, dt) in out_defs]
    out_shape += [jax.ShapeDtypeStruct((nT, r, c * CB), F32) for (r, c) in carry_shapes]
    res = pl.pallas_call(
        body, name=name + "_fwd", grid=(CB, nT),
        in_specs=p_specs + r_specs + [_ANY] * n_xi, out_specs=o_specs + s_specs + [_ANY] * n_xo,
        out_shape=out_shape + x_outs,
        scratch_shapes=[pltpu.VMEM((r, c), F32) for (r, c) in carry_shapes] + x_sems,
        compiler_params=_cparams(("arbitrary", "arbitrary")),
    )(*params, *rows, *x_ins)
    return list(res[:n_o]), list(res[n_o:n_o + n_c]), list(res[n_o + n_c:])


def _tiled_bwd(f, name, params, rows, views, saved, douts, carry_shapes, out_defs, tl):
    T = rows[0].shape[0]
    CB, tT, nT = tl.CB, tl.tT, tl.nT
    n_p, n_r, n_o, n_c = len(params), len(rows), len(out_defs), len(carry_shapes)
    p_specs, r_specs, dr_specs, o_specs, s_specs = _tiled_specs(params, views, carry_shapes, out_defs, tl, True)
    out_dtypes = [dt for (_, dt) in out_defs]

    def body(*refs):
        i = 0
        p_refs = refs[i:i + n_p]; i += n_p
        r_refs = refs[i:i + n_r]; i += n_r
        s_refs = refs[i:i + n_c]; i += n_c
        g_refs = refs[i:i + n_o]; i += n_o
        dp_refs = refs[i:i + n_p]; i += n_p
        dr_refs = refs[i:i + n_r]; i += n_r
        dc_refs = refs[i:]
        t = pl.program_id(1)

        @pl.when(t == 0)
        def _():
            for c in dc_refs:
                c[...] = jnp.zeros_like(c)
            for d in dp_refs:
                d[...] = jnp.zeros_like(d)

        t0 = (nT - 1 - t) * tT

        def g(P, R, C):
            outs, newc = f(P, R, C, t0)
            return [o.astype(dt) for o, dt in zip(outs, out_dtypes)], list(newc)

        _, vjp = jax.vjp(g, [p[...] for p in p_refs], [r[...] for r in r_refs], [s[...] for s in s_refs])
        dP, dR, dC = vjp(([gr[...] for gr in g_refs], [c[...] for c in dc_refs]))
        for d, v in zip(dp_refs, dP):
            d[...] += v
        for d, v in zip(dr_refs, dR):
            d[...] = v.astype(d.dtype)
        for c, v in zip(dc_refs, dC):
            c[...] = v

    out_shape = [jax.ShapeDtypeStruct(p.shape, F32) for p in params]
    out_shape += [jax.ShapeDtypeStruct((T, w), r.dtype) for r, (_, w) in zip(rows, views)]
    res = pl.pallas_call(
        body, name=name + "_bwd", grid=(CB, nT),
        in_specs=p_specs + r_specs + s_specs + o_specs, out_specs=p_specs + dr_specs, out_shape=out_shape,
        scratch_shapes=[pltpu.VMEM((r, c), F32) for (r, c) in carry_shapes],
        compiler_params=_cparams(("arbitrary", "arbitrary")),
    )(*params, *rows, *saved, *douts)
    return list(res[:n_p]), list(res[n_p:])


def tiled_op(f, name, params, rows, carry_shapes, out_defs, CB, tT, tiling=None, side_of=None):
    arrs = [r[0] if isinstance(r, tuple) else r for r in rows]
    views = [(r[1], r[2]) if isinstance(r, tuple) else (0, r.shape[1]) for r in rows]
    tl = tiling if tiling is not None else _Tiling(CB, tT, arrs[0].shape[0] // tT)
    make_side, side_arrs = side_of if side_of is not None else (None, [])

    def run_fwd(params, arrs, side_arrs):
        side = make_side(*side_arrs) if make_side is not None else None
        return _tiled_fwd(f, name, params, arrs, views, carry_shapes, out_defs, tl, side)

    @jax.custom_vjp
    def op(params, arrs, side_arrs):
        outs, _, extra = run_fwd(params, arrs, side_arrs)
        return outs, extra

    def op_fwd(params, arrs, side_arrs):
        outs, saved, extra = run_fwd(params, arrs, side_arrs)
        return (outs, extra), (params, arrs, saved, side_arrs)

    def op_bwd(res, cts):
        params, arrs, saved, side_arrs = res
        douts, _ = cts
        dP, dR = _tiled_bwd(f, name, params, arrs, views, saved, list(douts), carry_shapes, out_defs, tl)
        dR = [d if w == a.shape[1] else jnp.pad(d, ((0, 0), (c0, a.shape[1] - c0 - w)))
              for d, a, (c0, w) in zip(dR, arrs, views)]
        return dP, dR, [jnp.zeros_like(s) for s in side_arrs]

    op.defvjp(op_fwd, op_bwd)
    outs, extra = op(list(params), arrs, list(side_arrs))
    return outs if side_of is None else (outs, extra)


def _pick(n, pref):
    for t in pref:
        if n % t == 0:
            return t
    return n


class _Side:
    def __init__(self, ins, outs, n_sems, start, finish):
        self.ins, self.outs, self.n_sems, self.start, self.finish = ins, outs, n_sems, start, finish


def _side_parts(side):
    if side is None:
        return [], [], []
    sems = [pltpu.SemaphoreType.DMA((side.n_sems,)), pltpu.SemaphoreType.DMA((side.n_sems,))]
    return list(side.ins), list(side.outs), sems


def _mm_nn(a, w, res, name, out_dtype=F32, side=None):
    M, K = a.shape
    S, _, Ns = w.shape
    tm = _pick(M, (1024, 512, 256, 128))
    tn = _pick(Ns, (512, 256, 128))
    nps = Ns // tn
    has_res = res is not None
    n_main = 3 if has_res else 2
    s_ins, s_outs, s_sems = _side_parts(side)
    ni, nj = M // tm, S * nps

    def body(*refs):
        a_ref, w_ref = refs[0], refs[1]
        o_ref = refs[n_main + len(s_ins)]
        if side is not None:
            si = refs[n_main:n_main + len(s_ins)]
            so = refs[n_main + len(s_ins) + 1:n_main + len(s_ins) + 1 + len(s_outs)]
            send, recv = refs[-2], refs[-1]
            i, j = pl.program_id(0), pl.program_id(1)

            @pl.when((i == 0) & (j == 0))
            def _():
                side.start(si, so, send, recv)

        acc = _bdot_raw(a_ref[...], w_ref[...], "nn")
        if has_res:
            acc = acc + refs[2][...]
        o_ref[...] = acc.astype(o_ref.dtype)
        if side is not None:
            @pl.when((i == ni - 1) & (j == nj - 1))
            def _():
                side.finish(si, so, send, recv)

    in_specs = [pl.BlockSpec((tm, K), lambda i, j: (i, 0)),
                pl.BlockSpec((None, K, tn), lambda i, j: (j // nps, 0, j % nps))]
    args = [a, w]
    if has_res:
        in_specs.append(pl.BlockSpec((tm, tn), lambda i, j: (i, j)))
        args.append(res)
    out = pl.pallas_call(
        body, name=name, grid=(ni, nj), in_specs=in_specs + [_ANY] * len(s_ins),
        out_specs=[pl.BlockSpec((tm, tn), lambda i, j: (i, j))] + [_ANY] * len(s_outs),
        out_shape=[jax.ShapeDtypeStruct((M, S * Ns), out_dtype)] + s_outs,
        scratch_shapes=s_sems,
        compiler_params=_cparams(("arbitrary", "arbitrary")),
    )(*args, *s_ins)
    return out[0] if side is None else out


def _mm_nt(g, w, out_dtype, name, side=None):
    M, N = g.shape
    S, K, Ns = w.shape
    tm = _pick(M, (1024, 512, 256, 128))
    tk = _pick(K, (1024, 512, 256, 128))
    tn = _pick(Ns, (1792, 1024, 512, 256, 128))
    nps = Ns // tn
    nn = S * nps
    ni, nk = M // tm, K // tk
    s_ins, s_outs, s_sems = _side_parts(side)

    def body(*refs):
        g_ref, w_ref = refs[0], refs[1]
        o_ref = refs[2 + len(s_ins)]
        acc_ref = refs[3 + len(s_ins) + len(s_outs)]
        i, k, n = pl.program_id(0), pl.program_id(1), pl.program_id(2)
        if side is not None:
            si = refs[2:2 + len(s_ins)]
            so = refs[3 + len(s_ins):3 + len(s_ins) + len(s_outs)]

            @pl.when((i == 0) & (k == 0) & (n == 0))
            def _():
                side.start(si, so, refs[-2], refs[-1])

        @pl.when(n == 0)
        def _():
            acc_ref[...] = jnp.zeros_like(acc_ref)

        acc_ref[...] += _bdot_raw(g_ref[...], w_ref[...], "nt")

        @pl.when(n == nn - 1)
        def _():
            o_ref[...] = acc_ref[...].astype(o_ref.dtype)

        if side is not None:
            @pl.when((i == ni - 1) & (k == nk - 1) & (n == nn - 1))
            def _():
                side.finish(si, so, refs[-2], refs[-1])

    out = pl.pallas_call(
        body, name=name, grid=(ni, nk, nn),
        in_specs=[pl.BlockSpec((tm, tn), lambda i, k, n: (i, n)),
                  pl.BlockSpec((None, tk, tn), lambda i, k, n: (n // nps, k, n % nps))] + [_ANY] * len(s_ins),
        out_specs=[pl.BlockSpec((tm, tk), lambda i, k, n: (i, k))] + [_ANY] * len(s_outs),
        out_shape=[jax.ShapeDtypeStruct((M, K), out_dtype)] + s_outs,
        scratch_shapes=[pltpu.VMEM((tm, tk), F32)] + s_sems,
        compiler_params=_cparams(("arbitrary", "arbitrary", "arbitrary")),
    )(g, w, *s_ins)
    return out[0] if side is None else out


def _mm_tn(a, g, S, out_dtype, name, side=None):
    T, K = a.shape
    N = g.shape[1]
    Ns = N // S
    tk = _pick(K, (2048, 1024, 512, 256, 128))
    tn = _pick(Ns, (1024, 896, 512, 256, 128))
    tt = _pick(T, (1024, 512, 256, 128))
    nps = Ns // tn
    nt = T // tt
    nk, nj = K // tk, S * nps
    a_t = a.astype(BF16).T
    s_ins, s_outs, s_sems = _side_parts(side)

    def body(*refs):
        a_ref, g_ref = refs[0], refs[1]
        o_ref = refs[2 + len(s_ins)]
        acc_ref = refs[3 + len(s_ins) + len(s_outs)]
        k, j, t = pl.program_id(0), pl.program_id(1), pl.program_id(2)
        if side is not None:
            si = refs[2:2 + len(s_ins)]
            so = refs[3 + len(s_ins):3 + len(s_ins) + len(s_outs)]
            send, recv = refs[-2], refs[-1]

            @pl.when((k == 0) & (j == 0) & (t == 0))
            def _():
                side.start(si, so, send, recv)

        @pl.when(t == 0)
        def _():
            acc_ref[...] = jnp.zeros_like(acc_ref)

        acc_ref[...] += _bdot_raw(a_ref[...], g_ref[...], "nn")

        @pl.when(t == nt - 1)
        def _():
            o_ref[...] = acc_ref[...].astype(o_ref.dtype)

        if side is not None:
            @pl.when((k == nk - 1) & (j == nj - 1) & (t == nt - 1))
            def _():
                side.finish(si, so, send, recv)

    out = pl.pallas_call(
        body, name=name, grid=(nk, nj, nt),
        in_specs=[pl.BlockSpec((tk, tt), lambda k, j, t: (k, t)),
                  pl.BlockSpec((tt, tn), lambda k, j, t: (t, j))] + [_ANY] * len(s_ins),
        out_specs=[pl.BlockSpec((None, tk, tn), lambda k, j, t: (j // nps, k, j % nps))] + [_ANY] * len(s_outs),
        out_shape=[jax.ShapeDtypeStruct((S, K, Ns), out_dtype)] + s_outs,
        scratch_shapes=[pltpu.VMEM((tk, tn), F32)] + s_sems,
        compiler_params=_cparams(("arbitrary", "arbitrary", "arbitrary")),
    )(a_t, g, *s_ins)
    return out[0] if side is None else out


def linear(a, w, name, res=None):
    @jax.custom_vjp
    def op(a, w, res):
        return _mm_nn(a, w, res, name + "_fwd")

    def op_fwd(a, w, res):
        return _mm_nn(a, w, res, name + "_fwd"), (a, w)

    def op_bwd(saved, g):
        a, w = saved
        da = _mm_nt(g, w, a.dtype, name + "_bwd_a")
        dw = _mm_tn(a, g, w.shape[0], w.dtype, name + "_bwd_w")
        return da, dw, (None if res is None else g)

    op.defvjp(op_fwd, op_bwd)
    return op(a, w, res)


def _rms_tile(params, rows, carries, t0):
    (w,), (x,) = params, rows
    y = x * lax.rsqrt(jnp.mean(x * x, axis=-1, keepdims=True) + EPS) * w
    return [y], []


def _s5_tile(params, rows, carries, t0):
    t_m, s_re, s_im, r_re, r_im, apow_re, apow_im, d = params
    (u,) = rows
    n = u.shape[0]
    ti = _row_iota(n)
    pi = _row_iota(apow_re.shape[0])
    x_re, x_im = bdot(u, s_re, "nn"), bdot(u, s_im, "nn")
    k = 0
    while (1 << k) < n:
        sh = 1 << k
        p_re, p_im = _row_sel(apow_re, pi, k), _row_sel(apow_im, pi, k)
        q_re, q_im = sroll(x_re, sh), sroll(x_im, sh)
        m = ti >= sh
        x_re, x_im = (x_re + jnp.where(m, p_re * q_re - p_im * q_im, 0.0),
                      x_im + jnp.where(m, p_re * q_im + p_im * q_re, 0.0))
        k += 1
    x_re = jnp.where(ti >= 1, sroll(x_re, 1), 0.0)
    x_im = jnp.where(ti >= 1, sroll(x_im, 1), 0.0)
    y = bdot(u, t_m, "nn") + bdot(x_re, r_re, "nn") + bdot(x_im, r_im, "nn") + d * u
    return [jax.nn.gelu(y)], []


def _glu_tile(params, rows, carries, t0):
    (b,), (y, zg, ga) = params, rows
    return [y * jax.nn.sigmoid(zg + b) * _silu(ga)], []


def _neg_expm1(z):
    small = -(z * (1.0 + z * (0.5 + z * (1.0 / 6.0))))
    return jnp.where(z > -0.01, small, 1.0 - jnp.exp(z))


def _rg_tile(params, rows, carries, t0):
    conv_w, conv_b, w_a, b_a, w_x, b_x, sp = params
    x, gate = rows
    x_prev, h_prev = carries
    tT = x.shape[0]
    ti = _row_iota(tT)
    ci = _row_iota(RG_CONV)
    xc = _row_sel(conv_w, ci, RG_CONV - 1) * x + conv_b
    for k in range(1, RG_CONV):
        xs = jnp.where(ti >= k, sroll(x, k), sroll(x_prev, k))
        xc = xc + _row_sel(conv_w, ci, RG_CONV - 1 - k) * xs
    r = jax.nn.sigmoid(bdot(xc, w_a, "nn") + b_a)
    i = jax.nn.sigmoid(bdot(xc, w_x, "nn") + b_x)
    log_a = -RG_C * r * sp
    a = jnp.exp(log_a)
    mult = jnp.sqrt(_neg_expm1(2.0 * log_a))
    mult = jnp.where(ti + t0 == 0, 1.0, mult)
    b = mult * (i * xc)
    b = b + jnp.where(ti == 0, a * h_prev, 0.0)
    k = 1
    while k < tT:
        m = ti >= k
        b = b + jnp.where(m, a * sroll(b, k), 0.0)
        a = jnp.where(m, a * sroll(a, k), a)
        k *= 2
    return [b * _silu(gate)], [x, _last_row(b, ti)]


def _hg_tile(params, rows, carries, t0):
    lb, nw = params
    q, fl, v, gate = rows
    (st,) = carries
    tT = q.shape[0]
    ti = _row_iota(tT)
    qs = _silu(q)
    f = lb + (1.0 - lb) * jax.nn.sigmoid(fl)
    kk = 1.0 - f
    G = jnp.log(f)
    k = 1
    while k < tT:
        G = G + jnp.where(ti >= k, sroll(G, k), 0.0)
        k *= 2
    inter = bdot(qs * jnp.exp(G), st, "nt")
    tr = lax.broadcasted_iota(jnp.int32, (tT, tT), 0)
    sc = lax.broadcasted_iota(jnp.int32, (tT, tT), 1)
    attn = jnp.zeros((tT, tT), F32)
    blk = tT
    while blk > 8:
        sub = blk // 4
        for j in range(1, 4):
            ref = jnp.zeros_like(G)
            for b in range(tT // blk):
                row = _row_sel(G, ti, b * blk + sub * j - 1)
                ref = ref + jnp.where(_div2(ti, blk) == b, row, 0.0)
            tmask = _div2(_mod2(ti, blk), sub) == j
            smask = _mod2(ti, blk) < sub * j
            qt = qs * jnp.exp(jnp.where(tmask, G - ref, NEG_BIG))
            kt = kk * jnp.exp(jnp.where(smask, ref - G, NEG_BIG))
            aj = bdot(qt, kt, "nt")
            attn = attn + jnp.where(_div2(tr, blk) == _div2(sc, blk), aj, 0.0)
        blk = sub
    intra = bdot(attn, v, "nn")
    for d in range(blk):
        if d == 0:
            kd, gd, vd = kk, G, v
        else:
            kd, gd, vd = sroll(kk, d), sroll(G, d), sroll(v, d)
        m = _mod2(ti, blk) >= d
        w = jnp.sum(qs * kd * jnp.exp(jnp.where(m, G - gd, NEG_BIG)), axis=1, keepdims=True)
        intra = intra + w * vd
    o = inter + intra
    g_last = _last_row(G, ti)
    k_dec = kk * jnp.exp(g_last - G)
    st_new = st * jnp.exp(g_last) + bdot(v, k_dec, "tn")
    o = o * lax.rsqrt(jnp.mean(o * o, axis=-1, keepdims=True) + EPS) * nw
    return [o * _silu(gate)], [st_new]


def _row_sel(x, ti, r):
    return jnp.sum(jnp.where(ti == r, x, 0.0), axis=0, keepdims=True)


def _div2(i, p):
    return lax.shift_right_logical(i, jnp.int32(p.bit_length() - 1))


def _mod2(i, p):
    return lax.bitwise_and(i, jnp.int32(p - 1))


def _merge_tile(params, rows, carries, t0):
    b0, b1, b2, g0, g1, g2 = rows
    m = jax.nn.sigmoid(g0) * b0 + jax.nn.sigmoid(g1) * b1 + jax.nn.sigmoid(g2) * b2
    return [m], []


def _loss_tile(params, rows, carries, t0):
    (w,), (x, tgt) = params, rows
    y = x * lax.rsqrt(jnp.mean(x * x, axis=-1, keepdims=True) + EPS) * w
    e = y - tgt
    return [0.5 * jnp.mean(e * e, axis=-1, keepdims=True)], []


def _block_diag(w, cb):
    n, i, j = w.shape
    g = n // cb
    w4 = w.reshape(cb, g, i, j)
    eye = jnp.eye(g, dtype=w.dtype)
    return jnp.einsum("cgij,gk->cgikj", w4, eye).reshape(cb, g * i, g * j)


def _s5_params(lam_re, lam_im, log_step, b_re, b_im, c_re, c_im, d, levels):
    G, P = lam_re.shape
    step = jnp.exp(log_step)[:, None]
    mag = jnp.exp(lam_re * step)
    ang = lam_im * step
    abar_re = mag * jnp.cos(ang)
    abar_im = mag * jnp.sin(ang)
    num_re = abar_re - 1.0
    num_im = abar_im
    den = lam_re * lam_re + lam_im * lam_im
    coef_re = (num_re * lam_re + num_im * lam_im) / den
    coef_im = (num_im * lam_re - num_re * lam_im) / den
    bbar_re = coef_re[..., None] * b_re - coef_im[..., None] * b_im
    bbar_im = coef_re[..., None] * b_im + coef_im[..., None] * b_re
    H = b_re.shape[2]
    Lc = S5_LC
    hi = lax.Precision.HIGHEST

    def powers(ks):
        ks = jnp.asarray(ks, F32)[:, None, None]
        m = jnp.exp(ks * (lam_re * step))
        return m * jnp.cos(ks * ang), m * jnp.sin(ks * ang)

    pw_re, pw_im = powers(list(range(Lc + 1)))
    ab_re = pw_re[..., None] * bbar_re - pw_im[..., None] * bbar_im
    ab_im = pw_re[..., None] * bbar_im + pw_im[..., None] * bbar_re
    kern = (jnp.einsum("gap,kgph->gkha", c_re, ab_re[:Lc], precision=hi)
            - jnp.einsum("gap,kgph->gkha", c_im, ab_im[:Lc], precision=hi))
    kk = jnp.arange(Lc)[:, None, None]
    jj = jnp.arange(Lc)[None, :, None]
    ii = jnp.arange(Lc)[None, None, :]
    place = (ii - jj == kk).astype(F32)
    t_m = jnp.einsum("gkha,kji->gjhia", kern, place, precision=hi).reshape(G, Lc * H, Lc * H)
    s_re = ab_re[:Lc][::-1].transpose(1, 0, 3, 2).reshape(G, Lc * H, P)
    s_im = ab_im[:Lc][::-1].transpose(1, 0, 3, 2).reshape(G, Lc * H, P)
    m_re = c_re[None] * pw_re[1:, :, None, :] - c_im[None] * pw_im[1:, :, None, :]
    m_im = c_re[None] * pw_im[1:, :, None, :] + c_im[None] * pw_re[1:, :, None, :]
    r_re = m_re.transpose(1, 3, 0, 2).reshape(G, P, Lc * H)
    r_im = -m_im.transpose(1, 3, 0, 2).reshape(G, P, Lc * H)
    rows = max(8, levels)
    ap_re, ap_im = powers([Lc * (1 << k) for k in range(levels)] + [0] * (rows - levels))
    dd = jnp.tile(d.reshape(G, 1, H), (1, 1, Lc))
    return [t_m, s_re, s_im, r_re, r_im, ap_re.transpose(1, 0, 2), ap_im.transpose(1, 0, 2), dd]


_LANE = 128


def _lane_perm_matrix(lc, h):
    n = lc * _LANE
    src = jnp.arange(n).reshape(lc, _LANE // h, h).transpose(1, 0, 2).reshape(n)
    return (jnp.arange(n)[:, None] == src[None, :]).astype(BF16)


def _lane_perm(x, p, p_t, name):
    @jax.custom_vjp
    def op(x):
        return _mm_nn(x, p[None], None, name, BF16)

    def op_fwd(x):
        return _mm_nn(x, p[None], None, name, BF16), None

    def op_bwd(_, g):
        return (_mm_nn(g, p_t[None], None, name + "_t", x.dtype),)

    op.defvjp(op_fwd, op_bwd)
    return op(x)


def _to_chunks(u, lc, h, perm):
    t, w = u.shape
    x = u.reshape(t // lc, lc, w // _LANE, _LANE).transpose(2, 0, 1, 3).reshape((w // _LANE) * (t // lc), lc * _LANE)
    return _lane_perm(x, perm, perm.T, "s5_to_chunks")


def _from_chunks(y, lc, w, perm):
    tiles = w // _LANE
    n_chunks = y.shape[0] // tiles
    x = _lane_perm(y, perm.T, perm, "s5_from_chunks")
    return x.reshape(tiles, n_chunks, lc, _LANE).transpose(1, 2, 0, 3).reshape(n_chunks * lc, w)


def _vec(v, cb):
    return v.reshape(cb, 1, -1)


S5_LC = 16
RG_CB, RG_TT = 4, 256
HG_TT = 128
ROW_TT = 256
MERGE_TT = 128


_LAYER_SMALL = ['s5_lambda_re', 's5_lambda_im', 's5_log_step', 's5_b_re', 's5_b_im', 's5_c_re', 's5_c_im', 's5_d',
                's5_b_glu', 'rg_conv_w', 'rg_conv_b', 'rg_w_a', 'rg_b_a', 'rg_w_x', 'rg_b_x', 'rg_lambda', 'hg_norm_w']


def _lower_bounds(hg_lower_bounds):
    lb_sm = jax.nn.softmax(hg_lower_bounds, axis=0)
    return jnp.cumsum(lb_sm, axis=0) - lb_sm[0]


def _rms_bf16(x, w):
    T, D = x.shape
    return tiled_op(_rms_tile, "rms", [w.reshape(1, 1, D)], [x], [], [(D, BF16)], 1, min(ROW_TT, T))[0]


def _loss_rows(x, w, target):
    T, D = x.shape
    return tiled_op(_loss_tile, "loss", [w.reshape(1, 1, D)], [x, target], [], [(1, F32)], 1, min(ROW_TT, T))[0]


def _layer_tail(z, x, lw, rest, nxt):
    T, D = x.shape
    lw = dict(lw, **_unpack_rest(rest, lw["s5_d"].shape[0], D))
    W = lw["s5_d"].shape[0]
    row_tt = min(ROW_TT, T)
    rg_tt, hg_tt = min(RG_TT, T), min(HG_TT, T)
    n_chunks = T // S5_LC
    gpt = _LANE // S5_GROUP
    perm = _lane_perm_matrix(S5_LC, S5_GROUP)
    s5_tiling = _Tiling(W // S5_GROUP, n_chunks, 1, ncb=gpt, bmap=lambda cb, t: (cb // gpt, cb % gpt))
    g_a, x_b, g_b, q_c, f_c, i_c, g_c = [(z, k * W, W) for k in range(1, 8)]
    gl = [(z, 8 * W + n * D, D) for n in range(3)]
    s5p = _s5_params(lw["s5_lambda_re"], lw["s5_lambda_im"], lw["s5_log_step"], lw["s5_b_re"], lw["s5_b_im"],
                     lw["s5_c_re"], lw["s5_c_im"], lw["s5_d"], int(math.log2(n_chunks)))
    (y1c,) = tiled_op(_s5_tile, "s5", s5p, [_to_chunks(z[:, :W], S5_LC, S5_GROUP, perm)], [],
                      [(S5_LC * _LANE, BF16)], W // S5_GROUP, n_chunks, tiling=s5_tiling)
    y1 = _from_chunks(y1c, S5_LC, W, perm)
    zg = linear(y1, lw["s5_w_glu"].reshape(1, W, W), "w_glu")
    (y_a,) = tiled_op(_glu_tile, "glu", [lw["s5_b_glu"].reshape(1, 1, W)], [y1, zg, g_a], [], [(W, BF16)], 1, row_tt)
    rgp = [lw["rg_conv_w"].reshape(RG_CONV, RG_CB, W // RG_CB).transpose(1, 0, 2),
           _vec(lw["rg_conv_b"], RG_CB), _block_diag(lw["rg_w_a"], RG_CB), _vec(lw["rg_b_a"], RG_CB),
           _block_diag(lw["rg_w_x"], RG_CB), _vec(lw["rg_b_x"], RG_CB), _vec(jax.nn.softplus(-lw["rg_lambda"]), RG_CB)]
    rc = W // RG_CB
    (y_b,) = tiled_op(_rg_tile, "rg", rgp, [x_b, g_b], [(rg_tt, rc), (1, rc)], [(W, BF16)], RG_CB, rg_tt)
    dk = W // HG_HEADS
    hgp = [_vec(lw["lbs"], HG_HEADS), _vec(lw["hg_norm_w"], HG_HEADS)]
    land = None
    if nxt is None:
        (y_c,) = tiled_op(_hg_tile, "hg", hgp, [q_c, f_c, i_c, g_c], [(dk, dk)], [(W, BF16)], HG_HEADS, hg_tt)
    else:
        (y_c,), (land,) = tiled_op(_hg_tile, "hg_gather", hgp, [q_c, f_c, i_c, g_c], [(dk, dk)], [(W, BF16)],
                                   HG_HEADS, hg_tt, side_of=(_gather_side, [nxt]))
    br = [linear(y, lw["w_branch"][n], "w_br") for n, y in enumerate((y_a, y_b, y_c))]
    (mg,) = tiled_op(_merge_tile, "merge", [], br + gl, [], [(D, BF16)], 1, min(MERGE_TT, T))
    return linear(mg, lw["w_out"].reshape(1, D, D), "w_out", res=x), land


def _pack_rest(w_glu, w_branch, w_out):
    c = w_branch.shape[-1]
    parts = [w_branch[n].reshape(2, -1, c) for n in range(3)] + [w_out.reshape(2, -1, c), w_glu.reshape(2, -1, c)]
    return jnp.concatenate(parts, 1)


def _rest_rows(W, D):
    c = D // N_CHIPS
    return [W // 2] * 3 + [(D // N_CHIPS) * D // (2 * c), (W // N_CHIPS) * W // (2 * c)]


def _unpack_rest(p, W, D):
    lead = p.shape[:-3]
    out, off = [], 0
    for r in _rest_rows(W, D):
        out.append(p[..., off:off + r, :])
        off += r
    br = [b.reshape(lead + (W, D // N_CHIPS)) for b in out[:3]]
    w_out = out[3].reshape(lead + (D // N_CHIPS, D))
    w_glu = out[4].reshape(lead + (W // N_CHIPS, W))
    return {"w_branch": br, "w_out": w_out, "s5_w_glu": w_glu}


_ANY = pl.BlockSpec(memory_space=pl.ANY)


def _place():
    x, y, c = lax.axis_index("x"), lax.axis_index("y"), lax.axis_index("c")
    chips = [(1 - x, y), (x, 1 - y), (1 - x, 1 - y)]
    return x, y, c, chips


def _rcopy(src, dst, send_sems, recv_sems, k, to):
    return pltpu.make_async_remote_copy(src_ref=src, dst_ref=dst, send_sem=send_sems.at[k], recv_sem=recv_sems.at[k],
                                        device_id=to, device_id_type=MESH)


def _gather_side(w):
    shape = (N_CHIPS,) + w.shape

    def first(src, land, send, recv):
        x, y, c, chips = _place()
        me = 2 * x + y
        return [_rcopy(src.at[c], land.at[me, c], send, recv, j, (*chip, c)) for j, chip in enumerate(chips)]

    def start(ins, outs, send, recv):
        for cp in first(ins[0], outs[0], send, recv):
            cp.start()

    def finish(ins, outs, send, recv):
        land = outs[0]
        x, y, c, chips = _place()
        sibling = (x, y, 1 - c)
        passed = []
        for j, chip in enumerate(chips):
            s = 2 * chip[0] + chip[1]
            _rcopy(land.at[s, c], land.at[s, c], send, recv, j, (*chip, c)).wait_recv()
            cp = _rcopy(land.at[s, c], land.at[s, c], send, recv, 3 + j, sibling)
            cp.start()
            passed.append(cp)
        for j, chip in enumerate(chips):
            s = 2 * chip[0] + chip[1]
            _rcopy(land.at[s, 1 - c], land.at[s, 1 - c], send, recv, 3 + j, sibling).wait_recv()
        for cp in first(ins[0], land, send, recv) + passed:
            cp.wait_send()

    return _Side([w], [jax.ShapeDtypeStruct(shape, w.dtype)], 6, start, finish)


def _place_own(land, w):
    chip = 2 * lax.axis_index("x") + lax.axis_index("y")
    full = lax.dynamic_update_slice(land, w[None], (chip, 0, 0, 0))
    return full.reshape(N_CHIPS, 2 * w.shape[1], w.shape[2])


def gather_rows(w, name):
    side = _gather_side(w)

    def body(w_ref, o_ref, send, recv):
        side.start([w_ref], [o_ref], send, recv)
        side.finish([w_ref], [o_ref], send, recv)

    return pl.pallas_call(
        body, name=name, in_specs=[_ANY], out_specs=_ANY, out_shape=side.outs[0],
        scratch_shapes=[pltpu.SemaphoreType.DMA((6,)), pltpu.SemaphoreType.DMA((6,))],
    )(w)


def _scatter_side(p):
    def copies(src, dst, send, recv):
        x, y, c, chips = _place()
        return [_rcopy(src.at[2 * chip[0] + chip[1]], dst.at[j], send, recv, j, (*chip, c))
                for j, chip in enumerate(chips)]

    def start(ins, outs, send, recv):
        for cp in copies(ins[0], outs[0], send, recv):
            cp.start()

    def finish(ins, outs, send, recv):
        for cp in copies(ins[0], outs[0], send, recv):
            cp.wait()

    return _Side([p], [jax.ShapeDtypeStruct((3,) + p.shape[1:], p.dtype)], 3, start, finish)


def scatter_rows(p, name):
    side = _scatter_side(p)

    def body(p_ref, o_ref, send, recv):
        side.start([p_ref], [o_ref], send, recv)
        side.finish([p_ref], [o_ref], send, recv)

    return pl.pallas_call(
        body, name=name, in_specs=[_ANY], out_specs=_ANY, out_shape=side.outs[0],
        scratch_shapes=[pltpu.SemaphoreType.DMA((3,)), pltpu.SemaphoreType.DMA((3,))],
    )(p)


def _swap_rows(g, name):
    def body(g_ref, o_ref, send_sems, recv_sems):
        x, y, c, _ = _place()
        cp = _rcopy(g_ref.at[:, 1 - c], o_ref, send_sems, recv_sems, 0, (x, y, 1 - c))
        cp.start()
        cp.wait()

    return pl.pallas_call(
        body, name=name, in_specs=[_ANY], out_specs=_ANY,
        out_shape=jax.ShapeDtypeStruct((g.shape[0],) + g.shape[2:], g.dtype),
        scratch_shapes=[pltpu.SemaphoreType.DMA((1,)), pltpu.SemaphoreType.DMA((1,))],
    )(g)


def _swap_whole(g, name):
    def body(g_ref, o_ref, send_sems, recv_sems):
        x, y, c, _ = _place()
        cp = _rcopy(g_ref, o_ref, send_sems, recv_sems, 0, (x, y, 1 - c))
        cp.start()
        cp.wait()

    return pl.pallas_call(
        body, name=name, in_specs=[_ANY], out_specs=_ANY, out_shape=jax.ShapeDtypeStruct(g.shape, g.dtype),
        scratch_shapes=[pltpu.SemaphoreType.DMA((1,)), pltpu.SemaphoreType.DMA((1,))],
    )(g)


def _ew_call(fn, ins, out_dtypes, name):
    shape = ins[0].shape
    n = shape[-1]
    ins2 = [a.reshape(-1, n) for a in ins]
    rows = ins2[0].shape[0]
    tr = _pick(rows, (256, 128, 64, 32, 16, 8)) if n <= 2048 else _pick(rows, (128, 64, 32, 16, 8))
    n_in = len(ins2)

    def body(*refs):
        outs = fn(*[r[...] for r in refs[:n_in]])
        for o, v in zip(refs[n_in:], outs):
            o[...] = v.astype(o.dtype)

    spec = pl.BlockSpec((tr, n), lambda i: (i, 0))
    res = pl.pallas_call(
        body, name=name, grid=(rows // tr,), in_specs=[spec] * n_in, out_specs=[spec] * len(out_dtypes),
        out_shape=[jax.ShapeDtypeStruct((rows, n), dt) for dt in out_dtypes],
        compiler_params=_cparams(("parallel",)),
    )(*ins2)
    return [r.reshape(shape) for r in res]


def all_reduce_small(buf, name):
    _, R, Ln = buf.shape

    def body(in_ref, out_ref, recv_ref, send_a, recv_a, send_b, recv_b):
        x, y, c = lax.axis_index("x"), lax.axis_index("y"), lax.axis_index("c")
        me = 4 * x + 2 * y + c
        peers = []
        for r in range(1, 8):
            px, py, pc = x ^ ((r >> 2) & 1), y ^ ((r >> 1) & 1), c ^ (r & 1)
            peers.append((r, (px, py, pc), 4 * px + 2 * py + pc))
        cps = [_rcopy(in_ref.at[idx], recv_ref.at[r], send_a, recv_a, r, to) for r, to, idx in peers]
        for cp in cps:
            cp.start()
        for cp in cps:
            cp.wait()
        acc = in_ref[me]
        for r in range(1, 8):
            acc = acc + recv_ref[r]
        out_ref[me] = acc
        cps = [_rcopy(out_ref.at[me], out_ref.at[me], send_b, recv_b, r, to) for r, to, idx in peers]
        for cp in cps:
            cp.start()
        for (r, to, idx), cp in zip(peers, cps):
            cp.wait_send()
            _rcopy(out_ref.at[idx], out_ref.at[idx], send_b, recv_b, r, to).wait_recv()

    vm = pl.BlockSpec(memory_space=pltpu.VMEM)
    return pl.pallas_call(
        body, name=name, in_specs=[vm], out_specs=vm,
        out_shape=jax.ShapeDtypeStruct(buf.shape, F32),
        scratch_shapes=[pltpu.VMEM(buf.shape, F32)] + [pltpu.SemaphoreType.DMA((8,))] * 4,
        compiler_params=pltpu.CompilerParams(vmem_limit_bytes=VMEM_LIMIT_BYTES),
    )(buf)


def _adamw_math(w, g, m, v):
    m = ADAM_B1 * m + (1.0 - ADAM_B1) * g
    v = ADAM_B2 * v + (1.0 - ADAM_B2) * (g * g)
    m_hat = m / (1.0 - ADAM_B1 ** ADAM_STEP)
    v_hat = v / (1.0 - ADAM_B2 ** ADAM_STEP)
    delta = -ADAM_LR * (m_hat / (jnp.sqrt(v_hat) + ADAM_EPS) + ADAM_WD * w)
    return [delta, m, v]


def adamw(w, g, m, v, name):
    return _ew_call(_adamw_math, [w, g, m, v], [F32, F32, F32], name)


_WEIGHTS = ['norm_w', 'w_in', 's5_lambda_re', 's5_lambda_im', 's5_log_step', 's5_b_re', 's5_b_im', 's5_c_re', 's5_c_im',
            's5_d', 's5_w_glu', 's5_b_glu', 'rg_conv_w', 'rg_conv_b', 'rg_w_a', 'rg_b_a', 'rg_w_x', 'rg_b_x', 'rg_lambda',
            'hg_lower_bounds', 'hg_norm_w', 'w_branch', 'w_out', 'final_norm_w']
_BIG = ('w_in', 's5_w_glu', 'w_branch', 'w_out')
_SMALL = [n for n in _WEIGHTS if n not in _BIG]
_LANES = 128
_N_DEV = 8


def _pack(arrs):
    flat = jnp.concatenate([a.reshape(-1) for a in arrs])
    unit = _N_DEV * 8 * _LANES
    total = -(-flat.shape[0] // unit) * unit
    flat = jnp.pad(flat, (0, total - flat.shape[0]))
    return flat.reshape(_N_DEV, total // (_N_DEV * _LANES), _LANES)


def _unpack(buf, shapes):
    flat = buf.reshape(-1)
    out, off = [], 0
    for s in shapes:
        n = math.prod(s)
        out.append(flat[off:off + n].reshape(s))
        off += n
    return out


def _step(a):
    x_idx, y_idx, c_idx = lax.axis_index("x"), lax.axis_index("y"), lax.axis_index("c")
    chip = 2 * x_idx + y_idx
    L = a["norm_w"].shape[0]
    W = a["s5_d"].shape[1]
    cw = a["rg_conv_w"]
    wc = cw.shape[2]
    placed = lax.dynamic_update_slice(jnp.zeros((L, RG_CONV, W), F32), cw, (0, 0, chip * wc))
    placed = placed * (c_idx == 0).astype(F32)
    conv_full = _unpack(all_reduce_small(_pack([placed]), "gather_conv"), [(L, RG_CONV, W)])[0]
    wts = {n: a[n] for n in _SMALL}
    wts["rg_conv_w"] = conv_full
    lbs, vjp_lbs = jax.vjp(_lower_bounds, a["hg_lower_bounds"])
    D = a["w_in"].shape[1]
    w_in = a["w_in"].astype(BF16).reshape(L, 2, D // 2, a["w_in"].shape[2])
    rest = [_pack_rest(a["s5_w_glu"][l].astype(BF16), a["w_branch"][l].astype(BF16), a["w_out"][l].astype(BF16))
            for l in range(L)]
    x = a["x"][0]
    land_w = gather_rows(w_in[0], "gather_w_in_first")
    land_r = gather_rows(rest[0], "gather_rest_first")
    saved = []
    for l in range(L):
        last = l + 1 == L
        h, vjp_rms = jax.vjp(_rms_bf16, x, a["norm_w"][l])
        w_l = _place_own(land_w, w_in[l])
        if last:
            z = _mm_nn(h, w_l, None, "w_in_fwd")
        else:
            z, land_w = _mm_nn(h, w_l, None, "w_in_fwd_gather", side=_gather_side(w_in[l + 1]))
        lw = {n: wts[n][l] for n in _LAYER_SMALL}
        lw["lbs"] = lbs[l]
        r_l = lax.dynamic_update_slice(land_r, rest[l][None], (chip, 0, 0, 0))
        x_next, vjp_tail, land_r = jax.vjp(functools.partial(_layer_tail, nxt=None if last else rest[l + 1]),
                                           z, x, lw, r_l, has_aux=True)
        saved.append((h, w_l, vjp_rms, vjp_tail))
        x = x_next
    rows, vjp_loss = jax.vjp(_loss_rows, x, a["final_norm_w"], a["loss_target"][0])
    loss = lax.psum(jnp.sum(rows), ("x", "y", "c"))
    dx, d_final, _ = vjp_loss(jnp.ones_like(rows))
    gw = {n: [None] * L for n in _LAYER_SMALL + ["lbs", "norm_w"]}
    pairs = {"w": [None] * L, "r": [None] * L}
    recvs = {"w": [None] * L, "r": [None] * L}

    def pair_sum(g, tag):
        got = _swap_rows(g, "reduce_" + tag + "_swap")
        mine = lax.dynamic_index_in_dim(g, c_idx, 1, keepdims=False)
        return _ew_call(lambda p, q: [p.astype(F32) + q.astype(F32)], [mine, got], [BF16], "reduce_" + tag + "_pair")[0]

    for l in reversed(range(L)):
        h, w_l, vjp_rms, vjp_tail = saved[l]
        dz, dx_res, dlw, d_rest = vjp_tail(dx)
        if l + 1 == L:
            dh = _mm_nt(dz, w_l, h.dtype, "w_in_bwd_a")
            dw = _mm_tn(h, dz, N_CHIPS, BF16, "w_in_bwd_w")
        else:
            dh, recvs["r"][l + 1] = _mm_nt(dz, w_l, h.dtype, "w_in_bwd_a_scatter", side=_scatter_side(pairs["r"][l + 1]))
            dw, recvs["w"][l + 1] = _mm_tn(h, dz, N_CHIPS, BF16, "w_in_bwd_w_scatter", side=_scatter_side(pairs["w"][l + 1]))
        dx_rms, gw["norm_w"][l] = vjp_rms(dh)
        dx = dx_rms + dx_res
        for n in _LAYER_SMALL + ["lbs"]:
            gw[n][l] = dlw[n]
        pairs["w"][l] = pair_sum(dw.reshape(N_CHIPS, 2, D // 2, dw.shape[2]), "w_in")
        pairs["r"][l] = pair_sum(d_rest, "rest")
    recvs["w"][0] = scatter_rows(pairs["w"][0], "reduce_w_in_scatter")
    recvs["r"][0] = scatter_rows(pairs["r"][0], "reduce_rest_scatter")
    south = c_idx == 0

    def finish(tag, name):
        halves = []
        for l in range(L):
            own = lax.dynamic_index_in_dim(pairs[tag][l], chip, 0, keepdims=False)
            r = recvs[tag][l]
            halves.append(_ew_call(
                lambda p, r0, r1, r2: [((p.astype(F32) + r0.astype(F32)) + r1.astype(F32)) + r2.astype(F32)],
                [own, r[0], r[1], r[2]], [F32], "reduce_" + name + "_sum")[0])
        gh = jnp.stack(halves, 0)
        other = _swap_whole(gh, "reduce_" + name + "_join")
        return jnp.stack([jnp.where(south, gh, other), jnp.where(south, other, gh)], 1)

    grads = {"w_in": finish("w", "w_in").reshape(a["w_in"].shape)}
    g_rest = _unpack_rest(finish("r", "rest"), W, D)
    grads["w_branch"] = jnp.stack(g_rest["w_branch"], 1)
    grads["w_out"] = g_rest["w_out"]
    grads["s5_w_glu"] = g_rest["s5_w_glu"]
    gx = dx
    (d_lb,) = vjp_lbs(jnp.stack(gw["lbs"], 0))
    gw = {n: jnp.stack(v, 0) for n, v in gw.items()}
    gw["hg_lower_bounds"] = d_lb
    gw["final_norm_w"] = d_final
    small_shapes = [gw[n].shape for n in _SMALL]
    red = _unpack(all_reduce_small(_pack([gw[n].astype(F32) for n in _SMALL]), "reduce_small"), small_shapes)
    for n, g in zip(_SMALL, red):
        grads[n] = g
    grads["rg_conv_w"] = lax.dynamic_slice_in_dim(grads["rg_conv_w"], chip * wc, wc, 2)
    delta, new_m, new_v = {}, {}, {}
    for n in _BIG:
        delta[n], new_m[n], new_v[n] = adamw(a[n], grads[n], a["m_" + n], a["v_" + n], "adamw_" + n)
    shapes = [a[n].shape for n in _SMALL]
    packed = [_pack([t[n] for n in _SMALL]) for t in
              (a, grads, {n: a["m_" + n] for n in _SMALL}, {n: a["v_" + n] for n in _SMALL})]
    for dst, buf in zip((delta, new_m, new_v), adamw(*packed, "adamw_small")):
        for n, t in zip(_SMALL, _unpack(buf, shapes)):
            dst[n] = t
    return (loss, gx[None], *[grads[n] for n in _WEIGHTS], *[delta[n] for n in _WEIGHTS],
            *[new_m[n] for n in _WEIGHTS], *[new_v[n] for n in _WEIGHTS])


_ARG_NAMES = ["x"] + _WEIGHTS + ["loss_target"] + ["m_" + n for n in _WEIGHTS] + ["v_" + n for n in _WEIGHTS]


def kernel(x, norm_w, w_in, s5_lambda_re, s5_lambda_im, s5_log_step, s5_b_re, s5_b_im, s5_c_re, s5_c_im, s5_d, s5_w_glu, s5_b_glu, rg_conv_w, rg_conv_b, rg_w_a, rg_b_a, rg_w_x, rg_b_x, rg_lambda, hg_lower_bounds, hg_norm_w, w_branch, w_out, final_norm_w, loss_target, m_norm_w, m_w_in, m_s5_lambda_re, m_s5_lambda_im, m_s5_log_step, m_s5_b_re, m_s5_b_im, m_s5_c_re, m_s5_c_im, m_s5_d, m_s5_w_glu, m_s5_b_glu, m_rg_conv_w, m_rg_conv_b, m_rg_w_a, m_rg_b_a, m_rg_w_x, m_rg_b_x, m_rg_lambda, m_hg_lower_bounds, m_hg_norm_w, m_w_branch, m_w_out, m_final_norm_w, v_norm_w, v_w_in, v_s5_lambda_re, v_s5_lambda_im, v_s5_log_step, v_s5_b_re, v_s5_b_im, v_s5_c_re, v_s5_c_im, v_s5_d, v_s5_w_glu, v_s5_b_glu, v_rg_conv_w, v_rg_conv_b, v_rg_w_a, v_rg_b_a, v_rg_w_x, v_rg_b_x, v_rg_lambda, v_hg_lower_bounds, v_hg_norm_w, v_w_branch, v_w_out, v_final_norm_w):
    vals = (x, norm_w, w_in, s5_lambda_re, s5_lambda_im, s5_log_step, s5_b_re, s5_b_im, s5_c_re, s5_c_im, s5_d, s5_w_glu, s5_b_glu, rg_conv_w, rg_conv_b, rg_w_a, rg_b_a, rg_w_x, rg_b_x, rg_lambda, hg_lower_bounds, hg_norm_w, w_branch, w_out, final_norm_w, loss_target, m_norm_w, m_w_in, m_s5_lambda_re, m_s5_lambda_im, m_s5_log_step, m_s5_b_re, m_s5_b_im, m_s5_c_re, m_s5_c_im, m_s5_d, m_s5_w_glu, m_s5_b_glu, m_rg_conv_w, m_rg_conv_b, m_rg_w_a, m_rg_b_a, m_rg_w_x, m_rg_b_x, m_rg_lambda, m_hg_lower_bounds, m_hg_norm_w, m_w_branch, m_w_out, m_final_norm_w, v_norm_w, v_w_in, v_s5_lambda_re, v_s5_lambda_im, v_s5_log_step, v_s5_b_re, v_s5_b_im, v_s5_c_re, v_s5_c_im, v_s5_d, v_s5_w_glu, v_s5_b_glu, v_rg_conv_w, v_rg_conv_b, v_rg_w_a, v_rg_b_a, v_rg_w_x, v_rg_b_x, v_rg_lambda, v_hg_lower_bounds, v_hg_norm_w, v_w_branch, v_w_out, v_final_norm_w)
    return _step(dict(zip(_ARG_NAMES, vals)))
```

```python
import functools
import math

import jax
import jax.numpy as jnp
from jax import lax
from jax.experimental import pallas as pl
from jax.experimental.pallas import tpu as pltpu

F32 = jnp.float32
BF16 = jnp.bfloat16
EPS = 1e-6
RG_C = 8.0
S5_GROUP = 16
S5_STATE = 64
RG_BLOCKS = 16
RG_CONV = 4
HG_HEADS = 8
N_CHIPS = 4
VMEM_LIMIT_BYTES = 56 * 1024 * 1024
NEG_BIG = -1e30

ADAM_LR = 0.001
ADAM_B1 = 0.9
ADAM_B2 = 0.999
ADAM_EPS = 1e-08
ADAM_WD = 0.01
ADAM_STEP = 10

MESH = pl.DeviceIdType.MESH


def _cparams(sem):
    return pltpu.CompilerParams(dimension_semantics=sem, vmem_limit_bytes=VMEM_LIMIT_BYTES)


_DOT_DIMS = {"nn": (((1,), (0,)), ((), ())), "nt": (((1,), (1,)), ((), ())), "tn": (((0,), (0,)), ((), ()))}


def _bdot_raw(a, b, form):
    return lax.dot_general(a.astype(BF16), b.astype(BF16), _DOT_DIMS[form], preferred_element_type=F32)


@functools.partial(jax.custom_vjp, nondiff_argnums=(2,))
def bdot(a, b, form):
    return _bdot_raw(a, b, form)


def _bdot_fwd(a, b, form):
    return _bdot_raw(a, b, form), (a, b)


def _bdot_bwd(form, res, g):
    a, b = res
    if form == "nn":
        da, db = _bdot_raw(g, b, "nt"), _bdot_raw(a, g, "tn")
    elif form == "nt":
        da, db = _bdot_raw(g, b, "nn"), _bdot_raw(g, a, "tn")
    else:
        da, db = _bdot_raw(b, g, "nt"), _bdot_raw(a, g, "nn")
    return da.astype(a.dtype), db.astype(b.dtype)


bdot.defvjp(_bdot_fwd, _bdot_bwd)


def _onehot_mm(p, g, form):
    pb = p.astype(BF16)
    g1 = g.astype(BF16)
    r1 = g - g1.astype(F32)
    g2 = r1.astype(BF16)
    g3 = (r1 - g2.astype(F32)).astype(BF16)
    dn = _DOT_DIMS[form]
    return (lax.dot_general(pb, g1, dn, preferred_element_type=F32)
            + lax.dot_general(pb, g2, dn, preferred_element_type=F32)
            + lax.dot_general(pb, g3, dn, preferred_element_type=F32))


@jax.custom_vjp
def _pick_rows(p, g):
    return _onehot_mm(p, g, "nn")


def _pick_rows_fwd(p, g):
    return _onehot_mm(p, g, "nn"), p


def _pick_rows_bwd(p, ct):
    return jnp.zeros_like(p), _onehot_mm(p, ct, "tn")


_pick_rows.defvjp(_pick_rows_fwd, _pick_rows_bwd)


@functools.partial(jax.custom_vjp, nondiff_argnums=(1,))
def sroll(x, d):
    return pltpu.roll(x, d, 0)


def _sroll_fwd(x, d):
    return pltpu.roll(x, d, 0), None


def _sroll_bwd(d, _, g):
    return (pltpu.roll(g, g.shape[0] - d, 0),)


sroll.defvjp(_sroll_fwd, _sroll_bwd)


def _row_iota(n):
    return lax.broadcasted_iota(jnp.int32, (n, 1), 0)


def _last_row(x, ti):
    return jnp.sum(jnp.where(ti == x.shape[0] - 1, x, 0.0), axis=0, keepdims=True)


def _silu(x):
    return x * jax.nn.sigmoid(x)


class _Tiling:
    def __init__(self, CB, tT, nT, ncb=None, bmap=None):
        self.CB, self.tT, self.nT = CB, tT, nT
        self.ncb = CB if ncb is None else ncb
        self.bmap = (lambda cb, t: (t, cb)) if bmap is None else bmap


def _tiled_specs(params, views, carry_shapes, out_defs, tl, rev):
    nT = tl.nT
    tmap = (lambda t: nT - 1 - t) if rev else (lambda t: t)

    def rspec(col0, width):
        bw = width // tl.ncb
        off = col0 // bw

        def imap(cb, t):
            rb, cbk = tl.bmap(cb, tmap(t))
            return (rb, off + cbk)

        return pl.BlockSpec((tl.tT, bw), imap)

    p_specs = [pl.BlockSpec((None,) + p.shape[1:], lambda cb, t: (cb, 0, 0)) for p in params]
    r_specs = [rspec(c0, w) for (c0, w) in views]
    dr_specs = [rspec(0, w) for (_, w) in views]
    o_specs = [rspec(0, w) for (w, _) in out_defs]
    s_specs = [pl.BlockSpec((None, r, c), lambda cb, t: (tmap(t), 0, cb)) for (r, c) in carry_shapes]
    return p_specs, r_specs, dr_specs, o_specs, s_specs


def _tiled_fwd(f, name, params, rows, views, carry_shapes, out_defs, tl, side=None):
    T = rows[0].shape[0]
    CB, tT, nT = tl.CB, tl.tT, tl.nT
    n_p, n_r, n_o, n_c = len(params), len(rows), len(out_defs), len(carry_shapes)
    p_specs, r_specs, _, o_specs, s_specs = _tiled_specs(params, views, carry_shapes, out_defs, tl, False)
    x_ins, x_outs, x_sems = _side_parts(side)
    n_xi, n_xo = len(x_ins), len(x_outs)

    def body(*refs):
        i = 0
        p_refs = refs[i:i + n_p]; i += n_p
        r_refs = refs[i:i + n_r]; i += n_r
        xi_refs = refs[i:i + n_xi]; i += n_xi
        o_refs = refs[i:i + n_o]; i += n_o
        s_refs = refs[i:i + n_c]; i += n_c
        xo_refs = refs[i:i + n_xo]; i += n_xo
        c_refs = refs[i:i + n_c]
        cb, t = pl.program_id(0), pl.program_id(1)
        if side is not None:
            @pl.when((cb == 0) & (t == 0))
            def _():
                side.start(xi_refs, xo_refs, refs[-2], refs[-1])

        @pl.when(t == 0)
        def _():
            for c in c_refs:
                c[...] = jnp.zeros_like(c)

        carries = [c[...] for c in c_refs]
        for s, cv in zip(s_refs, carries):
            s[...] = cv
        outs, newc = f([p[...] for p in p_refs], [r[...] for r in r_refs], carries, t * tT)
        for o, v in zip(o_refs, outs):
            o[...] = v.astype(o.dtype)
        for c, v in zip(c_refs, newc):
            c[...] = v
        if side is not None:
            @pl.when((cb == CB - 1) & (t == nT - 1))
            def _():
                side.finish(xi_refs, xo_refs, refs[-2], refs[-1])

    out_shape = [jax.ShapeDtypeStruct((T, w), dt) for (w, dt) in out_defs]
    out_shape += [jax.ShapeDtypeStruct((nT, r, c * CB), F32) for (r, c) in carry_shapes]
    res = pl.pallas_call(
        body, name=name + "_fwd", grid=(CB, nT),
        in_specs=p_specs + r_specs + [_ANY] * n_xi, out_specs=o_specs + s_specs + [_ANY] * n_xo,
        out_shape=out_shape + x_outs,
        scratch_shapes=[pltpu.VMEM((r, c), F32) for (r, c) in carry_shapes] + x_sems,
        compiler_params=_cparams(("arbitrary", "arbitrary")),
    )(*params, *rows, *x_ins)
    return list(res[:n_o]), list(res[n_o:n_o + n_c]), list(res[n_o + n_c:])


def _tiled_bwd(f, name, params, rows, views, saved, douts, carry_shapes, out_defs, tl):
    T = rows[0].shape[0]
    CB, tT, nT = tl.CB, tl.tT, tl.nT
    n_p, n_r, n_o, n_c = len(params), len(rows), len(out_defs), len(carry_shapes)
    p_specs, r_specs, dr_specs, o_specs, s_specs = _tiled_specs(params, views, carry_shapes, out_defs, tl, True)
    out_dtypes = [dt for (_, dt) in out_defs]

    def body(*refs):
        i = 0
        p_refs = refs[i:i + n_p]; i += n_p
        r_refs = refs[i:i + n_r]; i += n_r
        s_refs = refs[i:i + n_c]; i += n_c
        g_refs = refs[i:i + n_o]; i += n_o
        dp_refs = refs[i:i + n_p]; i += n_p
        dr_refs = refs[i:i + n_r]; i += n_r
        dc_refs = refs[i:]
        t = pl.program_id(1)

        @pl.when(t == 0)
        def _():
            for c in dc_refs:
                c[...] = jnp.zeros_like(c)
            for d in dp_refs:
                d[...] = jnp.zeros_like(d)

        t0 = (nT - 1 - t) * tT

        def g(P, R, C):
            outs, newc = f(P, R, C, t0)
            return [o.astype(dt) for o, dt in zip(outs, out_dtypes)], list(newc)

        _, vjp = jax.vjp(g, [p[...] for p in p_refs], [r[...] for r in r_refs], [s[...] for s in s_refs])
        dP, dR, dC = vjp(([gr[...] for gr in g_refs], [c[...] for c in dc_refs]))
        for d, v in zip(dp_refs, dP):
            d[...] += v
        for d, v in zip(dr_refs, dR):
            d[...] = v.astype(d.dtype)
        for c, v in zip(dc_refs, dC):
            c[...] = v

    out_shape = [jax.ShapeDtypeStruct(p.shape, F32) for p in params]
    out_shape += [jax.ShapeDtypeStruct((T, w), r.dtype) for r, (_, w) in zip(rows, views)]
    res = pl.pallas_call(
        body, name=name + "_bwd", grid=(CB, nT),
        in_specs=p_specs + r_specs + s_specs + o_specs, out_specs=p_specs + dr_specs, out_shape=out_shape,
        scratch_shapes=[pltpu.VMEM((r, c), F32) for (r, c) in carry_shapes],
        compiler_params=_cparams(("arbitrary", "arbitrary")),
    )(*params, *rows, *saved, *douts)
    return list(res[:n_p]), list(res[n_p:])


def tiled_op(f, name, params, rows, carry_shapes, out_defs, CB, tT, tiling=None, side_of=None):
    arrs = [r[0] if isinstance(r, tuple) else r for r in rows]
    views = [(r[1], r[2]) if isinstance(r, tuple) else (0, r.shape[1]) for r in rows]
    tl = tiling if tiling is not None else _Tiling(CB, tT, arrs[0].shape[0] // tT)
    make_side, side_arrs = side_of if side_of is not None else (None, [])

    def run_fwd(params, arrs, side_arrs):
        side = make_side(*side_arrs) if make_side is not None else None
        return _tiled_fwd(f, name, params, arrs, views, carry_shapes, out_defs, tl, side)

    @jax.custom_vjp
    def op(params, arrs, side_arrs):
        outs, _, extra = run_fwd(params, arrs, side_arrs)
        return outs, extra

    def op_fwd(params, arrs, side_arrs):
        outs, saved, extra = run_fwd(params, arrs, side_arrs)
        return (outs, extra), (params, arrs, saved, side_arrs)

    def op_bwd(res, cts):
        params, arrs, saved, side_arrs = res
        douts, _ = cts
        dP, dR = _tiled_bwd(f, name, params, arrs, views, saved, list(douts), carry_shapes, out_defs, tl)
        dR = [d if w == a.shape[1] else jnp.pad(d, ((0, 0), (c0, a.shape[1] - c0 - w)))
              for d, a, (c0, w) in zip(dR, arrs, views)]
        return dP, dR, [jnp.zeros_like(s) for s in side_arrs]

    op.defvjp(op_fwd, op_bwd)
    outs, extra = op(list(params), arrs, list(side_arrs))
    return outs if side_of is None else (outs, extra)


def _pick(n, pref):
    for t in pref:
        if n % t == 0:
            return t
    return n


class _Side:
    def __init__(self, ins, outs, n_sems, start, finish):
        self.ins, self.outs, self.n_sems, self.start, self.finish = ins, outs, n_sems, start, finish


def _side_parts(side):
    if side is None:
        return [], [], []
    sems = [pltpu.SemaphoreType.DMA((side.n_sems,)), pltpu.SemaphoreType.DMA((side.n_sems,))]
    return list(side.ins), list(side.outs), sems


def _mm_nn(a, w, res, name, out_dtype=F32, side=None):
    M, K = a.shape
    S, _, Ns = w.shape
    tm = _pick(M, (1024, 512, 256, 128))
    tn = _pick(Ns, (512, 256, 128))
    nps = Ns // tn
    has_res = res is not None
    n_main = 3 if has_res else 2
    s_ins, s_outs, s_sems = _side_parts(side)
    ni, nj = M // tm, S * nps

    def body(*refs):
        a_ref, w_ref = refs[0], refs[1]
        o_ref = refs[n_main + len(s_ins)]
        if side is not None:
            si = refs[n_main:n_main + len(s_ins)]
            so = refs[n_main + len(s_ins) + 1:n_main + len(s_ins) + 1 + len(s_outs)]
            send, recv = refs[-2], refs[-1]
            i, j = pl.program_id(0), pl.program_id(1)

            @pl.when((i == 0) & (j == 0))
            def _():
                side.start(si, so, send, recv)

        acc = _bdot_raw(a_ref[...], w_ref[...], "nn")
        if has_res:
            acc = acc + refs[2][...]
        o_ref[...] = acc.astype(o_ref.dtype)
        if side is not None:
            @pl.when((i == ni - 1) & (j == nj - 1))
            def _():
                side.finish(si, so, send, recv)

    in_specs = [pl.BlockSpec((tm, K), lambda i, j: (i, 0)),
                pl.BlockSpec((None, K, tn), lambda i, j: (j // nps, 0, j % nps))]
    args = [a, w]
    if has_res:
        in_specs.append(pl.BlockSpec((tm, tn), lambda i, j: (i, j)))
        args.append(res)
    out = pl.pallas_call(
        body, name=name, grid=(ni, nj), in_specs=in_specs + [_ANY] * len(s_ins),
        out_specs=[pl.BlockSpec((tm, tn), lambda i, j: (i, j))] + [_ANY] * len(s_outs),
        out_shape=[jax.ShapeDtypeStruct((M, S * Ns), out_dtype)] + s_outs,
        scratch_shapes=s_sems,
        compiler_params=_cparams(("arbitrary", "arbitrary")),
    )(*args, *s_ins)
    return out[0] if side is None else out


def _mm_nt(g, w, out_dtype, name, side=None):
    M, N = g.shape
    S, K, Ns = w.shape
    tm = _pick(M, (1024, 512, 256, 128))
    tk = _pick(K, (1024, 512, 256, 128))
    tn = _pick(Ns, (1792, 1024, 512, 256, 128))
    nps = Ns // tn
    nn = S * nps
    ni, nk = M // tm, K // tk
    s_ins, s_outs, s_sems = _side_parts(side)

    def body(*refs):
        g_ref, w_ref = refs[0], refs[1]
        o_ref = refs[2 + len(s_ins)]
        acc_ref = refs[3 + len(s_ins) + len(s_outs)]
        i, k, n = pl.program_id(0), pl.program_id(1), pl.program_id(2)
        if side is not None:
            si = refs[2:2 + len(s_ins)]
            so = refs[3 + len(s_ins):3 + len(s_ins) + len(s_outs)]

            @pl.when((i == 0) & (k == 0) & (n == 0))
            def _():
                side.start(si, so, refs[-2], refs[-1])

        @pl.when(n == 0)
        def _():
            acc_ref[...] = jnp.zeros_like(acc_ref)

        acc_ref[...] += _bdot_raw(g_ref[...], w_ref[...], "nt")

        @pl.when(n == nn - 1)
        def _():
            o_ref[...] = acc_ref[...].astype(o_ref.dtype)

        if side is not None:
            @pl.when((i == ni - 1) & (k == nk - 1) & (n == nn - 1))
            def _():
                side.finish(si, so, refs[-2], refs[-1])

    out = pl.pallas_call(
        body, name=name, grid=(ni, nk, nn),
        in_specs=[pl.BlockSpec((tm, tn), lambda i, k, n: (i, n)),
                  pl.BlockSpec((None, tk, tn), lambda i, k, n: (n // nps, k, n % nps))] + [_ANY] * len(s_ins),
        out_specs=[pl.BlockSpec((tm, tk), lambda i, k, n: (i, k))] + [_ANY] * len(s_outs),
        out_shape=[jax.ShapeDtypeStruct((M, K), out_dtype)] + s_outs,
        scratch_shapes=[pltpu.VMEM((tm, tk), F32)] + s_sems,
        compiler_params=_cparams(("arbitrary", "arbitrary", "arbitrary")),
    )(g, w, *s_ins)
    return out[0] if side is None else out


def _mm_tn(a, g, S, out_dtype, name, side=None):
    T, K = a.shape
    N = g.shape[1]
    Ns = N // S
    tk = _pick(K, (2048, 1024, 512, 256, 128))
    tn = _pick(Ns, (1024, 896, 512, 256, 128))
    tt = _pick(T, (1024, 512, 256, 128))
    nps = Ns // tn
    nt = T // tt
    nk, nj = K // tk, S * nps
    a_t = a.astype(BF16).T
    s_ins, s_outs, s_sems = _side_parts(side)

    def body(*refs):
        a_ref, g_ref = refs[0], refs[1]
        o_ref = refs[2 + len(s_ins)]
        acc_ref = refs[3 + len(s_ins) + len(s_outs)]
        k, j, t = pl.program_id(0), pl.program_id(1), pl.program_id(2)
        if side is not None:
            si = refs[2:2 + len(s_ins)]
            so = refs[3 + len(s_ins):3 + len(s_ins) + len(s_outs)]
            send, recv = refs[-2], refs[-1]

            @pl.when((k == 0) & (j == 0) & (t == 0))
            def _():
                side.start(si, so, send, recv)

        @pl.when(t == 0)
        def _():
            acc_ref[...] = jnp.zeros_like(acc_ref)

        acc_ref[...] += _bdot_raw(a_ref[...], g_ref[...], "nn")

        @pl.when(t == nt - 1)
        def _():
            o_ref[...] = acc_ref[...].astype(o_ref.dtype)

        if side is not None:
            @pl.when((k == nk - 1) & (j == nj - 1) & (t == nt - 1))
            def _():
                side.finish(si, so, send, recv)

    out = pl.pallas_call(
        body, name=name, grid=(nk, nj, nt),
        in_specs=[pl.BlockSpec((tk, tt), lambda k, j, t: (k, t)),
                  pl.BlockSpec((tt, tn), lambda k, j, t: (t, j))] + [_ANY] * len(s_ins),
        out_specs=[pl.BlockSpec((None, tk, tn), lambda k, j, t: (j // nps, k, j % nps))] + [_ANY] * len(s_outs),
        out_shape=[jax.ShapeDtypeStruct((S, K, Ns), out_dtype)] + s_outs,
        scratch_shapes=[pltpu.VMEM((tk, tn), F32)] + s_sems,
        compiler_params=_cparams(("arbitrary", "arbitrary", "arbitrary")),
    )(a_t, g, *s_ins)
    return out[0] if side is None else out


def linear(a, w, name, res=None):
    @jax.custom_vjp
    def op(a, w, res):
        return _mm_nn(a, w, res, name + "_fwd")

    def op_fwd(a, w, res):
        return _mm_nn(a, w, res, name + "_fwd"), (a, w)

    def op_bwd(saved, g):
        a, w = saved
        da = _mm_nt(g, w, a.dtype, name + "_bwd_a")
        dw = _mm_tn(a, g, w.shape[0], w.dtype, name + "_bwd_w")
        return da, dw, (None if res is None else g)

    op.defvjp(op_fwd, op_bwd)
    return op(a, w, res)


def _rms_tile(params, rows, carries, t0):
    (w,), (x,) = params, rows
    y = x * lax.rsqrt(jnp.mean(x * x, axis=-1, keepdims=True) + EPS) * w
    return [y], []


def _s5_tile(params, rows, carries, t0):
    t_m, s_re, s_im, r_re, r_im, apow_re, apow_im, d = params
    (u,) = rows
    n = u.shape[0]
    ti = _row_iota(n)
    pi = _row_iota(apow_re.shape[0])
    x_re, x_im = bdot(u, s_re, "nn"), bdot(u, s_im, "nn")
    k = 0
    while (1 << k) < n:
        sh = 1 << k
        p_re, p_im = _row_sel(apow_re, pi, k), _row_sel(apow_im, pi, k)
        q_re, q_im = sroll(x_re, sh), sroll(x_im, sh)
        m = ti >= sh
        x_re, x_im = (x_re + jnp.where(m, p_re * q_re - p_im * q_im, 0.0),
                      x_im + jnp.where(m, p_re * q_im + p_im * q_re, 0.0))
        k += 1
    x_re = jnp.where(ti >= 1, sroll(x_re, 1), 0.0)
    x_im = jnp.where(ti >= 1, sroll(x_im, 1), 0.0)
    y = bdot(u, t_m, "nn") + bdot(x_re, r_re, "nn") + bdot(x_im, r_im, "nn") + d * u
    return [jax.nn.gelu(y)], []


def _glu_tile(params, rows, carries, t0):
    (b,), (y, zg, ga) = params, rows
    return [y * jax.nn.sigmoid(zg + b) * _silu(ga)], []


def _neg_expm1(z):
    small = -(z * (1.0 + z * (0.5 + z * (1.0 / 6.0))))
    return jnp.where(z > -0.01, small, 1.0 - jnp.exp(z))


def _rg_tile(params, rows, carries, t0):
    conv_w, conv_b, w_a, b_a, w_x, b_x, sp = params
    x, gate = rows
    x_prev, h_prev = carries
    tT = x.shape[0]
    ti = _row_iota(tT)
    ci = _row_iota(RG_CONV)
    xc = _row_sel(conv_w, ci, RG_CONV - 1) * x + conv_b
    for k in range(1, RG_CONV):
        xs = jnp.where(ti >= k, sroll(x, k), sroll(x_prev, k))
        xc = xc + _row_sel(conv_w, ci, RG_CONV - 1 - k) * xs
    r = jax.nn.sigmoid(bdot(xc, w_a, "nn") + b_a)
    i = jax.nn.sigmoid(bdot(xc, w_x, "nn") + b_x)
    log_a = -RG_C * r * sp
    a = jnp.exp(log_a)
    mult = jnp.sqrt(_neg_expm1(2.0 * log_a))
    mult = jnp.where(ti + t0 == 0, 1.0, mult)
    b = mult * (i * xc)
    b = b + jnp.where(ti == 0, a * h_prev, 0.0)
    k = 1
    while k < tT:
        m = ti >= k
        b = b + jnp.where(m, a * sroll(b, k), 0.0)
        a = jnp.where(m, a * sroll(a, k), a)
        k *= 2
    return [b * _silu(gate)], [x, _last_row(b, ti)]


def _hg_tile(params, rows, carries, t0):
    lb, nw = params
    q, fl, v, gate = rows
    (st,) = carries
    tT = q.shape[0]
    ti = _row_iota(tT)
    qs = _silu(q)
    f = lb + (1.0 - lb) * jax.nn.sigmoid(fl)
    kk = 1.0 - f
    G = jnp.log(f)
    k = 1
    while k < tT:
        G = G + jnp.where(ti >= k, sroll(G, k), 0.0)
        k *= 2
    inter = bdot(qs * jnp.exp(G), st, "nt")
    tr = lax.broadcasted_iota(jnp.int32, (tT, tT), 0)
    sc = lax.broadcasted_iota(jnp.int32, (tT, tT), 1)
    attn = jnp.zeros((tT, tT), F32)
    blk = tT
    while blk > 8:
        sub = blk // 4
        for j in range(1, 4):
            if blk == tT:
                ref = jnp.broadcast_to(_row_sel(G, ti, sub * j - 1), G.shape)
            else:
                ref = _pick_rows((sc == _div2(tr, blk) * blk + (sub * j - 1)).astype(F32), G)
            tmask = _div2(_mod2(ti, blk), sub) == j
            smask = _mod2(ti, blk) < sub * j
            qt = qs * jnp.exp(jnp.where(tmask, G - ref, NEG_BIG))
            kt = kk * jnp.exp(jnp.where(smask, ref - G, NEG_BIG))
            aj = bdot(qt, kt, "nt")
            attn = attn + jnp.where(_div2(tr, blk) == _div2(sc, blk), aj, 0.0)
        blk = sub
    intra = bdot(attn, v, "nn")
    for d in range(blk):
        if d == 0:
            kd, gd, vd = kk, G, v
        else:
            kd, gd, vd = sroll(kk, d), sroll(G, d), sroll(v, d)
        m = _mod2(ti, blk) >= d
        w = jnp.sum(qs * kd * jnp.exp(jnp.where(m, G - gd, NEG_BIG)), axis=1, keepdims=True)
        intra = intra + w * vd
    o = inter + intra
    g_last = _last_row(G, ti)
    k_dec = kk * jnp.exp(g_last - G)
    st_new = st * jnp.exp(g_last) + bdot(v, k_dec, "tn")
    o = o * lax.rsqrt(jnp.mean(o * o, axis=-1, keepdims=True) + EPS) * nw
    return [o * _silu(gate)], [st_new]


def _row_sel(x, ti, r):
    return jnp.sum(jnp.where(ti == r, x, 0.0), axis=0, keepdims=True)


def _div2(i, p):
    return lax.shift_right_logical(i, jnp.int32(p.bit_length() - 1))


def _mod2(i, p):
    return lax.bitwise_and(i, jnp.int32(p - 1))


def _merge_tile(params, rows, carries, t0):
    b0, b1, b2, g0, g1, g2 = rows
    m = jax.nn.sigmoid(g0) * b0 + jax.nn.sigmoid(g1) * b1 + jax.nn.sigmoid(g2) * b2
    return [m], []


def _loss_tile(params, rows, carries, t0):
    (w,), (x, tgt) = params, rows
    y = x * lax.rsqrt(jnp.mean(x * x, axis=-1, keepdims=True) + EPS) * w
    e = y - tgt
    return [0.5 * jnp.mean(e * e, axis=-1, keepdims=True)], []


def _block_diag(w, cb):
    n, i, j = w.shape
    g = n // cb
    w4 = w.reshape(cb, g, i, j)
    eye = jnp.eye(g, dtype=w.dtype)
    return jnp.einsum("cgij,gk->cgikj", w4, eye).reshape(cb, g * i, g * j)


def _s5_params(lam_re, lam_im, log_step, b_re, b_im, c_re, c_im, d, levels):
    G, P = lam_re.shape
    step = jnp.exp(log_step)[:, None]
    mag = jnp.exp(lam_re * step)
    ang = lam_im * step
    abar_re = mag * jnp.cos(ang)
    abar_im = mag * jnp.sin(ang)
    num_re = abar_re - 1.0
    num_im = abar_im
    den = lam_re * lam_re + lam_im * lam_im
    coef_re = (num_re * lam_re + num_im * lam_im) / den
    coef_im = (num_im * lam_re - num_re * lam_im) / den
    bbar_re = coef_re[..., None] * b_re - coef_im[..., None] * b_im
    bbar_im = coef_re[..., None] * b_im + coef_im[..., None] * b_re
    H = b_re.shape[2]
    Lc = S5_LC
    hi = lax.Precision.HIGHEST

    def powers(ks):
        ks = jnp.asarray(ks, F32)[:, None, None]
        m = jnp.exp(ks * (lam_re * step))
        return m * jnp.cos(ks * ang), m * jnp.sin(ks * ang)

    pw_re, pw_im = powers(list(range(Lc + 1)))
    ab_re = pw_re[..., None] * bbar_re - pw_im[..., None] * bbar_im
    ab_im = pw_re[..., None] * bbar_im + pw_im[..., None] * bbar_re
    kern = (jnp.einsum("gap,kgph->gkha", c_re, ab_re[:Lc], precision=hi)
            - jnp.einsum("gap,kgph->gkha", c_im, ab_im[:Lc], precision=hi))
    kk = jnp.arange(Lc)[:, None, None]
    jj = jnp.arange(Lc)[None, :, None]
    ii = jnp.arange(Lc)[None, None, :]
    place = (ii - jj == kk).astype(F32)
    t_m = jnp.einsum("gkha,kji->gjhia", kern, place, precision=hi).reshape(G, Lc * H, Lc * H)
    s_re = ab_re[:Lc][::-1].transpose(1, 0, 3, 2).reshape(G, Lc * H, P)
    s_im = ab_im[:Lc][::-1].transpose(1, 0, 3, 2).reshape(G, Lc * H, P)
    m_re = c_re[None] * pw_re[1:, :, None, :] - c_im[None] * pw_im[1:, :, None, :]
    m_im = c_re[None] * pw_im[1:, :, None, :] + c_im[None] * pw_re[1:, :, None, :]
    r_re = m_re.transpose(1, 3, 0, 2).reshape(G, P, Lc * H)
    r_im = -m_im.transpose(1, 3, 0, 2).reshape(G, P, Lc * H)
    rows = max(8, levels)
    ap_re, ap_im = powers([Lc * (1 << k) for k in range(levels)] + [0] * (rows - levels))
    dd = jnp.tile(d.reshape(G, 1, H), (1, 1, Lc))
    return [t_m, s_re, s_im, r_re, r_im, ap_re.transpose(1, 0, 2), ap_im.transpose(1, 0, 2), dd]


_LANE = 128


def _lane_perm_matrix(lc, h):
    n = lc * _LANE
    src = jnp.arange(n).reshape(lc, _LANE // h, h).transpose(1, 0, 2).reshape(n)
    return (jnp.arange(n)[:, None] == src[None, :]).astype(BF16)


def _lane_perm(x, p, p_t, name):
    @jax.custom_vjp
    def op(x):
        return _mm_nn(x, p[None], None, name, BF16)

    def op_fwd(x):
        return _mm_nn(x, p[None], None, name, BF16), None

    def op_bwd(_, g):
        return (_mm_nn(g, p_t[None], None, name + "_t", x.dtype),)

    op.defvjp(op_fwd, op_bwd)
    return op(x)


def _to_chunks(u, lc, h, perm):
    t, w = u.shape
    x = u.reshape(t // lc, lc, w // _LANE, _LANE).transpose(2, 0, 1, 3).reshape((w // _LANE) * (t // lc), lc * _LANE)
    return _lane_perm(x, perm, perm.T, "s5_to_chunks")


def _from_chunks(y, lc, w, perm):
    tiles = w // _LANE
    n_chunks = y.shape[0] // tiles
    x = _lane_perm(y, perm.T, perm, "s5_from_chunks")
    return x.reshape(tiles, n_chunks, lc, _LANE).transpose(1, 2, 0, 3).reshape(n_chunks * lc, w)


def _vec(v, cb):
    return v.reshape(cb, 1, -1)


S5_LC = 8
RG_CB, RG_TT = 4, 256
HG_TT = 128
ROW_TT = 256
MERGE_TT = 128


_LAYER_SMALL = ['s5_lambda_re', 's5_lambda_im', 's5_log_step', 's5_b_re', 's5_b_im', 's5_c_re', 's5_c_im', 's5_d',
                's5_b_glu', 'rg_conv_w', 'rg_conv_b', 'rg_w_a', 'rg_b_a', 'rg_w_x', 'rg_b_x', 'rg_lambda', 'hg_norm_w']


def _lower_bounds(hg_lower_bounds):
    lb_sm = jax.nn.softmax(hg_lower_bounds, axis=0)
    return jnp.cumsum(lb_sm, axis=0) - lb_sm[0]


def _rms_bf16(x, w):
    T, D = x.shape
    return tiled_op(_rms_tile, "rms", [w.reshape(1, 1, D)], [x], [], [(D, BF16)], 1, min(ROW_TT, T))[0]


def _loss_rows(x, w, target):
    T, D = x.shape
    return tiled_op(_loss_tile, "loss", [w.reshape(1, 1, D)], [x, target], [], [(1, F32)], 1, min(ROW_TT, T))[0]


def _layer_tail(z, x, lw, rest, nxt):
    T, D = x.shape
    lw = dict(lw, **_unpack_rest(rest, lw["s5_d"].shape[0], D))
    W = lw["s5_d"].shape[0]
    row_tt = min(ROW_TT, T)
    rg_tt, hg_tt = min(RG_TT, T), min(HG_TT, T)
    n_chunks = T // S5_LC
    gpt = _LANE // S5_GROUP
    perm = _lane_perm_matrix(S5_LC, S5_GROUP)
    s5_tiling = _Tiling(W // S5_GROUP, n_chunks, 1, ncb=gpt, bmap=lambda cb, t: (cb // gpt, cb % gpt))
    g_a, x_b, g_b, q_c, f_c, i_c, g_c = [(z, k * W, W) for k in range(1, 8)]
    gl = [(z, 8 * W + n * D, D) for n in range(3)]
    s5p = _s5_params(lw["s5_lambda_re"], lw["s5_lambda_im"], lw["s5_log_step"], lw["s5_b_re"], lw["s5_b_im"],
                     lw["s5_c_re"], lw["s5_c_im"], lw["s5_d"], int(math.log2(n_chunks)))
    (y1c,) = tiled_op(_s5_tile, "s5", s5p, [_to_chunks(z[:, :W], S5_LC, S5_GROUP, perm)], [],
                      [(S5_LC * _LANE, BF16)], W // S5_GROUP, n_chunks, tiling=s5_tiling)
    y1 = _from_chunks(y1c, S5_LC, W, perm)
    zg = linear(y1, lw["s5_w_glu"].reshape(1, W, W), "w_glu")
    (y_a,) = tiled_op(_glu_tile, "glu", [lw["s5_b_glu"].reshape(1, 1, W)], [y1, zg, g_a], [], [(W, BF16)], 1, row_tt)
    rgp = [lw["rg_conv_w"].reshape(RG_CONV, RG_CB, W // RG_CB).transpose(1, 0, 2),
           _vec(lw["rg_conv_b"], RG_CB), _block_diag(lw["rg_w_a"], RG_CB), _vec(lw["rg_b_a"], RG_CB),
           _block_diag(lw["rg_w_x"], RG_CB), _vec(lw["rg_b_x"], RG_CB), _vec(jax.nn.softplus(-lw["rg_lambda"]), RG_CB)]
    rc = W // RG_CB
    (y_b,) = tiled_op(_rg_tile, "rg", rgp, [x_b, g_b], [(rg_tt, rc), (1, rc)], [(W, BF16)], RG_CB, rg_tt)
    dk = W // HG_HEADS
    hgp = [_vec(lw["lbs"], HG_HEADS), _vec(lw["hg_norm_w"], HG_HEADS)]
    land = None
    if nxt is None:
        (y_c,) = tiled_op(_hg_tile, "hg", hgp, [q_c, f_c, i_c, g_c], [(dk, dk)], [(W, BF16)], HG_HEADS, hg_tt)
    else:
        (y_c,), (land,) = tiled_op(_hg_tile, "hg_gather", hgp, [q_c, f_c, i_c, g_c], [(dk, dk)], [(W, BF16)],
                                   HG_HEADS, hg_tt, side_of=(_gather_side, [nxt]))
    br = [linear(y, lw["w_branch"][n], "w_br") for n, y in enumerate((y_a, y_b, y_c))]
    (mg,) = tiled_op(_merge_tile, "merge", [], br + gl, [], [(D, BF16)], 1, min(MERGE_TT, T))
    return linear(mg, lw["w_out"].reshape(1, D, D), "w_out", res=x), land


def _pack_rest(w_glu, w_branch, w_out):
    c = w_branch.shape[-1]
    parts = [w_branch[n].reshape(2, -1, c) for n in range(3)] + [w_out.reshape(2, -1, c), w_glu.reshape(2, -1, c)]
    return jnp.concatenate(parts, 1)


def _rest_rows(W, D):
    c = D // N_CHIPS
    return [W // 2] * 3 + [(D // N_CHIPS) * D // (2 * c), (W // N_CHIPS) * W // (2 * c)]


def _unpack_rest(p, W, D):
    lead = p.shape[:-3]
    out, off = [], 0
    for r in _rest_rows(W, D):
        out.append(p[..., off:off + r, :])
        off += r
    br = [b.reshape(lead + (W, D // N_CHIPS)) for b in out[:3]]
    w_out = out[3].reshape(lead + (D // N_CHIPS, D))
    w_glu = out[4].reshape(lead + (W // N_CHIPS, W))
    return {"w_branch": br, "w_out": w_out, "s5_w_glu": w_glu}


_ANY = pl.BlockSpec(memory_space=pl.ANY)


def _place():
    x, y, c = lax.axis_index("x"), lax.axis_index("y"), lax.axis_index("c")
    chips = [(1 - x, y), (x, 1 - y), (1 - x, 1 - y)]
    return x, y, c, chips


def _rcopy(src, dst, send_sems, recv_sems, k, to):
    return pltpu.make_async_remote_copy(src_ref=src, dst_ref=dst, send_sem=send_sems.at[k], recv_sem=recv_sems.at[k],
                                        device_id=to, device_id_type=MESH)


def _gather_side(w):
    shape = (N_CHIPS,) + w.shape

    def first(src, land, send, recv):
        x, y, c, chips = _place()
        me = 2 * x + y
        return [_rcopy(src.at[c], land.at[me, c], send, recv, j, (*chip, c)) for j, chip in enumerate(chips)]

    def start(ins, outs, send, recv):
        for cp in first(ins[0], outs[0], send, recv):
            cp.start()

    def finish(ins, outs, send, recv):
        land = outs[0]
        x, y, c, chips = _place()
        sibling = (x, y, 1 - c)
        passed = []
        for j, chip in enumerate(chips):
            s = 2 * chip[0] + chip[1]
            _rcopy(land.at[s, c], land.at[s, c], send, recv, j, (*chip, c)).wait_recv()
            cp = _rcopy(land.at[s, c], land.at[s, c], send, recv, 3 + j, sibling)
            cp.start()
            passed.append(cp)
        for j, chip in enumerate(chips):
            s = 2 * chip[0] + chip[1]
            _rcopy(land.at[s, 1 - c], land.at[s, 1 - c], send, recv, 3 + j, sibling).wait_recv()
        for cp in first(ins[0], land, send, recv) + passed:
            cp.wait_send()

    return _Side([w], [jax.ShapeDtypeStruct(shape, w.dtype)], 6, start, finish)


def _place_own(land, w):
    chip = 2 * lax.axis_index("x") + lax.axis_index("y")
    full = lax.dynamic_update_slice(land, w[None], (chip, 0, 0, 0))
    return full.reshape(N_CHIPS, 2 * w.shape[1], w.shape[2])


def gather_rows(w, name):
    side = _gather_side(w)

    def body(w_ref, o_ref, send, recv):
        side.start([w_ref], [o_ref], send, recv)
        side.finish([w_ref], [o_ref], send, recv)

    return pl.pallas_call(
        body, name=name, in_specs=[_ANY], out_specs=_ANY, out_shape=side.outs[0],
        scratch_shapes=[pltpu.SemaphoreType.DMA((6,)), pltpu.SemaphoreType.DMA((6,))],
    )(w)


def _scatter_side(p):
    def copies(src, dst, send, recv):
        x, y, c, chips = _place()
        return [_rcopy(src.at[2 * chip[0] + chip[1]], dst.at[j], send, recv, j, (*chip, c))
                for j, chip in enumerate(chips)]

    def start(ins, outs, send, recv):
        for cp in copies(ins[0], outs[0], send, recv):
            cp.start()

    def finish(ins, outs, send, recv):
        for cp in copies(ins[0], outs[0], send, recv):
            cp.wait()

    return _Side([p], [jax.ShapeDtypeStruct((3,) + p.shape[1:], p.dtype)], 3, start, finish)


def scatter_rows(p, name):
    side = _scatter_side(p)

    def body(p_ref, o_ref, send, recv):
        side.start([p_ref], [o_ref], send, recv)
        side.finish([p_ref], [o_ref], send, recv)

    return pl.pallas_call(
        body, name=name, in_specs=[_ANY], out_specs=_ANY, out_shape=side.outs[0],
        scratch_shapes=[pltpu.SemaphoreType.DMA((3,)), pltpu.SemaphoreType.DMA((3,))],
    )(p)


def _swap_rows(g, name):
    def body(g_ref, o_ref, send_sems, recv_sems):
        x, y, c, _ = _place()
        cp = _rcopy(g_ref.at[:, 1 - c], o_ref, send_sems, recv_sems, 0, (x, y, 1 - c))
        cp.start()
        cp.wait()

    return pl.pallas_call(
        body, name=name, in_specs=[_ANY], out_specs=_ANY,
        out_shape=jax.ShapeDtypeStruct((g.shape[0],) + g.shape[2:], g.dtype),
        scratch_shapes=[pltpu.SemaphoreType.DMA((1,)), pltpu.SemaphoreType.DMA((1,))],
    )(g)


def _swap_whole(g, name):
    def body(g_ref, o_ref, send_sems, recv_sems):
        x, y, c, _ = _place()
        cp = _rcopy(g_ref, o_ref, send_sems, recv_sems, 0, (x, y, 1 - c))
        cp.start()
        cp.wait()

    return pl.pallas_call(
        body, name=name, in_specs=[_ANY], out_specs=_ANY, out_shape=jax.ShapeDtypeStruct(g.shape, g.dtype),
        scratch_shapes=[pltpu.SemaphoreType.DMA((1,)), pltpu.SemaphoreType.DMA((1,))],
    )(g)


def _ew_call(fn, ins, out_dtypes, name):
    shape = ins[0].shape
    n = shape[-1]
    ins2 = [a.reshape(-1, n) for a in ins]
    rows = ins2[0].shape[0]
    tr = _pick(rows, (256, 128, 64, 32, 16, 8)) if n <= 2048 else _pick(rows, (128, 64, 32, 16, 8))
    n_in = len(ins2)

    def body(*refs):
        outs = fn(*[r[...] for r in refs[:n_in]])
        for o, v in zip(refs[n_in:], outs):
            o[...] = v.astype(o.dtype)

    spec = pl.BlockSpec((tr, n), lambda i: (i, 0))
    res = pl.pallas_call(
        body, name=name, grid=(rows // tr,), in_specs=[spec] * n_in, out_specs=[spec] * len(out_dtypes),
        out_shape=[jax.ShapeDtypeStruct((rows, n), dt) for dt in out_dtypes],
        compiler_params=_cparams(("parallel",)),
    )(*ins2)
    return [r.reshape(shape) for r in res]


def all_reduce_small(buf, name):
    _, R, Ln = buf.shape

    def body(in_ref, out_ref, recv_ref, send_a, recv_a, send_b, recv_b):
        x, y, c = lax.axis_index("x"), lax.axis_index("y"), lax.axis_index("c")
        me = 4 * x + 2 * y + c
        peers = []
        for r in range(1, 8):
            px, py, pc = x ^ ((r >> 2) & 1), y ^ ((r >> 1) & 1), c ^ (r & 1)
            peers.append((r, (px, py, pc), 4 * px + 2 * py + pc))
        cps = [_rcopy(in_ref.at[idx], recv_ref.at[r], send_a, recv_a, r, to) for r, to, idx in peers]
        for cp in cps:
            cp.start()
        for cp in cps:
            cp.wait()
        acc = in_ref[me]
        for r in range(1, 8):
            acc = acc + recv_ref[r]
        out_ref[me] = acc
        cps = [_rcopy(out_ref.at[me], out_ref.at[me], send_b, recv_b, r, to) for r, to, idx in peers]
        for cp in cps:
            cp.start()
        for (r, to, idx), cp in zip(peers, cps):
            cp.wait_send()
            _rcopy(out_ref.at[idx], out_ref.at[idx], send_b, recv_b, r, to).wait_recv()

    vm = pl.BlockSpec(memory_space=pltpu.VMEM)
    return pl.pallas_call(
        body, name=name, in_specs=[vm], out_specs=vm,
        out_shape=jax.ShapeDtypeStruct(buf.shape, F32),
        scratch_shapes=[pltpu.VMEM(buf.shape, F32)] + [pltpu.SemaphoreType.DMA((8,))] * 4,
        compiler_params=pltpu.CompilerParams(vmem_limit_bytes=VMEM_LIMIT_BYTES),
    )(buf)


def _adamw_math(w, g, m, v):
    m = ADAM_B1 * m + (1.0 - ADAM_B1) * g
    v = ADAM_B2 * v + (1.0 - ADAM_B2) * (g * g)
    m_hat = m / (1.0 - ADAM_B1 ** ADAM_STEP)
    v_hat = v / (1.0 - ADAM_B2 ** ADAM_STEP)
    delta = -ADAM_LR * (m_hat / (jnp.sqrt(v_hat) + ADAM_EPS) + ADAM_WD * w)
    return [delta, m, v]


def adamw(w, g, m, v, name):
    return _ew_call(_adamw_math, [w, g, m, v], [F32, F32, F32], name)


_WEIGHTS = ['norm_w', 'w_in', 's5_lambda_re', 's5_lambda_im', 's5_log_step', 's5_b_re', 's5_b_im', 's5_c_re', 's5_c_im',
            's5_d', 's5_w_glu', 's5_b_glu', 'rg_conv_w', 'rg_conv_b', 'rg_w_a', 'rg_b_a', 'rg_w_x', 'rg_b_x', 'rg_lambda',
            'hg_lower_bounds', 'hg_norm_w', 'w_branch', 'w_out', 'final_norm_w']
_BIG = ('w_in', 's5_w_glu', 'w_branch', 'w_out')
_SMALL = [n for n in _WEIGHTS if n not in _BIG]
_LANES = 128
_N_DEV = 8


def _pack(arrs):
    flat = jnp.concatenate([a.reshape(-1) for a in arrs])
    unit = _N_DEV * 8 * _LANES
    total = -(-flat.shape[0] // unit) * unit
    flat = jnp.pad(flat, (0, total - flat.shape[0]))
    return flat.reshape(_N_DEV, total // (_N_DEV * _LANES), _LANES)


def _unpack(buf, shapes):
    flat = buf.reshape(-1)
    out, off = [], 0
    for s in shapes:
        n = math.prod(s)
        out.append(flat[off:off + n].reshape(s))
        off += n
    return out


def _step(a):
    x_idx, y_idx, c_idx = lax.axis_index("x"), lax.axis_index("y"), lax.axis_index("c")
    chip = 2 * x_idx + y_idx
    L = a["norm_w"].shape[0]
    W = a["s5_d"].shape[1]
    cw = a["rg_conv_w"]
    wc = cw.shape[2]
    placed = lax.dynamic_update_slice(jnp.zeros((L, RG_CONV, W), F32), cw, (0, 0, chip * wc))
    placed = placed * (c_idx == 0).astype(F32)
    conv_full = _unpack(all_reduce_small(_pack([placed]), "gather_conv"), [(L, RG_CONV, W)])[0]
    wts = {n: a[n] for n in _SMALL}
    wts["rg_conv_w"] = conv_full
    lbs, vjp_lbs = jax.vjp(_lower_bounds, a["hg_lower_bounds"])
    D = a["w_in"].shape[1]
    w_in = a["w_in"].astype(BF16).reshape(L, 2, D // 2, a["w_in"].shape[2])
    rest = [_pack_rest(a["s5_w_glu"][l].astype(BF16), a["w_branch"][l].astype(BF16), a["w_out"][l].astype(BF16))
            for l in range(L)]
    x = a["x"][0]
    land_w = gather_rows(w_in[0], "gather_w_in_first")
    land_r = gather_rows(rest[0], "gather_rest_first")
    saved = []
    for l in range(L):
        last = l + 1 == L
        h, vjp_rms = jax.vjp(_rms_bf16, x, a["norm_w"][l])
        w_l = _place_own(land_w, w_in[l])
        if last:
            z = _mm_nn(h, w_l, None, "w_in_fwd")
        else:
            z, land_w = _mm_nn(h, w_l, None, "w_in_fwd_gather", side=_gather_side(w_in[l + 1]))
        lw = {n: wts[n][l] for n in _LAYER_SMALL}
        lw["lbs"] = lbs[l]
        r_l = lax.dynamic_update_slice(land_r, rest[l][None], (chip, 0, 0, 0))
        x_next, vjp_tail, land_r = jax.vjp(functools.partial(_layer_tail, nxt=None if last else rest[l + 1]),
                                           z, x, lw, r_l, has_aux=True)
        saved.append((h, w_l, vjp_rms, vjp_tail))
        x = x_next
    rows, vjp_loss = jax.vjp(_loss_rows, x, a["final_norm_w"], a["loss_target"][0])
    loss = lax.psum(jnp.sum(rows), ("x", "y", "c"))
    dx, d_final, _ = vjp_loss(jnp.ones_like(rows))
    gw = {n: [None] * L for n in _LAYER_SMALL + ["lbs", "norm_w"]}
    pairs = {"w": [None] * L, "r": [None] * L}
    recvs = {"w": [None] * L, "r": [None] * L}

    def pair_sum(g, tag):
        got = _swap_rows(g, "reduce_" + tag + "_swap")
        mine = lax.dynamic_index_in_dim(g, c_idx, 1, keepdims=False)
        return _ew_call(lambda p, q: [p.astype(F32) + q.astype(F32)], [mine, got], [BF16], "reduce_" + tag + "_pair")[0]

    for l in reversed(range(L)):
        h, w_l, vjp_rms, vjp_tail = saved[l]
        dz, dx_res, dlw, d_rest = vjp_tail(dx)
        if l + 1 == L:
            dh = _mm_nt(dz, w_l, h.dtype, "w_in_bwd_a")
            dw = _mm_tn(h, dz, N_CHIPS, BF16, "w_in_bwd_w")
        else:
            dh, recvs["r"][l + 1] = _mm_nt(dz, w_l, h.dtype, "w_in_bwd_a_scatter", side=_scatter_side(pairs["r"][l + 1]))
            dw, recvs["w"][l + 1] = _mm_tn(h, dz, N_CHIPS, BF16, "w_in_bwd_w_scatter", side=_scatter_side(pairs["w"][l + 1]))
        dx_rms, gw["norm_w"][l] = vjp_rms(dh)
        dx = dx_rms + dx_res
        for n in _LAYER_SMALL + ["lbs"]:
            gw[n][l] = dlw[n]
        pairs["w"][l] = pair_sum(dw.reshape(N_CHIPS, 2, D // 2, dw.shape[2]), "w_in")
        pairs["r"][l] = pair_sum(d_rest, "rest")
    recvs["w"][0] = scatter_rows(pairs["w"][0], "reduce_w_in_scatter")
    recvs["r"][0] = scatter_rows(pairs["r"][0], "reduce_rest_scatter")
    south = c_idx == 0

    def finish(tag, name):
        halves = []
        for l in range(L):
            own = lax.dynamic_index_in_dim(pairs[tag][l], chip, 0, keepdims=False)
            r = recvs[tag][l]
            halves.append(_ew_call(
                lambda p, r0, r1, r2: [((p.astype(F32) + r0.astype(F32)) + r1.astype(F32)) + r2.astype(F32)],
                [own, r[0], r[1], r[2]], [F32], "reduce_" + name + "_sum")[0])
        gh = jnp.stack(halves, 0)
        other = _swap_whole(gh, "reduce_" + name + "_join")
        return jnp.stack([jnp.where(south, gh, other), jnp.where(south, other, gh)], 1)

    grads = {"w_in": finish("w", "w_in").reshape(a["w_in"].shape)}
    g_rest = _unpack_rest(finish("r", "rest"), W, D)
    grads["w_branch"] = jnp.stack(g_rest["w_branch"], 1)
    grads["w_out"] = g_rest["w_out"]
    grads["s5_w_glu"] = g_rest["s5_w_glu"]
    gx = dx
    (d_lb,) = vjp_lbs(jnp.stack(gw["lbs"], 0))
    gw = {n: jnp.stack(v, 0) for n, v in gw.items()}
    gw["hg_lower_bounds"] = d_lb
    gw["final_norm_w"] = d_final
    small_shapes = [gw[n].shape for n in _SMALL]
    red = _unpack(all_reduce_small(_pack([gw[n].astype(F32) for n in _SMALL]), "reduce_small"), small_shapes)
    for n, g in zip(_SMALL, red):
        grads[n] = g
    grads["rg_conv_w"] = lax.dynamic_slice_in_dim(grads["rg_conv_w"], chip * wc, wc, 2)
    delta, new_m, new_v = {}, {}, {}
    for n in _BIG:
        delta[n], new_m[n], new_v[n] = adamw(a[n], grads[n], a["m_" + n], a["v_" + n], "adamw_" + n)
    shapes = [a[n].shape for n in _SMALL]
    packed = [_pack([t[n] for n in _SMALL]) for t in
              (a, grads, {n: a["m_" + n] for n in _SMALL}, {n: a["v_" + n] for n in _SMALL})]
    for dst, buf in zip((delta, new_m, new_v), adamw(*packed, "adamw_small")):
        for n, t in zip(_SMALL, _unpack(buf, shapes)):
            dst[n] = t
    return (loss, gx[None], *[grads[n] for n in _WEIGHTS], *[delta[n] for n in _WEIGHTS],
            *[new_m[n] for n in _WEIGHTS], *[new_v[n] for n in _WEIGHTS])


_ARG_NAMES = ["x"] + _WEIGHTS + ["loss_target"] + ["m_" + n for n in _WEIGHTS] + ["v_" + n for n in _WEIGHTS]


def kernel(x, norm_w, w_in, s5_lambda_re, s5_lambda_im, s5_log_step, s5_b_re, s5_b_im, s5_c_re, s5_c_im, s5_d, s5_w_glu, s5_b_glu, rg_conv_w, rg_conv_b, rg_w_a, rg_b_a, rg_w_x, rg_b_x, rg_lambda, hg_lower_bounds, hg_norm_w, w_branch, w_out, final_norm_w, loss_target, m_norm_w, m_w_in, m_s5_lambda_re, m_s5_lambda_im, m_s5_log_step, m_s5_b_re, m_s5_b_im, m_s5_c_re, m_s5_c_im, m_s5_d, m_s5_w_glu, m_s5_b_glu, m_rg_conv_w, m_rg_conv_b, m_rg_w_a, m_rg_b_a, m_rg_w_x, m_rg_b_x, m_rg_lambda, m_hg_lower_bounds, m_hg_norm_w, m_w_branch, m_w_out, m_final_norm_w, v_norm_w, v_w_in, v_s5_lambda_re, v_s5_lambda_im, v_s5_log_step, v_s5_b_re, v_s5_b_im, v_s5_c_re, v_s5_c_im, v_s5_d, v_s5_w_glu, v_s5_b_glu, v_rg_conv_w, v_rg_conv_b, v_rg_w_a, v_rg_b_a, v_rg_w_x, v_rg_b_x, v_rg_lambda, v_hg_lower_bounds, v_hg_norm_w, v_w_branch, v_w_out, v_final_norm_w):
    vals = (x, norm_w, w_in, s5_lambda_re, s5_lambda_im, s5_log_step, s5_b_re, s5_b_im, s5_c_re, s5_c_im, s5_d, s5_w_glu, s5_b_glu, rg_conv_w, rg_conv_b, rg_w_a, rg_b_a, rg_w_x, rg_b_x, rg_lambda, hg_lower_bounds, hg_norm_w, w_branch, w_out, final_norm_w, loss_target, m_norm_w, m_w_in, m_s5_lambda_re, m_s5_lambda_im, m_s5_log_step, m_s5_b_re, m_s5_b_im, m_s5_c_re, m_s5_c_im, m_s5_d, m_s5_w_glu, m_s5_b_glu, m_rg_conv_w, m_rg_conv_b, m_rg_w_a, m_rg_b_a, m_rg_w_x, m_rg_b_x, m_rg_lambda, m_hg_lower_bounds, m_hg_norm_w, m_w_branch, m_w_out, m_final_norm_w, v_norm_w, v_w_in, v_s5_lambda_re, v_s5_lambda_im, v_s5_log_step, v_s5_b_re, v_s5_b_im, v_s5_c_re, v_s5_c_im, v_s5_d, v_s5_w_glu, v_s5_b_glu, v_rg_conv_w, v_rg_conv_b, v_rg_w_a, v_rg_b_a, v_rg_w_x, v_rg_b_x, v_rg_lambda, v_hg_lower_bounds, v_hg_norm_w, v_w_branch, v_w_out, v_final_norm_w)
    return _step(dict(zip(_ARG_NAMES, vals)))
```

```python
import functools
import math

import jax
import jax.numpy as jnp
from jax import lax
from jax.experimental import pallas as pl
from jax.experimental.pallas import tpu as pltpu

F32 = jnp.float32
BF16 = jnp.bfloat16
EPS = 1e-6
RG_C = 8.0
S5_GROUP = 16
S5_STATE = 64
RG_BLOCKS = 16
RG_CONV = 4
HG_HEADS = 8
N_CHIPS = 4
VMEM_LIMIT_BYTES = 56 * 1024 * 1024
NEG_BIG = -1e30

ADAM_LR = 0.001
ADAM_B1 = 0.9
ADAM_B2 = 0.999
ADAM_EPS = 1e-08
ADAM_WD = 0.01
ADAM_STEP = 10

MESH = pl.DeviceIdType.MESH


def _cparams(sem):
    return pltpu.CompilerParams(dimension_semantics=sem, vmem_limit_bytes=VMEM_LIMIT_BYTES)


_DOT_DIMS = {"nn": (((1,), (0,)), ((), ())), "nt": (((1,), (1,)), ((), ())), "tn": (((0,), (0,)), ((), ()))}


def _bdot_raw(a, b, form):
    return lax.dot_general(a.astype(BF16), b.astype(BF16), _DOT_DIMS[form], preferred_element_type=F32)


@functools.partial(jax.custom_vjp, nondiff_argnums=(2,))
def bdot(a, b, form):
    return _bdot_raw(a, b, form)


def _bdot_fwd(a, b, form):
    return _bdot_raw(a, b, form), (a, b)


def _bdot_bwd(form, res, g):
    a, b = res
    if form == "nn":
        da, db = _bdot_raw(g, b, "nt"), _bdot_raw(a, g, "tn")
    elif form == "nt":
        da, db = _bdot_raw(g, b, "nn"), _bdot_raw(g, a, "tn")
    else:
        da, db = _bdot_raw(b, g, "nt"), _bdot_raw(a, g, "nn")
    return da.astype(a.dtype), db.astype(b.dtype)


bdot.defvjp(_bdot_fwd, _bdot_bwd)


@functools.partial(jax.custom_vjp, nondiff_argnums=(1,))
def sroll(x, d):
    return pltpu.roll(x, d, 0)


def _sroll_fwd(x, d):
    return pltpu.roll(x, d, 0), None


def _sroll_bwd(d, _, g):
    return (pltpu.roll(g, g.shape[0] - d, 0),)


sroll.defvjp(_sroll_fwd, _sroll_bwd)


def _row_iota(n):
    return lax.broadcasted_iota(jnp.int32, (n, 1), 0)


def _last_row(x, ti):
    return jnp.sum(jnp.where(ti == x.shape[0] - 1, x, 0.0), axis=0, keepdims=True)


def _silu(x):
    return x * jax.nn.sigmoid(x)


class _Tiling:
    def __init__(self, CB, tT, nT, ncb=None, bmap=None):
        self.CB, self.tT, self.nT = CB, tT, nT
        self.ncb = CB if ncb is None else ncb
        self.bmap = (lambda cb, t: (t, cb)) if bmap is None else bmap


def _tiled_specs(params, views, carry_shapes, out_defs, tl, rev):
    nT = tl.nT
    tmap = (lambda t: nT - 1 - t) if rev else (lambda t: t)

    def rspec(col0, width):
        bw = width // tl.ncb
        off = col0 // bw

        def imap(cb, t):
            rb, cbk = tl.bmap(cb, tmap(t))
            return (rb, off + cbk)

        return pl.BlockSpec((tl.tT, bw), imap)

    p_specs = [pl.BlockSpec((None,) + p.shape[1:], lambda cb, t: (cb, 0, 0)) for p in params]
    r_specs = [rspec(c0, w) for (c0, w) in views]
    dr_specs = [rspec(0, w) for (_, w) in views]
    o_specs = [rspec(0, w) for (w, _) in out_defs]
    s_specs = [pl.BlockSpec((None, r, c), lambda cb, t: (tmap(t), 0, cb)) for (r, c) in carry_shapes]
    return p_specs, r_specs, dr_specs, o_specs, s_specs


def _tiled_fwd(f, name, params, rows, views, carry_shapes, out_defs, tl, side=None):
    T = rows[0].shape[0]
    CB, tT, nT = tl.CB, tl.tT, tl.nT
    n_p, n_r, n_o, n_c = len(params), len(rows), len(out_defs), len(carry_shapes)
    p_specs, r_specs, _, o_specs, s_specs = _tiled_specs(params, views, carry_shapes, out_defs, tl, False)
    x_ins, x_outs, x_sems = _side_parts(side)
    n_xi, n_xo = len(x_ins), len(x_outs)

    def body(*refs):
        i = 0
        p_refs = refs[i:i + n_p]; i += n_p
        r_refs = refs[i:i + n_r]; i += n_r
        xi_refs = refs[i:i + n_xi]; i += n_xi
        o_refs = refs[i:i + n_o]; i += n_o
        s_refs = refs[i:i + n_c]; i += n_c
        xo_refs = refs[i:i + n_xo]; i += n_xo
        c_refs = refs[i:i + n_c]
        cb, t = pl.program_id(0), pl.program_id(1)
        if side is not None:
            @pl.when((cb == 0) & (t == 0))
            def _():
                side.start(xi_refs, xo_refs, refs[-2], refs[-1])

        @pl.when(t == 0)
        def _():
            for c in c_refs:
                c[...] = jnp.zeros_like(c)

        carries = [c[...] for c in c_refs]
        for s, cv in zip(s_refs, carries):
            s[...] = cv
        outs, newc = f([p[...] for p in p_refs], [r[...].astype(F32) for r in r_refs], carries, t * tT)
        for o, v in zip(o_refs, outs):
            o[...] = v.astype(o.dtype)
        for c, v in zip(c_refs, newc):
            c[...] = v
        if side is not None:
            @pl.when((cb == CB - 1) & (t == nT - 1))
            def _():
                side.finish(xi_refs, xo_refs, refs[-2], refs[-1])

    out_shape = [jax.ShapeDtypeStruct((T, w), dt) for (w, dt) in out_defs]
    out_shape += [jax.ShapeDtypeStruct((nT, r, c * CB), F32) for (r, c) in carry_shapes]
    res = pl.pallas_call(
        body, name=name + "_fwd", grid=(CB, nT),
        in_specs=p_specs + r_specs + [_ANY] * n_xi, out_specs=o_specs + s_specs + [_ANY] * n_xo,
        out_shape=out_shape + x_outs,
        scratch_shapes=[pltpu.VMEM((r, c), F32) for (r, c) in carry_shapes] + x_sems,
        compiler_params=_cparams(("arbitrary", "arbitrary")),
    )(*params, *rows, *x_ins)
    return list(res[:n_o]), list(res[n_o:n_o + n_c]), list(res[n_o + n_c:])


def _tiled_bwd(f, name, params, rows, views, saved, douts, carry_shapes, out_defs, tl):
    T = rows[0].shape[0]
    CB, tT, nT = tl.CB, tl.tT, tl.nT
    n_p, n_r, n_o, n_c = len(params), len(rows), len(out_defs), len(carry_shapes)
    p_specs, r_specs, dr_specs, o_specs, s_specs = _tiled_specs(params, views, carry_shapes, out_defs, tl, True)
    out_dtypes = [dt for (_, dt) in out_defs]

    def body(*refs):
        i = 0
        p_refs = refs[i:i + n_p]; i += n_p
        r_refs = refs[i:i + n_r]; i += n_r
        s_refs = refs[i:i + n_c]; i += n_c
        g_refs = refs[i:i + n_o]; i += n_o
        dp_refs = refs[i:i + n_p]; i += n_p
        dr_refs = refs[i:i + n_r]; i += n_r
        dc_refs = refs[i:]
        t = pl.program_id(1)

        @pl.when(t == 0)
        def _():
            for c in dc_refs:
                c[...] = jnp.zeros_like(c)
            for d in dp_refs:
                d[...] = jnp.zeros_like(d)

        t0 = (nT - 1 - t) * tT

        def g(P, R, C):
            outs, newc = f(P, R, C, t0)
            return [o.astype(dt) for o, dt in zip(outs, out_dtypes)], list(newc)

        _, vjp = jax.vjp(g, [p[...] for p in p_refs], [r[...].astype(F32) for r in r_refs], [s[...] for s in s_refs])
        dP, dR, dC = vjp(([gr[...] for gr in g_refs], [c[...] for c in dc_refs]))
        for d, v in zip(dp_refs, dP):
            d[...] += v
        for d, v in zip(dr_refs, dR):
            d[...] = v.astype(d.dtype)
        for c, v in zip(dc_refs, dC):
            c[...] = v

    out_shape = [jax.ShapeDtypeStruct(p.shape, F32) for p in params]
    out_shape += [jax.ShapeDtypeStruct((T, w), r.dtype) for r, (_, w) in zip(rows, views)]
    res = pl.pallas_call(
        body, name=name + "_bwd", grid=(CB, nT),
        in_specs=p_specs + r_specs + s_specs + o_specs, out_specs=p_specs + dr_specs, out_shape=out_shape,
        scratch_shapes=[pltpu.VMEM((r, c), F32) for (r, c) in carry_shapes],
        compiler_params=_cparams(("arbitrary", "arbitrary")),
    )(*params, *rows, *saved, *douts)
    return list(res[:n_p]), list(res[n_p:])


def tiled_op(f, name, params, rows, carry_shapes, out_defs, CB, tT, tiling=None, side_of=None):
    arrs = [r[0] if isinstance(r, tuple) else r for r in rows]
    views = [(r[1], r[2]) if isinstance(r, tuple) else (0, r.shape[1]) for r in rows]
    tl = tiling if tiling is not None else _Tiling(CB, tT, arrs[0].shape[0] // tT)
    make_side, side_arrs = side_of if side_of is not None else (None, [])

    def run_fwd(params, arrs, side_arrs):
        side = make_side(*side_arrs) if make_side is not None else None
        return _tiled_fwd(f, name, params, arrs, views, carry_shapes, out_defs, tl, side)

    @jax.custom_vjp
    def op(params, arrs, side_arrs):
        outs, _, extra = run_fwd(params, arrs, side_arrs)
        return outs, extra

    def op_fwd(params, arrs, side_arrs):
        outs, saved, extra = run_fwd(params, arrs, side_arrs)
        return (outs, extra), (params, arrs, saved, side_arrs)

    def op_bwd(res, cts):
        params, arrs, saved, side_arrs = res
        douts, _ = cts
        dP, dR = _tiled_bwd(f, name, params, arrs, views, saved, list(douts), carry_shapes, out_defs, tl)
        dR = [d if w == a.shape[1] else jnp.pad(d, ((0, 0), (c0, a.shape[1] - c0 - w)))
              for d, a, (c0, w) in zip(dR, arrs, views)]
        return dP, dR, [jnp.zeros_like(s) for s in side_arrs]

    op.defvjp(op_fwd, op_bwd)
    outs, extra = op(list(params), arrs, list(side_arrs))
    return outs if side_of is None else (outs, extra)


def _pick(n, pref):
    for t in pref:
        if n % t == 0:
            return t
    return n


class _Side:
    def __init__(self, ins, outs, n_sems, start, finish):
        self.ins, self.outs, self.n_sems, self.start, self.finish = ins, outs, n_sems, start, finish


def _side_parts(side):
    if side is None:
        return [], [], []
    sems = [pltpu.SemaphoreType.DMA((side.n_sems,)), pltpu.SemaphoreType.DMA((side.n_sems,))]
    return list(side.ins), list(side.outs), sems


def _mm_nn(a, w, res, name, out_dtype=F32, side=None):
    M, K = a.shape
    S, _, Ns = w.shape
    tm = _pick(M, (1024, 512, 256, 128))
    tn = _pick(Ns, (512, 256, 128))
    nps = Ns // tn
    has_res = res is not None
    n_main = 3 if has_res else 2
    s_ins, s_outs, s_sems = _side_parts(side)
    ni, nj = M // tm, S * nps

    def body(*refs):
        a_ref, w_ref = refs[0], refs[1]
        o_ref = refs[n_main + len(s_ins)]
        if side is not None:
            si = refs[n_main:n_main + len(s_ins)]
            so = refs[n_main + len(s_ins) + 1:n_main + len(s_ins) + 1 + len(s_outs)]
            send, recv = refs[-2], refs[-1]
            i, j = pl.program_id(0), pl.program_id(1)

            @pl.when((i == 0) & (j == 0))
            def _():
                side.start(si, so, send, recv)

        acc = _bdot_raw(a_ref[...], w_ref[...], "nn")
        if has_res:
            acc = acc + refs[2][...]
        o_ref[...] = acc.astype(o_ref.dtype)
        if side is not None:
            @pl.when((i == ni - 1) & (j == nj - 1))
            def _():
                side.finish(si, so, send, recv)

    in_specs = [pl.BlockSpec((tm, K), lambda i, j: (i, 0)),
                pl.BlockSpec((None, K, tn), lambda i, j: (j // nps, 0, j % nps))]
    args = [a, w]
    if has_res:
        in_specs.append(pl.BlockSpec((tm, tn), lambda i, j: (i, j)))
        args.append(res)
    out = pl.pallas_call(
        body, name=name, grid=(ni, nj), in_specs=in_specs + [_ANY] * len(s_ins),
        out_specs=[pl.BlockSpec((tm, tn), lambda i, j: (i, j))] + [_ANY] * len(s_outs),
        out_shape=[jax.ShapeDtypeStruct((M, S * Ns), out_dtype)] + s_outs,
        scratch_shapes=s_sems,
        compiler_params=_cparams(("arbitrary", "arbitrary")),
    )(*args, *s_ins)
    return out[0] if side is None else out


def _mm_nt(g, w, out_dtype, name, side=None):
    M, N = g.shape
    S, K, Ns = w.shape
    tm = _pick(M, (1024, 512, 256, 128))
    tk = _pick(K, (1024, 512, 256, 128))
    tn = _pick(Ns, (1792, 1024, 512, 256, 128))
    nps = Ns // tn
    nn = S * nps
    ni, nk = M // tm, K // tk
    s_ins, s_outs, s_sems = _side_parts(side)

    def body(*refs):
        g_ref, w_ref = refs[0], refs[1]
        o_ref = refs[2 + len(s_ins)]
        acc_ref = refs[3 + len(s_ins) + len(s_outs)]
        i, k, n = pl.program_id(0), pl.program_id(1), pl.program_id(2)
        if side is not None:
            si = refs[2:2 + len(s_ins)]
            so = refs[3 + len(s_ins):3 + len(s_ins) + len(s_outs)]

            @pl.when((i == 0) & (k == 0) & (n == 0))
            def _():
                side.start(si, so, refs[-2], refs[-1])

        @pl.when(n == 0)
        def _():
            acc_ref[...] = jnp.zeros_like(acc_ref)

        acc_ref[...] += _bdot_raw(g_ref[...], w_ref[...], "nt")

        @pl.when(n == nn - 1)
        def _():
            o_ref[...] = acc_ref[...].astype(o_ref.dtype)

        if side is not None:
            @pl.when((i == ni - 1) & (k == nk - 1) & (n == nn - 1))
            def _():
                side.finish(si, so, refs[-2], refs[-1])

    out = pl.pallas_call(
        body, name=name, grid=(ni, nk, nn),
        in_specs=[pl.BlockSpec((tm, tn), lambda i, k, n: (i, n)),
                  pl.BlockSpec((None, tk, tn), lambda i, k, n: (n // nps, k, n % nps))] + [_ANY] * len(s_ins),
        out_specs=[pl.BlockSpec((tm, tk), lambda i, k, n: (i, k))] + [_ANY] * len(s_outs),
        out_shape=[jax.ShapeDtypeStruct((M, K), out_dtype)] + s_outs,
        scratch_shapes=[pltpu.VMEM((tm, tk), F32)] + s_sems,
        compiler_params=_cparams(("arbitrary", "arbitrary", "arbitrary")),
    )(g, w, *s_ins)
    return out[0] if side is None else out


def _mm_tn(a, g, S, out_dtype, name, side=None):
    T, K = a.shape
    N = g.shape[1]
    Ns = N // S
    tk = _pick(K, (2048, 1024, 512, 256, 128))
    tn = _pick(Ns, (1024, 896, 512, 256, 128))
    tt = _pick(T, (1024, 512, 256, 128))
    nps = Ns // tn
    nt = T // tt
    nk, nj = K // tk, S * nps
    a_t = a.astype(BF16).T
    s_ins, s_outs, s_sems = _side_parts(side)

    def body(*refs):
        a_ref, g_ref = refs[0], refs[1]
        o_ref = refs[2 + len(s_ins)]
        acc_ref = refs[3 + len(s_ins) + len(s_outs)]
        k, j, t = pl.program_id(0), pl.program_id(1), pl.program_id(2)
        if side is not None:
            si = refs[2:2 + len(s_ins)]
            so = refs[3 + len(s_ins):3 + len(s_ins) + len(s_outs)]
            send, recv = refs[-2], refs[-1]

            @pl.when((k == 0) & (j == 0) & (t == 0))
            def _():
                side.start(si, so, send, recv)

        @pl.when(t == 0)
        def _():
            acc_ref[...] = jnp.zeros_like(acc_ref)

        acc_ref[...] += _bdot_raw(a_ref[...], g_ref[...], "nn")

        @pl.when(t == nt - 1)
        def _():
            o_ref[...] = acc_ref[...].astype(o_ref.dtype)

        if side is not None:
            @pl.when((k == nk - 1) & (j == nj - 1) & (t == nt - 1))
            def _():
                side.finish(si, so, send, recv)

    out = pl.pallas_call(
        body, name=name, grid=(nk, nj, nt),
        in_specs=[pl.BlockSpec((tk, tt), lambda k, j, t: (k, t)),
                  pl.BlockSpec((tt, tn), lambda k, j, t: (t, j))] + [_ANY] * len(s_ins),
        out_specs=[pl.BlockSpec((None, tk, tn), lambda k, j, t: (j // nps, k, j % nps))] + [_ANY] * len(s_outs),
        out_shape=[jax.ShapeDtypeStruct((S, K, Ns), out_dtype)] + s_outs,
        scratch_shapes=[pltpu.VMEM((tk, tn), F32)] + s_sems,
        compiler_params=_cparams(("arbitrary", "arbitrary", "arbitrary")),
    )(a_t, g, *s_ins)
    return out[0] if side is None else out


def linear(a, w, name, res=None):
    @jax.custom_vjp
    def op(a, w, res):
        return _mm_nn(a, w, res, name + "_fwd")

    def op_fwd(a, w, res):
        return _mm_nn(a, w, res, name + "_fwd"), (a, w)

    def op_bwd(saved, g):
        a, w = saved
        da = _mm_nt(g, w, a.dtype, name + "_bwd_a")
        dw = _mm_tn(a, g, w.shape[0], w.dtype, name + "_bwd_w")
        return da, dw, (None if res is None else g)

    op.defvjp(op_fwd, op_bwd)
    return op(a, w, res)


def _rms_tile(params, rows, carries, t0):
    (w,), (x,) = params, rows
    y = x * lax.rsqrt(jnp.mean(x * x, axis=-1, keepdims=True) + EPS) * w
    return [y], []


def _s5_tile(params, rows, carries, t0):
    t_m, s_re, s_im, r_re, r_im, apow_re, apow_im, d = params
    (u,) = rows
    n = u.shape[0]
    ti = _row_iota(n)
    pi = _row_iota(apow_re.shape[0])
    x_re, x_im = bdot(u, s_re, "nn"), bdot(u, s_im, "nn")
    k = 0
    while (1 << k) < n:
        sh = 1 << k
        p_re, p_im = _row_sel(apow_re, pi, k), _row_sel(apow_im, pi, k)
        q_re, q_im = sroll(x_re, sh), sroll(x_im, sh)
        m = ti >= sh
        x_re, x_im = (x_re + jnp.where(m, p_re * q_re - p_im * q_im, 0.0),
                      x_im + jnp.where(m, p_re * q_im + p_im * q_re, 0.0))
        k += 1
    x_re = jnp.where(ti >= 1, sroll(x_re, 1), 0.0)
    x_im = jnp.where(ti >= 1, sroll(x_im, 1), 0.0)
    y = bdot(u, t_m, "nn") + bdot(x_re, r_re, "nn") + bdot(x_im, r_im, "nn") + d * u
    return [jax.nn.gelu(y)], []


def _glu_tile(params, rows, carries, t0):
    (b,), (y, zg, ga) = params, rows
    return [y * jax.nn.sigmoid(zg + b) * _silu(ga)], []


def _neg_expm1(z):
    small = -(z * (1.0 + z * (0.5 + z * (1.0 / 6.0))))
    return jnp.where(z > -0.01, small, 1.0 - jnp.exp(z))


def _rg_tile(params, rows, carries, t0):
    conv_w, conv_b, w_a, b_a, w_x, b_x, sp = params
    x, gate = rows
    x_prev, h_prev = carries
    tT = x.shape[0]
    ti = _row_iota(tT)
    ci = _row_iota(RG_CONV)
    xc = _row_sel(conv_w, ci, RG_CONV - 1) * x + conv_b
    for k in range(1, RG_CONV):
        xs = jnp.where(ti >= k, sroll(x, k), sroll(x_prev, k))
        xc = xc + _row_sel(conv_w, ci, RG_CONV - 1 - k) * xs
    r = jax.nn.sigmoid(bdot(xc, w_a, "nn") + b_a)
    i = jax.nn.sigmoid(bdot(xc, w_x, "nn") + b_x)
    log_a = -RG_C * r * sp
    a = jnp.exp(log_a)
    mult = jnp.sqrt(_neg_expm1(2.0 * log_a))
    mult = jnp.where(ti + t0 == 0, 1.0, mult)
    b = mult * (i * xc)
    b = b + jnp.where(ti == 0, a * h_prev, 0.0)
    k = 1
    while k < tT:
        m = ti >= k
        b = b + jnp.where(m, a * sroll(b, k), 0.0)
        a = jnp.where(m, a * sroll(a, k), a)
        k *= 2
    return [b * _silu(gate)], [x, _last_row(b, ti)]


def _hg_tile(params, rows, carries, t0):
    lb, nw = params
    q, fl, v, gate = rows
    (st,) = carries
    tT = q.shape[0]
    ti = _row_iota(tT)
    qs = _silu(q)
    f = lb + (1.0 - lb) * jax.nn.sigmoid(fl)
    kk = 1.0 - f
    G = jnp.log(f)
    k = 1
    while k < tT:
        G = G + jnp.where(ti >= k, sroll(G, k), 0.0)
        k *= 2
    inter = bdot(qs * jnp.exp(G), st, "nt")
    tr = lax.broadcasted_iota(jnp.int32, (tT, tT), 0)
    sc = lax.broadcasted_iota(jnp.int32, (tT, tT), 1)
    attn = jnp.zeros((tT, tT), F32)
    blk = tT
    while blk > 8:
        sub = blk // 4
        for j in range(1, 4):
            ref = jnp.zeros_like(G)
            for b in range(tT // blk):
                row = _row_sel(G, ti, b * blk + sub * j - 1)
                ref = ref + jnp.where(_div2(ti, blk) == b, row, 0.0)
            tmask = _div2(_mod2(ti, blk), sub) == j
            smask = _mod2(ti, blk) < sub * j
            qt = qs * jnp.exp(jnp.where(tmask, G - ref, NEG_BIG))
            kt = kk * jnp.exp(jnp.where(smask, ref - G, NEG_BIG))
            aj = bdot(qt, kt, "nt")
            attn = attn + jnp.where(_div2(tr, blk) == _div2(sc, blk), aj, 0.0)
        blk = sub
    intra = bdot(attn, v, "nn")
    for d in range(blk):
        if d == 0:
            kd, gd, vd = kk, G, v
        else:
            kd, gd, vd = sroll(kk, d), sroll(G, d), sroll(v, d)
        m = _mod2(ti, blk) >= d
        w = jnp.sum(qs * kd * jnp.exp(jnp.where(m, G - gd, NEG_BIG)), axis=1, keepdims=True)
        intra = intra + w * vd
    o = inter + intra
    g_last = _last_row(G, ti)
    k_dec = kk * jnp.exp(g_last - G)
    st_new = st * jnp.exp(g_last) + bdot(v, k_dec, "tn")
    o = o * lax.rsqrt(jnp.mean(o * o, axis=-1, keepdims=True) + EPS) * nw
    return [o * _silu(gate)], [st_new]


def _row_sel(x, ti, r):
    return jnp.sum(jnp.where(ti == r, x, 0.0), axis=0, keepdims=True)


def _div2(i, p):
    return lax.shift_right_logical(i, jnp.int32(p.bit_length() - 1))


def _mod2(i, p):
    return lax.bitwise_and(i, jnp.int32(p - 1))


def _merge_tile(params, rows, carries, t0):
    b0, b1, b2, g0, g1, g2 = rows
    m = jax.nn.sigmoid(g0) * b0 + jax.nn.sigmoid(g1) * b1 + jax.nn.sigmoid(g2) * b2
    return [m], []


def _loss_tile(params, rows, carries, t0):
    (w,), (x, tgt) = params, rows
    y = x * lax.rsqrt(jnp.mean(x * x, axis=-1, keepdims=True) + EPS) * w
    e = y - tgt
    return [0.5 * jnp.mean(e * e, axis=-1, keepdims=True)], []


def _block_diag(w, cb):
    n, i, j = w.shape
    g = n // cb
    w4 = w.reshape(cb, g, i, j)
    eye = jnp.eye(g, dtype=w.dtype)
    return jnp.einsum("cgij,gk->cgikj", w4, eye).reshape(cb, g * i, g * j)


def _s5_params(lam_re, lam_im, log_step, b_re, b_im, c_re, c_im, d, levels):
    G, P = lam_re.shape
    step = jnp.exp(log_step)[:, None]
    mag = jnp.exp(lam_re * step)
    ang = lam_im * step
    abar_re = mag * jnp.cos(ang)
    abar_im = mag * jnp.sin(ang)
    num_re = abar_re - 1.0
    num_im = abar_im
    den = lam_re * lam_re + lam_im * lam_im
    coef_re = (num_re * lam_re + num_im * lam_im) / den
    coef_im = (num_im * lam_re - num_re * lam_im) / den
    bbar_re = coef_re[..., None] * b_re - coef_im[..., None] * b_im
    bbar_im = coef_re[..., None] * b_im + coef_im[..., None] * b_re
    H = b_re.shape[2]
    Lc = S5_LC
    hi = lax.Precision.HIGHEST

    def powers(ks):
        ks = jnp.asarray(ks, F32)[:, None, None]
        m = jnp.exp(ks * (lam_re * step))
        return m * jnp.cos(ks * ang), m * jnp.sin(ks * ang)

    pw_re, pw_im = powers(list(range(Lc + 1)))
    ab_re = pw_re[..., None] * bbar_re - pw_im[..., None] * bbar_im
    ab_im = pw_re[..., None] * bbar_im + pw_im[..., None] * bbar_re
    kern = (jnp.einsum("gap,kgph->gkha", c_re, ab_re[:Lc], precision=hi)
            - jnp.einsum("gap,kgph->gkha", c_im, ab_im[:Lc], precision=hi))
    kk = jnp.arange(Lc)[:, None, None]
    jj = jnp.arange(Lc)[None, :, None]
    ii = jnp.arange(Lc)[None, None, :]
    place = (ii - jj == kk).astype(F32)
    t_m = jnp.einsum("gkha,kji->gjhia", kern, place, precision=hi).reshape(G, Lc * H, Lc * H)
    s_re = ab_re[:Lc][::-1].transpose(1, 0, 3, 2).reshape(G, Lc * H, P)
    s_im = ab_im[:Lc][::-1].transpose(1, 0, 3, 2).reshape(G, Lc * H, P)
    m_re = c_re[None] * pw_re[1:, :, None, :] - c_im[None] * pw_im[1:, :, None, :]
    m_im = c_re[None] * pw_im[1:, :, None, :] + c_im[None] * pw_re[1:, :, None, :]
    r_re = m_re.transpose(1, 3, 0, 2).reshape(G, P, Lc * H)
    r_im = -m_im.transpose(1, 3, 0, 2).reshape(G, P, Lc * H)
    rows = max(8, levels)
    ap_re, ap_im = powers([Lc * (1 << k) for k in range(levels)] + [0] * (rows - levels))
    dd = jnp.tile(d.reshape(G, 1, H), (1, 1, Lc))
    return [t_m, s_re, s_im, r_re, r_im, ap_re.transpose(1, 0, 2), ap_im.transpose(1, 0, 2), dd]


_LANE = 128


def _lane_perm_matrix(lc, h):
    n = lc * _LANE
    src = jnp.arange(n).reshape(lc, _LANE // h, h).transpose(1, 0, 2).reshape(n)
    return (jnp.arange(n)[:, None] == src[None, :]).astype(BF16)


def _lane_perm(x, p, p_t, name):
    @jax.custom_vjp
    def op(x):
        return _mm_nn(x, p[None], None, name, BF16)

    def op_fwd(x):
        return _mm_nn(x, p[None], None, name, BF16), None

    def op_bwd(_, g):
        return (_mm_nn(g, p_t[None], None, name + "_t", x.dtype),)

    op.defvjp(op_fwd, op_bwd)
    return op(x)


def _to_chunks(u, lc, h, perm):
    t, w = u.shape
    x = u.reshape(t // lc, lc, w // _LANE, _LANE).transpose(2, 0, 1, 3).reshape((w // _LANE) * (t // lc), lc * _LANE)
    return _lane_perm(x, perm, perm.T, "s5_to_chunks")


def _from_chunks(y, lc, w, perm):
    tiles = w // _LANE
    n_chunks = y.shape[0] // tiles
    x = _lane_perm(y, perm.T, perm, "s5_from_chunks")
    return x.reshape(tiles, n_chunks, lc, _LANE).transpose(1, 2, 0, 3).reshape(n_chunks * lc, w)


def _vec(v, cb):
    return v.reshape(cb, 1, -1)


S5_LC = 16
RG_CB, RG_TT = 4, 256
HG_TT = 128
ROW_TT = 256
MERGE_TT = 128


_LAYER_SMALL = ['s5_lambda_re', 's5_lambda_im', 's5_log_step', 's5_b_re', 's5_b_im', 's5_c_re', 's5_c_im', 's5_d',
                's5_b_glu', 'rg_conv_w', 'rg_conv_b', 'rg_w_a', 'rg_b_a', 'rg_w_x', 'rg_b_x', 'rg_lambda', 'hg_norm_w']


def _lower_bounds(hg_lower_bounds):
    lb_sm = jax.nn.softmax(hg_lower_bounds, axis=0)
    return jnp.cumsum(lb_sm, axis=0) - lb_sm[0]


def _rms_bf16(x, w):
    T, D = x.shape
    return tiled_op(_rms_tile, "rms", [w.reshape(1, 1, D)], [x], [], [(D, BF16)], 1, min(ROW_TT, T))[0]


def _loss_rows(x, w, target):
    T, D = x.shape
    return tiled_op(_loss_tile, "loss", [w.reshape(1, 1, D)], [x, target], [], [(1, F32)], 1, min(ROW_TT, T))[0]


def _layer_tail(z, x, lw, rest, nxt):
    T, D = x.shape
    lw = dict(lw, **_unpack_rest(rest, lw["s5_d"].shape[0], D))
    W = lw["s5_d"].shape[0]
    row_tt = min(ROW_TT, T)
    rg_tt, hg_tt = min(RG_TT, T), min(HG_TT, T)
    n_chunks = T // S5_LC
    gpt = _LANE // S5_GROUP
    perm = _lane_perm_matrix(S5_LC, S5_GROUP)
    s5_tiling = _Tiling(W // S5_GROUP, n_chunks, 1, ncb=gpt, bmap=lambda cb, t: (cb // gpt, cb % gpt))
    g_a, x_b, g_b, q_c, f_c, i_c, g_c = [(z, k * W, W) for k in range(1, 8)]
    gl = [(z, 8 * W + n * D, D) for n in range(3)]
    s5p = _s5_params(lw["s5_lambda_re"], lw["s5_lambda_im"], lw["s5_log_step"], lw["s5_b_re"], lw["s5_b_im"],
                     lw["s5_c_re"], lw["s5_c_im"], lw["s5_d"], int(math.log2(n_chunks)))
    (y1c,) = tiled_op(_s5_tile, "s5", s5p, [_to_chunks(z[:, :W], S5_LC, S5_GROUP, perm)], [],
                      [(S5_LC * _LANE, BF16)], W // S5_GROUP, n_chunks, tiling=s5_tiling)
    y1 = _from_chunks(y1c, S5_LC, W, perm)
    zg = linear(y1, lw["s5_w_glu"].reshape(1, W, W), "w_glu")
    (y_a,) = tiled_op(_glu_tile, "glu", [lw["s5_b_glu"].reshape(1, 1, W)], [y1, zg, g_a], [], [(W, BF16)], 1, row_tt)
    rgp = [lw["rg_conv_w"].reshape(RG_CONV, RG_CB, W // RG_CB).transpose(1, 0, 2),
           _vec(lw["rg_conv_b"], RG_CB), _block_diag(lw["rg_w_a"], RG_CB), _vec(lw["rg_b_a"], RG_CB),
           _block_diag(lw["rg_w_x"], RG_CB), _vec(lw["rg_b_x"], RG_CB), _vec(jax.nn.softplus(-lw["rg_lambda"]), RG_CB)]
    rc = W // RG_CB
    (y_b,) = tiled_op(_rg_tile, "rg", rgp, [x_b, g_b], [(rg_tt, rc), (1, rc)], [(W, BF16)], RG_CB, rg_tt)
    dk = W // HG_HEADS
    hgp = [_vec(lw["lbs"], HG_HEADS), _vec(lw["hg_norm_w"], HG_HEADS)]
    land = None
    if nxt is None:
        (y_c,) = tiled_op(_hg_tile, "hg", hgp, [q_c, f_c, i_c, g_c], [(dk, dk)], [(W, BF16)], HG_HEADS, hg_tt)
    else:
        (y_c,), (land,) = tiled_op(_hg_tile, "hg_gather", hgp, [q_c, f_c, i_c, g_c], [(dk, dk)], [(W, BF16)],
                                   HG_HEADS, hg_tt, side_of=(_gather_side, [nxt]))
    br = [linear(y, lw["w_branch"][n], "w_br") for n, y in enumerate((y_a, y_b, y_c))]
    (mg,) = tiled_op(_merge_tile, "merge", [], br + gl, [], [(D, BF16)], 1, min(MERGE_TT, T))
    return linear(mg, lw["w_out"].reshape(1, D, D), "w_out", res=x), land


def _pack_rest(w_glu, w_branch, w_out):
    c = w_branch.shape[-1]
    parts = [w_branch[n].reshape(2, -1, c) for n in range(3)] + [w_out.reshape(2, -1, c), w_glu.reshape(2, -1, c)]
    return jnp.concatenate(parts, 1)


def _rest_rows(W, D):
    c = D // N_CHIPS
    return [W // 2] * 3 + [(D // N_CHIPS) * D // (2 * c), (W // N_CHIPS) * W // (2 * c)]


def _unpack_rest(p, W, D):
    lead = p.shape[:-3]
    out, off = [], 0
    for r in _rest_rows(W, D):
        out.append(p[..., off:off + r, :])
        off += r
    br = [b.reshape(lead + (W, D // N_CHIPS)) for b in out[:3]]
    w_out = out[3].reshape(lead + (D // N_CHIPS, D))
    w_glu = out[4].reshape(lead + (W // N_CHIPS, W))
    return {"w_branch": br, "w_out": w_out, "s5_w_glu": w_glu}


_ANY = pl.BlockSpec(memory_space=pl.ANY)


def _place():
    x, y, c = lax.axis_index("x"), lax.axis_index("y"), lax.axis_index("c")
    chips = [(1 - x, y), (x, 1 - y), (1 - x, 1 - y)]
    return x, y, c, chips


def _rcopy(src, dst, send_sems, recv_sems, k, to):
    return pltpu.make_async_remote_copy(src_ref=src, dst_ref=dst, send_sem=send_sems.at[k], recv_sem=recv_sems.at[k],
                                        device_id=to, device_id_type=MESH)


def _gather_side(w):
    shape = (N_CHIPS,) + w.shape

    def first(src, land, send, recv):
        x, y, c, chips = _place()
        me = 2 * x + y
        return [_rcopy(src.at[c], land.at[me, c], send, recv, j, (*chip, c)) for j, chip in enumerate(chips)]

    def start(ins, outs, send, recv):
        for cp in first(ins[0], outs[0], send, recv):
            cp.start()

    def finish(ins, outs, send, recv):
        land = outs[0]
        x, y, c, chips = _place()
        sibling = (x, y, 1 - c)
        passed = []
        for j, chip in enumerate(chips):
            s = 2 * chip[0] + chip[1]
            _rcopy(land.at[s, c], land.at[s, c], send, recv, j, (*chip, c)).wait_recv()
            cp = _rcopy(land.at[s, c], land.at[s, c], send, recv, 3 + j, sibling)
            cp.start()
            passed.append(cp)
        for j, chip in enumerate(chips):
            s = 2 * chip[0] + chip[1]
            _rcopy(land.at[s, 1 - c], land.at[s, 1 - c], send, recv, 3 + j, sibling).wait_recv()
        for cp in first(ins[0], land, send, recv) + passed:
            cp.wait_send()

    return _Side([w], [jax.ShapeDtypeStruct(shape, w.dtype)], 6, start, finish)


def _place_own(land, w):
    chip = 2 * lax.axis_index("x") + lax.axis_index("y")
    full = lax.dynamic_update_slice(land, w[None], (chip, 0, 0, 0))
    return full.reshape(N_CHIPS, 2 * w.shape[1], w.shape[2])


def gather_rows(w, name):
    side = _gather_side(w)

    def body(w_ref, o_ref, send, recv):
        side.start([w_ref], [o_ref], send, recv)
        side.finish([w_ref], [o_ref], send, recv)

    return pl.pallas_call(
        body, name=name, in_specs=[_ANY], out_specs=_ANY, out_shape=side.outs[0],
        scratch_shapes=[pltpu.SemaphoreType.DMA((6,)), pltpu.SemaphoreType.DMA((6,))],
    )(w)


def _scatter_side(p):
    def copies(src, dst, send, recv):
        x, y, c, chips = _place()
        return [_rcopy(src.at[2 * chip[0] + chip[1]], dst.at[j], send, recv, j, (*chip, c))
                for j, chip in enumerate(chips)]

    def start(ins, outs, send, recv):
        for cp in copies(ins[0], outs[0], send, recv):
            cp.start()

    def finish(ins, outs, send, recv):
        for cp in copies(ins[0], outs[0], send, recv):
            cp.wait()

    return _Side([p], [jax.ShapeDtypeStruct((3,) + p.shape[1:], p.dtype)], 3, start, finish)


def scatter_rows(p, name):
    side = _scatter_side(p)

    def body(p_ref, o_ref, send, recv):
        side.start([p_ref], [o_ref], send, recv)
        side.finish([p_ref], [o_ref], send, recv)

    return pl.pallas_call(
        body, name=name, in_specs=[_ANY], out_specs=_ANY, out_shape=side.outs[0],
        scratch_shapes=[pltpu.SemaphoreType.DMA((3,)), pltpu.SemaphoreType.DMA((3,))],
    )(p)


def _swap_rows(g, name):
    def body(g_ref, o_ref, send_sems, recv_sems):
        x, y, c, _ = _place()
        cp = _rcopy(g_ref.at[:, 1 - c], o_ref, send_sems, recv_sems, 0, (x, y, 1 - c))
        cp.start()
        cp.wait()

    return pl.pallas_call(
        body, name=name, in_specs=[_ANY], out_specs=_ANY,
        out_shape=jax.ShapeDtypeStruct((g.shape[0],) + g.shape[2:], g.dtype),
        scratch_shapes=[pltpu.SemaphoreType.DMA((1,)), pltpu.SemaphoreType.DMA((1,))],
    )(g)


def _swap_whole(g, name):
    def body(g_ref, o_ref, send_sems, recv_sems):
        x, y, c, _ = _place()
        cp = _rcopy(g_ref, o_ref, send_sems, recv_sems, 0, (x, y, 1 - c))
        cp.start()
        cp.wait()

    return pl.pallas_call(
        body, name=name, in_specs=[_ANY], out_specs=_ANY, out_shape=jax.ShapeDtypeStruct(g.shape, g.dtype),
        scratch_shapes=[pltpu.SemaphoreType.DMA((1,)), pltpu.SemaphoreType.DMA((1,))],
    )(g)


def _ew_call(fn, ins, out_dtypes, name):
    shape = ins[0].shape
    n = shape[-1]
    ins2 = [a.reshape(-1, n) for a in ins]
    rows = ins2[0].shape[0]
    tr = _pick(rows, (256, 128, 64, 32, 16, 8)) if n <= 2048 else _pick(rows, (128, 64, 32, 16, 8))
    n_in = len(ins2)

    def body(*refs):
        outs = fn(*[r[...] for r in refs[:n_in]])
        for o, v in zip(refs[n_in:], outs):
            o[...] = v.astype(o.dtype)

    spec = pl.BlockSpec((tr, n), lambda i: (i, 0))
    res = pl.pallas_call(
        body, name=name, grid=(rows // tr,), in_specs=[spec] * n_in, out_specs=[spec] * len(out_dtypes),
        out_shape=[jax.ShapeDtypeStruct((rows, n), dt) for dt in out_dtypes],
        compiler_params=_cparams(("parallel",)),
    )(*ins2)
    return [r.reshape(shape) for r in res]


def all_reduce_small(buf, name):
    _, R, Ln = buf.shape

    def body(in_ref, out_ref, recv_ref, send_a, recv_a, send_b, recv_b):
        x, y, c = lax.axis_index("x"), lax.axis_index("y"), lax.axis_index("c")
        me = 4 * x + 2 * y + c
        peers = []
        for r in range(1, 8):
            px, py, pc = x ^ ((r >> 2) & 1), y ^ ((r >> 1) & 1), c ^ (r & 1)
            peers.append((r, (px, py, pc), 4 * px + 2 * py + pc))
        cps = [_rcopy(in_ref.at[idx], recv_ref.at[r], send_a, recv_a, r, to) for r, to, idx in peers]
        for cp in cps:
            cp.start()
        for cp in cps:
            cp.wait()
        acc = in_ref[me]
        for r in range(1, 8):
            acc = acc + recv_ref[r]
        out_ref[me] = acc
        cps = [_rcopy(out_ref.at[me], out_ref.at[me], send_b, recv_b, r, to) for r, to, idx in peers]
        for cp in cps:
            cp.start()
        for (r, to, idx), cp in zip(peers, cps):
            cp.wait_send()
            _rcopy(out_ref.at[idx], out_ref.at[idx], send_b, recv_b, r, to).wait_recv()

    vm = pl.BlockSpec(memory_space=pltpu.VMEM)
    return pl.pallas_call(
        body, name=name, in_specs=[vm], out_specs=vm,
        out_shape=jax.ShapeDtypeStruct(buf.shape, F32),
        scratch_shapes=[pltpu.VMEM(buf.shape, F32)] + [pltpu.SemaphoreType.DMA((8,))] * 4,
        compiler_params=pltpu.CompilerParams(vmem_limit_bytes=VMEM_LIMIT_BYTES),
    )(buf)


def _adamw_math(w, g, m, v):
    m = ADAM_B1 * m + (1.0 - ADAM_B1) * g
    v = ADAM_B2 * v + (1.0 - ADAM_B2) * (g * g)
    m_hat = m / (1.0 - ADAM_B1 ** ADAM_STEP)
    v_hat = v / (1.0 - ADAM_B2 ** ADAM_STEP)
    delta = -ADAM_LR * (m_hat / (jnp.sqrt(v_hat) + ADAM_EPS) + ADAM_WD * w)
    return [delta, m, v]


def adamw(w, g, m, v, name):
    return _ew_call(_adamw_math, [w, g, m, v], [F32, F32, F32], name)


_WEIGHTS = ['norm_w', 'w_in', 's5_lambda_re', 's5_lambda_im', 's5_log_step', 's5_b_re', 's5_b_im', 's5_c_re', 's5_c_im',
            's5_d', 's5_w_glu', 's5_b_glu', 'rg_conv_w', 'rg_conv_b', 'rg_w_a', 'rg_b_a', 'rg_w_x', 'rg_b_x', 'rg_lambda',
            'hg_lower_bounds', 'hg_norm_w', 'w_branch', 'w_out', 'final_norm_w']
_BIG = ('w_in', 's5_w_glu', 'w_branch', 'w_out')
_SMALL = [n for n in _WEIGHTS if n not in _BIG]
_LANES = 128
_N_DEV = 8


def _pack(arrs):
    flat = jnp.concatenate([a.reshape(-1) for a in arrs])
    unit = _N_DEV * 8 * _LANES
    total = -(-flat.shape[0] // unit) * unit
    flat = jnp.pad(flat, (0, total - flat.shape[0]))
    return flat.reshape(_N_DEV, total // (_N_DEV * _LANES), _LANES)


def _unpack(buf, shapes):
    flat = buf.reshape(-1)
    out, off = [], 0
    for s in shapes:
        n = math.prod(s)
        out.append(flat[off:off + n].reshape(s))
        off += n
    return out


def _step(a):
    x_idx, y_idx, c_idx = lax.axis_index("x"), lax.axis_index("y"), lax.axis_index("c")
    chip = 2 * x_idx + y_idx
    L = a["norm_w"].shape[0]
    W = a["s5_d"].shape[1]
    cw = a["rg_conv_w"]
    wc = cw.shape[2]
    placed = lax.dynamic_update_slice(jnp.zeros((L, RG_CONV, W), F32), cw, (0, 0, chip * wc))
    placed = placed * (c_idx == 0).astype(F32)
    conv_full = _unpack(all_reduce_small(_pack([placed]), "gather_conv"), [(L, RG_CONV, W)])[0]
    wts = {n: a[n] for n in _SMALL}
    wts["rg_conv_w"] = conv_full
    lbs, vjp_lbs = jax.vjp(_lower_bounds, a["hg_lower_bounds"])
    D = a["w_in"].shape[1]
    w_in = a["w_in"].astype(BF16).reshape(L, 2, D // 2, a["w_in"].shape[2])
    rest = [_pack_rest(a["s5_w_glu"][l].astype(BF16), a["w_branch"][l].astype(BF16), a["w_out"][l].astype(BF16))
            for l in range(L)]
    x = a["x"][0]
    land_w = gather_rows(w_in[0], "gather_w_in_first")
    land_r = gather_rows(rest[0], "gather_rest_first")
    saved = []
    for l in range(L):
        last = l + 1 == L
        h, vjp_rms = jax.vjp(_rms_bf16, x, a["norm_w"][l])
        w_l = _place_own(land_w, w_in[l])
        if last:
            z = _mm_nn(h, w_l, None, "w_in_fwd", BF16)
        else:
            z, land_w = _mm_nn(h, w_l, None, "w_in_fwd_gather", BF16, side=_gather_side(w_in[l + 1]))
        lw = {n: wts[n][l] for n in _LAYER_SMALL}
        lw["lbs"] = lbs[l]
        r_l = lax.dynamic_update_slice(land_r, rest[l][None], (chip, 0, 0, 0))
        x_next, vjp_tail, land_r = jax.vjp(functools.partial(_layer_tail, nxt=None if last else rest[l + 1]),
                                           z, x, lw, r_l, has_aux=True)
        saved.append((h, w_l, vjp_rms, vjp_tail))
        x = x_next
    rows, vjp_loss = jax.vjp(_loss_rows, x, a["final_norm_w"], a["loss_target"][0])
    loss = lax.psum(jnp.sum(rows), ("x", "y", "c"))
    dx, d_final, _ = vjp_loss(jnp.ones_like(rows))
    gw = {n: [None] * L for n in _LAYER_SMALL + ["lbs", "norm_w"]}
    pairs = {"w": [None] * L, "r": [None] * L}
    recvs = {"w": [None] * L, "r": [None] * L}

    def pair_sum(g, tag):
        got = _swap_rows(g, "reduce_" + tag + "_swap")
        mine = lax.dynamic_index_in_dim(g, c_idx, 1, keepdims=False)
        return _ew_call(lambda p, q: [p.astype(F32) + q.astype(F32)], [mine, got], [BF16], "reduce_" + tag + "_pair")[0]

    for l in reversed(range(L)):
        h, w_l, vjp_rms, vjp_tail = saved[l]
        dz, dx_res, dlw, d_rest = vjp_tail(dx)
        if l + 1 == L:
            dh = _mm_nt(dz, w_l, h.dtype, "w_in_bwd_a")
            dw = _mm_tn(h, dz, N_CHIPS, BF16, "w_in_bwd_w")
        else:
            dh, recvs["r"][l + 1] = _mm_nt(dz, w_l, h.dtype, "w_in_bwd_a_scatter", side=_scatter_side(pairs["r"][l + 1]))
            dw, recvs["w"][l + 1] = _mm_tn(h, dz, N_CHIPS, BF16, "w_in_bwd_w_scatter", side=_scatter_side(pairs["w"][l + 1]))
        dx_rms, gw["norm_w"][l] = vjp_rms(dh)
        dx = dx_rms + dx_res
        for n in _LAYER_SMALL + ["lbs"]:
            gw[n][l] = dlw[n]
        pairs["w"][l] = pair_sum(dw.reshape(N_CHIPS, 2, D // 2, dw.shape[2]), "w_in")
        pairs["r"][l] = pair_sum(d_rest, "rest")
    recvs["w"][0] = scatter_rows(pairs["w"][0], "reduce_w_in_scatter")
    recvs["r"][0] = scatter_rows(pairs["r"][0], "reduce_rest_scatter")
    south = c_idx == 0

    def finish(tag, name):
        halves = []
        for l in range(L):
            own = lax.dynamic_index_in_dim(pairs[tag][l], chip, 0, keepdims=False)
            r = recvs[tag][l]
            halves.append(_ew_call(
                lambda p, r0, r1, r2: [((p.astype(F32) + r0.astype(F32)) + r1.astype(F32)) + r2.astype(F32)],
                [own, r[0], r[1], r[2]], [F32], "reduce_" + name + "_sum")[0])
        gh = jnp.stack(halves, 0)
        other = _swap_whole(gh, "reduce_" + name + "_join")
        return jnp.stack([jnp.where(south, gh, other), jnp.where(south, other, gh)], 1)

    grads = {"w_in": finish("w", "w_in").reshape(a["w_in"].shape)}
    g_rest = _unpack_rest(finish("r", "rest"), W, D)
    grads["w_branch"] = jnp.stack(g_rest["w_branch"], 1)
    grads["w_out"] = g_rest["w_out"]
    grads["s5_w_glu"] = g_rest["s5_w_glu"]
    gx = dx
    (d_lb,) = vjp_lbs(jnp.stack(gw["lbs"], 0))
    gw = {n: jnp.stack(v, 0) for n, v in gw.items()}
    gw["hg_lower_bounds"] = d_lb
    gw["final_norm_w"] = d_final
    small_shapes = [gw[n].shape for n in _SMALL]
    red = _unpack(all_reduce_small(_pack([gw[n].astype(F32) for n in _SMALL]), "reduce_small"), small_shapes)
    for n, g in zip(_SMALL, red):
        grads[n] = g
    grads["rg_conv_w"] = lax.dynamic_slice_in_dim(grads["rg_conv_w"], chip * wc, wc, 2)
    delta, new_m, new_v = {}, {}, {}
    for n in _BIG:
        delta[n], new_m[n], new_v[n] = adamw(a[n], grads[n], a["m_" + n], a["v_" + n], "adamw_" + n)
    shapes = [a[n].shape for n in _SMALL]
    packed = [_pack([t[n] for n in _SMALL]) for t in
              (a, grads, {n: a["m_" + n] for n in _SMALL}, {n: a["v_" + n] for n in _SMALL})]
    for dst, buf in zip((delta, new_m, new_v), adamw(*packed, "adamw_small")):
        for n, t in zip(_SMALL, _unpack(buf, shapes)):
            dst[n] = t
    return (loss, gx[None], *[grads[n] for n in _WEIGHTS], *[delta[n] for n in _WEIGHTS],
            *[new_m[n] for n in _WEIGHTS], *[new_v[n] for n in _WEIGHTS])


_ARG_NAMES = ["x"] + _WEIGHTS + ["loss_target"] + ["m_" + n for n in _WEIGHTS] + ["v_" + n for n in _WEIGHTS]


def kernel(x, norm_w, w_in, s5_lambda_re, s5_lambda_im, s5_log_step, s5_b_re, s5_b_im, s5_c_re, s5_c_im, s5_d, s5_w_glu, s5_b_glu, rg_conv_w, rg_conv_b, rg_w_a, rg_b_a, rg_w_x, rg_b_x, rg_lambda, hg_lower_bounds, hg_norm_w, w_branch, w_out, final_norm_w, loss_target, m_norm_w, m_w_in, m_s5_lambda_re, m_s5_lambda_im, m_s5_log_step, m_s5_b_re, m_s5_b_im, m_s5_c_re, m_s5_c_im, m_s5_d, m_s5_w_glu, m_s5_b_glu, m_rg_conv_w, m_rg_conv_b, m_rg_w_a, m_rg_b_a, m_rg_w_x, m_rg_b_x, m_rg_lambda, m_hg_lower_bounds, m_hg_norm_w, m_w_branch, m_w_out, m_final_norm_w, v_norm_w, v_w_in, v_s5_lambda_re, v_s5_lambda_im, v_s5_log_step, v_s5_b_re, v_s5_b_im, v_s5_c_re, v_s5_c_im, v_s5_d, v_s5_w_glu, v_s5_b_glu, v_rg_conv_w, v_rg_conv_b, v_rg_w_a, v_rg_b_a, v_rg_w_x, v_rg_b_x, v_rg_lambda, v_hg_lower_bounds, v_hg_norm_w, v_w_branch, v_w_out, v_final_norm_w):
    vals = (x, norm_w, w_in, s5_lambda_re, s5_lambda_im, s5_log_step, s5_b_re, s5_b_im, s5_c_re, s5_c_im, s5_d, s5_w_glu, s5_b_glu, rg_conv_w, rg_conv_b, rg_w_a, rg_b_a, rg_w_x, rg_b_x, rg_lambda, hg_lower_bounds, hg_norm_w, w_branch, w_out, final_norm_w, loss_target, m_norm_w, m_w_in, m_s5_lambda_re, m_s5_lambda_im, m_s5_log_step, m_s5_b_re, m_s5_b_im, m_s5_c_re, m_s5_c_im, m_s5_d, m_s5_w_glu, m_s5_b_glu, m_rg_conv_w, m_rg_conv_b, m_rg_w_a, m_rg_b_a, m_rg_w_x, m_rg_b_x, m_rg_lambda, m_hg_lower_bounds, m_hg_norm_w, m_w_branch, m_w_out, m_final_norm_w, v_norm_w, v_w_in, v_s5_lambda_re, v_s5_lambda_im, v_s5_log_step, v_s5_b_re, v_s5_b_im, v_s5_c_re, v_s5_c_im, v_s5_d, v_s5_w_glu, v_s5_b_glu, v_rg_conv_w, v_rg_conv_b, v_rg_w_a, v_rg_b_a, v_rg_w_x, v_rg_b_x, v_rg_lambda, v_hg_lower_bounds, v_hg_norm_w, v_w_branch, v_w_out, v_final_norm_w)
    return _step(dict(zip(_ARG_NAMES, vals)))
```

```python
import functools
import math

import jax
import jax.numpy as jnp
from jax import lax
from jax.experimental import pallas as pl
from jax.experimental.pallas import tpu as pltpu

F32 = jnp.float32
BF16 = jnp.bfloat16
EPS = 1e-6
RG_C = 8.0
S5_GROUP = 16
S5_STATE = 64
RG_BLOCKS = 16
RG_CONV = 4
HG_HEADS = 8
N_CHIPS = 4
VMEM_LIMIT_BYTES = 56 * 1024 * 1024
NEG_BIG = -1e30

ADAM_LR = 0.001
ADAM_B1 = 0.9
ADAM_B2 = 0.999
ADAM_EPS = 1e-08
ADAM_WD = 0.01
ADAM_STEP = 10

MESH = pl.DeviceIdType.MESH


def _cparams(sem):
    return pltpu.CompilerParams(dimension_semantics=sem, vmem_limit_bytes=VMEM_LIMIT_BYTES)


_DOT_DIMS = {"nn": (((1,), (0,)), ((), ())), "nt": (((1,), (1,)), ((), ())), "tn": (((0,), (0,)), ((), ()))}


def _bdot_raw(a, b, form):
    return lax.dot_general(a.astype(BF16), b.astype(BF16), _DOT_DIMS[form], preferred_element_type=F32)


@functools.partial(jax.custom_vjp, nondiff_argnums=(2,))
def bdot(a, b, form):
    return _bdot_raw(a, b, form)


def _bdot_fwd(a, b, form):
    return _bdot_raw(a, b, form), (a, b)


def _bdot_bwd(form, res, g):
    a, b = res
    if form == "nn":
        da, db = _bdot_raw(g, b, "nt"), _bdot_raw(a, g, "tn")
    elif form == "nt":
        da, db = _bdot_raw(g, b, "nn"), _bdot_raw(g, a, "tn")
    else:
        da, db = _bdot_raw(b, g, "nt"), _bdot_raw(a, g, "nn")
    return da.astype(a.dtype), db.astype(b.dtype)


bdot.defvjp(_bdot_fwd, _bdot_bwd)


@functools.partial(jax.custom_vjp, nondiff_argnums=(1,))
def sroll(x, d):
    return pltpu.roll(x, d, 0)


def _sroll_fwd(x, d):
    return pltpu.roll(x, d, 0), None


def _sroll_bwd(d, _, g):
    return (pltpu.roll(g, g.shape[0] - d, 0),)


sroll.defvjp(_sroll_fwd, _sroll_bwd)


def _row_iota(n):
    return lax.broadcasted_iota(jnp.int32, (n, 1), 0)


def _last_row(x, ti):
    return jnp.sum(jnp.where(ti == x.shape[0] - 1, x, 0.0), axis=0, keepdims=True)


def _silu(x):
    return x * jax.nn.sigmoid(x)


class _Tiling:
    def __init__(self, CB, tT, nT, ncb=None, bmap=None):
        self.CB, self.tT, self.nT = CB, tT, nT
        self.ncb = CB if ncb is None else ncb
        self.bmap = (lambda cb, t: (t, cb)) if bmap is None else bmap


def _tiled_specs(params, views, carry_shapes, out_defs, tl, rev):
    nT = tl.nT
    tmap = (lambda t: nT - 1 - t) if rev else (lambda t: t)

    def rspec(col0, width):
        bw = width // tl.ncb
        off = col0 // bw

        def imap(cb, t):
            rb, cbk = tl.bmap(cb, tmap(t))
            return (rb, off + cbk)

        return pl.BlockSpec((tl.tT, bw), imap)

    p_specs = [pl.BlockSpec((None,) + p.shape[1:], lambda cb, t: (cb, 0, 0)) for p in params]
    r_specs = [rspec(c0, w) for (c0, w) in views]
    dr_specs = [rspec(0, w) for (_, w) in views]
    o_specs = [rspec(0, w) for (w, _) in out_defs]
    s_specs = [pl.BlockSpec((None, r, c), lambda cb, t: (tmap(t), 0, cb)) for (r, c) in carry_shapes]
    return p_specs, r_specs, dr_specs, o_specs, s_specs


def _tiled_fwd(f, name, params, rows, views, carry_shapes, out_defs, tl, side=None):
    T = rows[0].shape[0]
    CB, tT, nT = tl.CB, tl.tT, tl.nT
    n_p, n_r, n_o, n_c = len(params), len(rows), len(out_defs), len(carry_shapes)
    p_specs, r_specs, _, o_specs, s_specs = _tiled_specs(params, views, carry_shapes, out_defs, tl, False)
    x_ins, x_outs, x_sems = _side_parts(side)
    n_xi, n_xo = len(x_ins), len(x_outs)

    def body(*refs):
        i = 0
        p_refs = refs[i:i + n_p]; i += n_p
        r_refs = refs[i:i + n_r]; i += n_r
        xi_refs = refs[i:i + n_xi]; i += n_xi
        o_refs = refs[i:i + n_o]; i += n_o
        s_refs = refs[i:i + n_c]; i += n_c
        xo_refs = refs[i:i + n_xo]; i += n_xo
        c_refs = refs[i:i + n_c]
        cb, t = pl.program_id(0), pl.program_id(1)
        if side is not None:
            @pl.when((cb == 0) & (t == 0))
            def _():
                side.start(xi_refs, xo_refs, refs[-2], refs[-1])

        @pl.when(t == 0)
        def _():
            for c in c_refs:
                c[...] = jnp.zeros_like(c)

        carries = [c[...] for c in c_refs]
        for s, cv in zip(s_refs, carries):
            s[...] = cv
        outs, newc = f([p[...] for p in p_refs], [r[...].astype(F32) for r in r_refs], carries, t * tT)
        for o, v in zip(o_refs, outs):
            o[...] = v.astype(o.dtype)
        for c, v in zip(c_refs, newc):
            c[...] = v
        if side is not None:
            @pl.when((cb == CB - 1) & (t == nT - 1))
            def _():
                side.finish(xi_refs, xo_refs, refs[-2], refs[-1])

    out_shape = [jax.ShapeDtypeStruct((T, w), dt) for (w, dt) in out_defs]
    out_shape += [jax.ShapeDtypeStruct((nT, r, c * CB), F32) for (r, c) in carry_shapes]
    res = pl.pallas_call(
        body, name=name + "_fwd", grid=(CB, nT),
        in_specs=p_specs + r_specs + [_ANY] * n_xi, out_specs=o_specs + s_specs + [_ANY] * n_xo,
        out_shape=out_shape + x_outs,
        scratch_shapes=[pltpu.VMEM((r, c), F32) for (r, c) in carry_shapes] + x_sems,
        compiler_params=_cparams(("arbitrary", "arbitrary")),
    )(*params, *rows, *x_ins)
    return list(res[:n_o]), list(res[n_o:n_o + n_c]), list(res[n_o + n_c:])


def _tiled_bwd(f, name, params, rows, views, saved, douts, carry_shapes, out_defs, tl):
    T = rows[0].shape[0]
    CB, tT, nT = tl.CB, tl.tT, tl.nT
    n_p, n_r, n_o, n_c = len(params), len(rows), len(out_defs), len(carry_shapes)
    p_specs, r_specs, dr_specs, o_specs, s_specs = _tiled_specs(params, views, carry_shapes, out_defs, tl, True)
    out_dtypes = [dt for (_, dt) in out_defs]

    def body(*refs):
        i = 0
        p_refs = refs[i:i + n_p]; i += n_p
        r_refs = refs[i:i + n_r]; i += n_r
        s_refs = refs[i:i + n_c]; i += n_c
        g_refs = refs[i:i + n_o]; i += n_o
        dp_refs = refs[i:i + n_p]; i += n_p
        dr_refs = refs[i:i + n_r]; i += n_r
        dc_refs = refs[i:]
        t = pl.program_id(1)

        @pl.when(t == 0)
        def _():
            for c in dc_refs:
                c[...] = jnp.zeros_like(c)
            for d in dp_refs:
                d[...] = jnp.zeros_like(d)

        t0 = (nT - 1 - t) * tT

        def g(P, R, C):
            outs, newc = f(P, R, C, t0)
            return [o.astype(dt) for o, dt in zip(outs, out_dtypes)], list(newc)

        _, vjp = jax.vjp(g, [p[...] for p in p_refs], [r[...].astype(F32) for r in r_refs], [s[...] for s in s_refs])
        dP, dR, dC = vjp(([gr[...] for gr in g_refs], [c[...] for c in dc_refs]))
        for d, v in zip(dp_refs, dP):
            d[...] += v
        for d, v in zip(dr_refs, dR):
            d[...] = v.astype(d.dtype)
        for c, v in zip(dc_refs, dC):
            c[...] = v

    out_shape = [jax.ShapeDtypeStruct(p.shape, F32) for p in params]
    out_shape += [jax.ShapeDtypeStruct((T, w), r.dtype) for r, (_, w) in zip(rows, views)]
    res = pl.pallas_call(
        body, name=name + "_bwd", grid=(CB, nT),
        in_specs=p_specs + r_specs + s_specs + o_specs, out_specs=p_specs + dr_specs, out_shape=out_shape,
        scratch_shapes=[pltpu.VMEM((r, c), F32) for (r, c) in carry_shapes],
        compiler_params=_cparams(("arbitrary", "arbitrary")),
    )(*params, *rows, *saved, *douts)
    return list(res[:n_p]), list(res[n_p:])


def tiled_op(f, name, params, rows, carry_shapes, out_defs, CB, tT, tiling=None, side_of=None):
    arrs = [r[0] if isinstance(r, tuple) else r for r in rows]
    views = [(r[1], r[2]) if isinstance(r, tuple) else (0, r.shape[1]) for r in rows]
    tl = tiling if tiling is not None else _Tiling(CB, tT, arrs[0].shape[0] // tT)
    make_side, side_arrs = side_of if side_of is not None else (None, [])

    def run_fwd(params, arrs, side_arrs):
        side = make_side(*side_arrs) if make_side is not None else None
        return _tiled_fwd(f, name, params, arrs, views, carry_shapes, out_defs, tl, side)

    @jax.custom_vjp
    def op(params, arrs, side_arrs):
        outs, _, extra = run_fwd(params, arrs, side_arrs)
        return outs, extra

    def op_fwd(params, arrs, side_arrs):
        outs, saved, extra = run_fwd(params, arrs, side_arrs)
        return (outs, extra), (params, arrs, saved, side_arrs)

    def op_bwd(res, cts):
        params, arrs, saved, side_arrs = res
        douts, _ = cts
        dP, dR = _tiled_bwd(f, name, params, arrs, views, saved, list(douts), carry_shapes, out_defs, tl)
        dR = [d if w == a.shape[1] else jnp.pad(d, ((0, 0), (c0, a.shape[1] - c0 - w)))
              for d, a, (c0, w) in zip(dR, arrs, views)]
        return dP, dR, [jnp.zeros_like(s) for s in side_arrs]

    op.defvjp(op_fwd, op_bwd)
    outs, extra = op(list(params), arrs, list(side_arrs))
    return outs if side_of is None else (outs, extra)


def _pick(n, pref):
    for t in pref:
        if n % t == 0:
            return t
    return n


class _Side:
    def __init__(self, ins, outs, n_sems, start, finish):
        self.ins, self.outs, self.n_sems, self.start, self.finish = ins, outs, n_sems, start, finish


def _side_parts(side):
    if side is None:
        return [], [], []
    sems = [pltpu.SemaphoreType.DMA((side.n_sems,)), pltpu.SemaphoreType.DMA((side.n_sems,))]
    return list(side.ins), list(side.outs), sems


def _mm_nn(a, w, res, name, out_dtype=F32, side=None):
    M, K = a.shape
    S, _, Ns = w.shape
    tm = _pick(M, (1024, 512, 256, 128))
    tn = _pick(Ns, (512, 256, 128))
    nps = Ns // tn
    has_res = res is not None
    n_main = 3 if has_res else 2
    s_ins, s_outs, s_sems = _side_parts(side)
    ni, nj = M // tm, S * nps

    def body(*refs):
        a_ref, w_ref = refs[0], refs[1]
        o_ref = refs[n_main + len(s_ins)]
        if side is not None:
            si = refs[n_main:n_main + len(s_ins)]
            so = refs[n_main + len(s_ins) + 1:n_main + len(s_ins) + 1 + len(s_outs)]
            send, recv = refs[-2], refs[-1]
            i, j = pl.program_id(0), pl.program_id(1)

            @pl.when((i == 0) & (j == 0))
            def _():
                side.start(si, so, send, recv)

        acc = _bdot_raw(a_ref[...], w_ref[...], "nn")
        if has_res:
            acc = acc + refs[2][...]
        o_ref[...] = acc.astype(o_ref.dtype)
        if side is not None:
            @pl.when((i == ni - 1) & (j == nj - 1))
            def _():
                side.finish(si, so, send, recv)

    in_specs = [pl.BlockSpec((tm, K), lambda i, j: (i, 0)),
                pl.BlockSpec((None, K, tn), lambda i, j: (j // nps, 0, j % nps))]
    args = [a, w]
    if has_res:
        in_specs.append(pl.BlockSpec((tm, tn), lambda i, j: (i, j)))
        args.append(res)
    out = pl.pallas_call(
        body, name=name, grid=(ni, nj), in_specs=in_specs + [_ANY] * len(s_ins),
        out_specs=[pl.BlockSpec((tm, tn), lambda i, j: (i, j))] + [_ANY] * len(s_outs),
        out_shape=[jax.ShapeDtypeStruct((M, S * Ns), out_dtype)] + s_outs,
        scratch_shapes=s_sems,
        compiler_params=_cparams(("arbitrary", "arbitrary")),
    )(*args, *s_ins)
    return out[0] if side is None else out


def _mm_nt(g, w, out_dtype, name, side=None):
    M, N = g.shape
    S, K, Ns = w.shape
    tm = _pick(M, (1024, 512, 256, 128))
    tk = _pick(K, (1024, 512, 256, 128))
    tn = _pick(Ns, (1792, 1024, 512, 256, 128))
    nps = Ns // tn
    nn = S * nps
    ni, nk = M // tm, K // tk
    s_ins, s_outs, s_sems = _side_parts(side)

    def body(*refs):
        g_ref, w_ref = refs[0], refs[1]
        o_ref = refs[2 + len(s_ins)]
        acc_ref = refs[3 + len(s_ins) + len(s_outs)]
        i, k, n = pl.program_id(0), pl.program_id(1), pl.program_id(2)
        if side is not None:
            si = refs[2:2 + len(s_ins)]
            so = refs[3 + len(s_ins):3 + len(s_ins) + len(s_outs)]

            @pl.when((i == 0) & (k == 0) & (n == 0))
            def _():
                side.start(si, so, refs[-2], refs[-1])

        @pl.when(n == 0)
        def _():
            acc_ref[...] = jnp.zeros_like(acc_ref)

        acc_ref[...] += _bdot_raw(g_ref[...], w_ref[...], "nt")

        @pl.when(n == nn - 1)
        def _():
            o_ref[...] = acc_ref[...].astype(o_ref.dtype)

        if side is not None:
            @pl.when((i == ni - 1) & (k == nk - 1) & (n == nn - 1))
            def _():
                side.finish(si, so, refs[-2], refs[-1])

    out = pl.pallas_call(
        body, name=name, grid=(ni, nk, nn),
        in_specs=[pl.BlockSpec((tm, tn), lambda i, k, n: (i, n)),
                  pl.BlockSpec((None, tk, tn), lambda i, k, n: (n // nps, k, n % nps))] + [_ANY] * len(s_ins),
        out_specs=[pl.BlockSpec((tm, tk), lambda i, k, n: (i, k))] + [_ANY] * len(s_outs),
        out_shape=[jax.ShapeDtypeStruct((M, K), out_dtype)] + s_outs,
        scratch_shapes=[pltpu.VMEM((tm, tk), F32)] + s_sems,
        compiler_params=_cparams(("arbitrary", "arbitrary", "arbitrary")),
    )(g, w, *s_ins)
    return out[0] if side is None else out


def _mm_tn(a, g, S, out_dtype, name, side=None):
    T, K = a.shape
    N = g.shape[1]
    Ns = N // S
    tk = _pick(K, (2048, 1024, 512, 256, 128))
    tn = _pick(Ns, (1024, 896, 512, 256, 128))
    tt = _pick(T, (1024, 512, 256, 128))
    nps = Ns // tn
    nt = T // tt
    nk, nj = K // tk, S * nps
    a_t = a.astype(BF16).T
    s_ins, s_outs, s_sems = _side_parts(side)

    def body(*refs):
        a_ref, g_ref = refs[0], refs[1]
        o_ref = refs[2 + len(s_ins)]
        acc_ref = refs[3 + len(s_ins) + len(s_outs)]
        k, j, t = pl.program_id(0), pl.program_id(1), pl.program_id(2)
        if side is not None:
            si = refs[2:2 + len(s_ins)]
            so = refs[3 + len(s_ins):3 + len(s_ins) + len(s_outs)]
            send, recv = refs[-2], refs[-1]

            @pl.when((k == 0) & (j == 0) & (t == 0))
            def _():
                side.start(si, so, send, recv)

        @pl.when(t == 0)
        def _():
            acc_ref[...] = jnp.zeros_like(acc_ref)

        acc_ref[...] += _bdot_raw(a_ref[...], g_ref[...], "nn")

        @pl.when(t == nt - 1)
        def _():
            o_ref[...] = acc_ref[...].astype(o_ref.dtype)

        if side is not None:
            @pl.when((k == nk - 1) & (j == nj - 1) & (t == nt - 1))
            def _():
                side.finish(si, so, send, recv)

    out = pl.pallas_call(
        body, name=name, grid=(nk, nj, nt),
        in_specs=[pl.BlockSpec((tk, tt), lambda k, j, t: (k, t)),
                  pl.BlockSpec((tt, tn), lambda k, j, t: (t, j))] + [_ANY] * len(s_ins),
        out_specs=[pl.BlockSpec((None, tk, tn), lambda k, j, t: (j // nps, k, j % nps))] + [_ANY] * len(s_outs),
        out_shape=[jax.ShapeDtypeStruct((S, K, Ns), out_dtype)] + s_outs,
        scratch_shapes=[pltpu.VMEM((tk, tn), F32)] + s_sems,
        compiler_params=_cparams(("arbitrary", "arbitrary", "arbitrary")),
    )(a_t, g, *s_ins)
    return out[0] if side is None else out


def linear(a, w, name, res=None, out_dtype=F32):
    @jax.custom_vjp
    def op(a, w, res):
        return _mm_nn(a, w, res, name + "_fwd", out_dtype)

    def op_fwd(a, w, res):
        return _mm_nn(a, w, res, name + "_fwd", out_dtype), (a, w)

    def op_bwd(saved, g):
        a, w = saved
        da = _mm_nt(g, w, a.dtype, name + "_bwd_a")
        dw = _mm_tn(a, g, w.shape[0], w.dtype, name + "_bwd_w")
        return da, dw, (None if res is None else g)

    op.defvjp(op_fwd, op_bwd)
    return op(a, w, res)


def _rms_tile(params, rows, carries, t0):
    (w,), (x,) = params, rows
    y = x * lax.rsqrt(jnp.mean(x * x, axis=-1, keepdims=True) + EPS) * w
    return [y], []


def _s5_tile(params, rows, carries, t0):
    t_m, s_re, s_im, r_re, r_im, apow_re, apow_im, d = params
    (u,) = rows
    n = u.shape[0]
    ti = _row_iota(n)
    pi = _row_iota(apow_re.shape[0])
    x_re, x_im = bdot(u, s_re, "nn"), bdot(u, s_im, "nn")
    k = 0
    while (1 << k) < n:
        sh = 1 << k
        p_re, p_im = _row_sel(apow_re, pi, k), _row_sel(apow_im, pi, k)
        q_re, q_im = sroll(x_re, sh), sroll(x_im, sh)
        m = ti >= sh
        x_re, x_im = (x_re + jnp.where(m, p_re * q_re - p_im * q_im, 0.0),
                      x_im + jnp.where(m, p_re * q_im + p_im * q_re, 0.0))
        k += 1
    x_re = jnp.where(ti >= 1, sroll(x_re, 1), 0.0)
    x_im = jnp.where(ti >= 1, sroll(x_im, 1), 0.0)
    y = bdot(u, t_m, "nn") + bdot(x_re, r_re, "nn") + bdot(x_im, r_im, "nn") + d * u
    return [jax.nn.gelu(y)], []


def _glu_tile(params, rows, carries, t0):
    (b,), (y, zg, ga) = params, rows
    return [y * jax.nn.sigmoid(zg + b) * _silu(ga)], []


def _neg_expm1(z):
    small = -(z * (1.0 + z * (0.5 + z * (1.0 / 6.0))))
    return jnp.where(z > -0.01, small, 1.0 - jnp.exp(z))


def _rg_tile(params, rows, carries, t0):
    conv_w, conv_b, w_a, b_a, w_x, b_x, sp = params
    x, gate = rows
    x_prev, h_prev = carries
    tT = x.shape[0]
    ti = _row_iota(tT)
    ci = _row_iota(RG_CONV)
    xc = _row_sel(conv_w, ci, RG_CONV - 1) * x + conv_b
    for k in range(1, RG_CONV):
        xs = jnp.where(ti >= k, sroll(x, k), sroll(x_prev, k))
        xc = xc + _row_sel(conv_w, ci, RG_CONV - 1 - k) * xs
    r = jax.nn.sigmoid(bdot(xc, w_a, "nn") + b_a)
    i = jax.nn.sigmoid(bdot(xc, w_x, "nn") + b_x)
    log_a = -RG_C * r * sp
    a = jnp.exp(log_a)
    mult = jnp.sqrt(_neg_expm1(2.0 * log_a))
    mult = jnp.where(ti + t0 == 0, 1.0, mult)
    b = mult * (i * xc)
    b = b + jnp.where(ti == 0, a * h_prev, 0.0)
    k = 1
    while k < tT:
        m = ti >= k
        b = b + jnp.where(m, a * sroll(b, k), 0.0)
        a = jnp.where(m, a * sroll(a, k), a)
        k *= 2
    return [b * _silu(gate)], [x, _last_row(b, ti)]


def _hg_tile(params, rows, carries, t0):
    lb, nw = params
    q, fl, v, gate = rows
    (st,) = carries
    tT = q.shape[0]
    ti = _row_iota(tT)
    qs = _silu(q)
    f = lb + (1.0 - lb) * jax.nn.sigmoid(fl)
    kk = 1.0 - f
    G = jnp.log(f)
    k = 1
    while k < tT:
        G = G + jnp.where(ti >= k, sroll(G, k), 0.0)
        k *= 2
    inter = bdot(qs * jnp.exp(G), st, "nt")
    tr = lax.broadcasted_iota(jnp.int32, (tT, tT), 0)
    sc = lax.broadcasted_iota(jnp.int32, (tT, tT), 1)
    attn = jnp.zeros((tT, tT), F32)
    blk = tT
    while blk > 8:
        sub = blk // 4
        for j in range(1, 4):
            ref = jnp.zeros_like(G)
            for b in range(tT // blk):
                row = _row_sel(G, ti, b * blk + sub * j - 1)
                ref = ref + jnp.where(_div2(ti, blk) == b, row, 0.0)
            tmask = _div2(_mod2(ti, blk), sub) == j
            smask = _mod2(ti, blk) < sub * j
            qt = qs * jnp.exp(jnp.where(tmask, G - ref, NEG_BIG))
            kt = kk * jnp.exp(jnp.where(smask, ref - G, NEG_BIG))
            aj = bdot(qt, kt, "nt")
            attn = attn + jnp.where(_div2(tr, blk) == _div2(sc, blk), aj, 0.0)
        blk = sub
    intra = bdot(attn, v, "nn")
    for d in range(blk):
        if d == 0:
            kd, gd, vd = kk, G, v
        else:
            kd, gd, vd = sroll(kk, d), sroll(G, d), sroll(v, d)
        m = _mod2(ti, blk) >= d
        w = jnp.sum(qs * kd * jnp.exp(jnp.where(m, G - gd, NEG_BIG)), axis=1, keepdims=True)
        intra = intra + w * vd
    o = inter + intra
    g_last = _last_row(G, ti)
    k_dec = kk * jnp.exp(g_last - G)
    st_new = st * jnp.exp(g_last) + bdot(v, k_dec, "tn")
    o = o * lax.rsqrt(jnp.mean(o * o, axis=-1, keepdims=True) + EPS) * nw
    return [o * _silu(gate)], [st_new]


def _row_sel(x, ti, r):
    return jnp.sum(jnp.where(ti == r, x, 0.0), axis=0, keepdims=True)


def _div2(i, p):
    return lax.shift_right_logical(i, jnp.int32(p.bit_length() - 1))


def _mod2(i, p):
    return lax.bitwise_and(i, jnp.int32(p - 1))


def _merge_tile(params, rows, carries, t0):
    b0, b1, b2, g0, g1, g2 = rows
    m = jax.nn.sigmoid(g0) * b0 + jax.nn.sigmoid(g1) * b1 + jax.nn.sigmoid(g2) * b2
    return [m], []


def _loss_tile(params, rows, carries, t0):
    (w,), (x, tgt) = params, rows
    y = x * lax.rsqrt(jnp.mean(x * x, axis=-1, keepdims=True) + EPS) * w
    e = y - tgt
    return [0.5 * jnp.mean(e * e, axis=-1, keepdims=True)], []


def _block_diag(w, cb):
    n, i, j = w.shape
    g = n // cb
    w4 = w.reshape(cb, g, i, j)
    eye = jnp.eye(g, dtype=w.dtype)
    return jnp.einsum("cgij,gk->cgikj", w4, eye).reshape(cb, g * i, g * j)


def _s5_params(lam_re, lam_im, log_step, b_re, b_im, c_re, c_im, d, levels):
    G, P = lam_re.shape
    step = jnp.exp(log_step)[:, None]
    mag = jnp.exp(lam_re * step)
    ang = lam_im * step
    abar_re = mag * jnp.cos(ang)
    abar_im = mag * jnp.sin(ang)
    num_re = abar_re - 1.0
    num_im = abar_im
    den = lam_re * lam_re + lam_im * lam_im
    coef_re = (num_re * lam_re + num_im * lam_im) / den
    coef_im = (num_im * lam_re - num_re * lam_im) / den
    bbar_re = coef_re[..., None] * b_re - coef_im[..., None] * b_im
    bbar_im = coef_re[..., None] * b_im + coef_im[..., None] * b_re
    H = b_re.shape[2]
    Lc = S5_LC
    hi = lax.Precision.HIGHEST

    def powers(ks):
        ks = jnp.asarray(ks, F32)[:, None, None]
        m = jnp.exp(ks * (lam_re * step))
        return m * jnp.cos(ks * ang), m * jnp.sin(ks * ang)

    pw_re, pw_im = powers(list(range(Lc + 1)))
    ab_re = pw_re[..., None] * bbar_re - pw_im[..., None] * bbar_im
    ab_im = pw_re[..., None] * bbar_im + pw_im[..., None] * bbar_re
    kern = (jnp.einsum("gap,kgph->gkha", c_re, ab_re[:Lc], precision=hi)
            - jnp.einsum("gap,kgph->gkha", c_im, ab_im[:Lc], precision=hi))
    kk = jnp.arange(Lc)[:, None, None]
    jj = jnp.arange(Lc)[None, :, None]
    ii = jnp.arange(Lc)[None, None, :]
    place = (ii - jj == kk).astype(F32)
    t_m = jnp.einsum("gkha,kji->gjhia", kern, place, precision=hi).reshape(G, Lc * H, Lc * H)
    s_re = ab_re[:Lc][::-1].transpose(1, 0, 3, 2).reshape(G, Lc * H, P)
    s_im = ab_im[:Lc][::-1].transpose(1, 0, 3, 2).reshape(G, Lc * H, P)
    m_re = c_re[None] * pw_re[1:, :, None, :] - c_im[None] * pw_im[1:, :, None, :]
    m_im = c_re[None] * pw_im[1:, :, None, :] + c_im[None] * pw_re[1:, :, None, :]
    r_re = m_re.transpose(1, 3, 0, 2).reshape(G, P, Lc * H)
    r_im = -m_im.transpose(1, 3, 0, 2).reshape(G, P, Lc * H)
    rows = max(8, levels)
    ap_re, ap_im = powers([Lc * (1 << k) for k in range(levels)] + [0] * (rows - levels))
    dd = jnp.tile(d.reshape(G, 1, H), (1, 1, Lc))
    return [t_m, s_re, s_im, r_re, r_im, ap_re.transpose(1, 0, 2), ap_im.transpose(1, 0, 2), dd]


_LANE = 128


def _lane_perm_matrix(lc, h):
    n = lc * _LANE
    src = jnp.arange(n).reshape(lc, _LANE // h, h).transpose(1, 0, 2).reshape(n)
    return (jnp.arange(n)[:, None] == src[None, :]).astype(BF16)


def _lane_perm(x, p, p_t, name):
    @jax.custom_vjp
    def op(x):
        return _mm_nn(x, p[None], None, name, BF16)

    def op_fwd(x):
        return _mm_nn(x, p[None], None, name, BF16), None

    def op_bwd(_, g):
        return (_mm_nn(g, p_t[None], None, name + "_t", x.dtype),)

    op.defvjp(op_fwd, op_bwd)
    return op(x)


def _to_chunks(u, lc, h, perm):
    t, w = u.shape
    x = u.reshape(t // lc, lc, w // _LANE, _LANE).transpose(2, 0, 1, 3).reshape((w // _LANE) * (t // lc), lc * _LANE)
    return _lane_perm(x, perm, perm.T, "s5_to_chunks")


def _from_chunks(y, lc, w, perm):
    tiles = w // _LANE
    n_chunks = y.shape[0] // tiles
    x = _lane_perm(y, perm.T, perm, "s5_from_chunks")
    return x.reshape(tiles, n_chunks, lc, _LANE).transpose(1, 2, 0, 3).reshape(n_chunks * lc, w)


def _vec(v, cb):
    return v.reshape(cb, 1, -1)


S5_LC = 16
RG_CB, RG_TT = 4, 256
HG_TT = 128
ROW_TT = 256
MERGE_TT = 128


_LAYER_SMALL = ['s5_lambda_re', 's5_lambda_im', 's5_log_step', 's5_b_re', 's5_b_im', 's5_c_re', 's5_c_im', 's5_d',
                's5_b_glu', 'rg_conv_w', 'rg_conv_b', 'rg_w_a', 'rg_b_a', 'rg_w_x', 'rg_b_x', 'rg_lambda', 'hg_norm_w']


def _lower_bounds(hg_lower_bounds):
    lb_sm = jax.nn.softmax(hg_lower_bounds, axis=0)
    return jnp.cumsum(lb_sm, axis=0) - lb_sm[0]


def _rms_bf16(x, w):
    T, D = x.shape
    return tiled_op(_rms_tile, "rms", [w.reshape(1, 1, D)], [x], [], [(D, BF16)], 1, min(ROW_TT, T))[0]


def _loss_rows(x, w, target):
    T, D = x.shape
    return tiled_op(_loss_tile, "loss", [w.reshape(1, 1, D)], [x, target], [], [(1, F32)], 1, min(ROW_TT, T))[0]


def _layer_tail(z, x, lw, rest, nxt):
    T, D = x.shape
    lw = dict(lw, **_unpack_rest(rest, lw["s5_d"].shape[0], D))
    W = lw["s5_d"].shape[0]
    row_tt = min(ROW_TT, T)
    rg_tt, hg_tt = min(RG_TT, T), min(HG_TT, T)
    n_chunks = T // S5_LC
    gpt = _LANE // S5_GROUP
    perm = _lane_perm_matrix(S5_LC, S5_GROUP)
    s5_tiling = _Tiling(W // S5_GROUP, n_chunks, 1, ncb=gpt, bmap=lambda cb, t: (cb // gpt, cb % gpt))
    g_a, x_b, g_b, q_c, f_c, i_c, g_c = [(z, k * W, W) for k in range(1, 8)]
    gl = [(z, 8 * W + n * D, D) for n in range(3)]
    s5p = _s5_params(lw["s5_lambda_re"], lw["s5_lambda_im"], lw["s5_log_step"], lw["s5_b_re"], lw["s5_b_im"],
                     lw["s5_c_re"], lw["s5_c_im"], lw["s5_d"], int(math.log2(n_chunks)))
    (y1c,) = tiled_op(_s5_tile, "s5", s5p, [_to_chunks(z[:, :W], S5_LC, S5_GROUP, perm)], [],
                      [(S5_LC * _LANE, BF16)], W // S5_GROUP, n_chunks, tiling=s5_tiling)
    y1 = _from_chunks(y1c, S5_LC, W, perm)
    zg = linear(y1, lw["s5_w_glu"].reshape(1, W, W), "w_glu")
    (y_a,) = tiled_op(_glu_tile, "glu", [lw["s5_b_glu"].reshape(1, 1, W)], [y1, zg, g_a], [], [(W, BF16)], 1, row_tt)
    rgp = [lw["rg_conv_w"].reshape(RG_CONV, RG_CB, W // RG_CB).transpose(1, 0, 2),
           _vec(lw["rg_conv_b"], RG_CB), _block_diag(lw["rg_w_a"], RG_CB), _vec(lw["rg_b_a"], RG_CB),
           _block_diag(lw["rg_w_x"], RG_CB), _vec(lw["rg_b_x"], RG_CB), _vec(jax.nn.softplus(-lw["rg_lambda"]), RG_CB)]
    rc = W // RG_CB
    (y_b,) = tiled_op(_rg_tile, "rg", rgp, [x_b, g_b], [(rg_tt, rc), (1, rc)], [(W, BF16)], RG_CB, rg_tt)
    dk = W // HG_HEADS
    hgp = [_vec(lw["lbs"], HG_HEADS), _vec(lw["hg_norm_w"], HG_HEADS)]
    land = None
    if nxt is None:
        (y_c,) = tiled_op(_hg_tile, "hg", hgp, [q_c, f_c, i_c, g_c], [(dk, dk)], [(W, BF16)], HG_HEADS, hg_tt)
    else:
        (y_c,), (land,) = tiled_op(_hg_tile, "hg_gather", hgp, [q_c, f_c, i_c, g_c], [(dk, dk)], [(W, BF16)],
                                   HG_HEADS, hg_tt, side_of=(_gather_side, [nxt]))
    br = [linear(y, lw["w_branch"][n], "w_br", out_dtype=BF16) for n, y in enumerate((y_a, y_b, y_c))]
    (mg,) = tiled_op(_merge_tile, "merge", [], br + gl, [], [(D, BF16)], 1, min(MERGE_TT, T))
    return linear(mg, lw["w_out"].reshape(1, D, D), "w_out", res=x), land


def _pack_rest(w_glu, w_branch, w_out):
    c = w_branch.shape[-1]
    parts = [w_branch[n].reshape(2, -1, c) for n in range(3)] + [w_out.reshape(2, -1, c), w_glu.reshape(2, -1, c)]
    return jnp.concatenate(parts, 1)


def _rest_rows(W, D):
    c = D // N_CHIPS
    return [W // 2] * 3 + [(D // N_CHIPS) * D // (2 * c), (W // N_CHIPS) * W // (2 * c)]


def _unpack_rest(p, W, D):
    lead = p.shape[:-3]
    out, off = [], 0
    for r in _rest_rows(W, D):
        out.append(p[..., off:off + r, :])
        off += r
    br = [b.reshape(lead + (W, D // N_CHIPS)) for b in out[:3]]
    w_out = out[3].reshape(lead + (D // N_CHIPS, D))
    w_glu = out[4].reshape(lead + (W // N_CHIPS, W))
    return {"w_branch": br, "w_out": w_out, "s5_w_glu": w_glu}


_ANY = pl.BlockSpec(memory_space=pl.ANY)


def _place():
    x, y, c = lax.axis_index("x"), lax.axis_index("y"), lax.axis_index("c")
    chips = [(1 - x, y), (x, 1 - y), (1 - x, 1 - y)]
    return x, y, c, chips


def _rcopy(src, dst, send_sems, recv_sems, k, to):
    return pltpu.make_async_remote_copy(src_ref=src, dst_ref=dst, send_sem=send_sems.at[k], recv_sem=recv_sems.at[k],
                                        device_id=to, device_id_type=MESH)


def _gather_side(w):
    shape = (N_CHIPS,) + w.shape

    def first(src, land, send, recv):
        x, y, c, chips = _place()
        me = 2 * x + y
        return [_rcopy(src.at[c], land.at[me, c], send, recv, j, (*chip, c)) for j, chip in enumerate(chips)]

    def start(ins, outs, send, recv):
        for cp in first(ins[0], outs[0], send, recv):
            cp.start()

    def finish(ins, outs, send, recv):
        land = outs[0]
        x, y, c, chips = _place()
        sibling = (x, y, 1 - c)
        passed = []
        for j, chip in enumerate(chips):
            s = 2 * chip[0] + chip[1]
            _rcopy(land.at[s, c], land.at[s, c], send, recv, j, (*chip, c)).wait_recv()
            cp = _rcopy(land.at[s, c], land.at[s, c], send, recv, 3 + j, sibling)
            cp.start()
            passed.append(cp)
        for j, chip in enumerate(chips):
            s = 2 * chip[0] + chip[1]
            _rcopy(land.at[s, 1 - c], land.at[s, 1 - c], send, recv, 3 + j, sibling).wait_recv()
        for cp in first(ins[0], land, send, recv) + passed:
            cp.wait_send()

    return _Side([w], [jax.ShapeDtypeStruct(shape, w.dtype)], 6, start, finish)


def _place_own(land, w):
    chip = 2 * lax.axis_index("x") + lax.axis_index("y")
    full = lax.dynamic_update_slice(land, w[None], (chip, 0, 0, 0))
    return full.reshape(N_CHIPS, 2 * w.shape[1], w.shape[2])


def gather_rows(w, name):
    side = _gather_side(w)

    def body(w_ref, o_ref, send, recv):
        side.start([w_ref], [o_ref], send, recv)
        side.finish([w_ref], [o_ref], send, recv)

    return pl.pallas_call(
        body, name=name, in_specs=[_ANY], out_specs=_ANY, out_shape=side.outs[0],
        scratch_shapes=[pltpu.SemaphoreType.DMA((6,)), pltpu.SemaphoreType.DMA((6,))],
    )(w)


def _scatter_side(p):
    def copies(src, dst, send, recv):
        x, y, c, chips = _place()
        return [_rcopy(src.at[2 * chip[0] + chip[1]], dst.at[j], send, recv, j, (*chip, c))
                for j, chip in enumerate(chips)]

    def start(ins, outs, send, recv):
        for cp in copies(ins[0], outs[0], send, recv):
            cp.start()

    def finish(ins, outs, send, recv):
        for cp in copies(ins[0], outs[0], send, recv):
            cp.wait()

    return _Side([p], [jax.ShapeDtypeStruct((3,) + p.shape[1:], p.dtype)], 3, start, finish)


def scatter_rows(p, name):
    side = _scatter_side(p)

    def body(p_ref, o_ref, send, recv):
        side.start([p_ref], [o_ref], send, recv)
        side.finish([p_ref], [o_ref], send, recv)

    return pl.pallas_call(
        body, name=name, in_specs=[_ANY], out_specs=_ANY, out_shape=side.outs[0],
        scratch_shapes=[pltpu.SemaphoreType.DMA((3,)), pltpu.SemaphoreType.DMA((3,))],
    )(p)


def _swap_rows(g, name):
    def body(g_ref, o_ref, send_sems, recv_sems):
        x, y, c, _ = _place()
        cp = _rcopy(g_ref.at[:, 1 - c], o_ref, send_sems, recv_sems, 0, (x, y, 1 - c))
        cp.start()
        cp.wait()

    return pl.pallas_call(
        body, name=name, in_specs=[_ANY], out_specs=_ANY,
        out_shape=jax.ShapeDtypeStruct((g.shape[0],) + g.shape[2:], g.dtype),
        scratch_shapes=[pltpu.SemaphoreType.DMA((1,)), pltpu.SemaphoreType.DMA((1,))],
    )(g)


def _swap_whole(g, name):
    def body(g_ref, o_ref, send_sems, recv_sems):
        x, y, c, _ = _place()
        cp = _rcopy(g_ref, o_ref, send_sems, recv_sems, 0, (x, y, 1 - c))
        cp.start()
        cp.wait()

    return pl.pallas_call(
        body, name=name, in_specs=[_ANY], out_specs=_ANY, out_shape=jax.ShapeDtypeStruct(g.shape, g.dtype),
        scratch_shapes=[pltpu.SemaphoreType.DMA((1,)), pltpu.SemaphoreType.DMA((1,))],
    )(g)


def _ew_call(fn, ins, out_dtypes, name):
    shape = ins[0].shape
    n = shape[-1]
    ins2 = [a.reshape(-1, n) for a in ins]
    rows = ins2[0].shape[0]
    tr = _pick(rows, (256, 128, 64, 32, 16, 8)) if n <= 2048 else _pick(rows, (128, 64, 32, 16, 8))
    n_in = len(ins2)

    def body(*refs):
        outs = fn(*[r[...] for r in refs[:n_in]])
        for o, v in zip(refs[n_in:], outs):
            o[...] = v.astype(o.dtype)

    spec = pl.BlockSpec((tr, n), lambda i: (i, 0))
    res = pl.pallas_call(
        body, name=name, grid=(rows // tr,), in_specs=[spec] * n_in, out_specs=[spec] * len(out_dtypes),
        out_shape=[jax.ShapeDtypeStruct((rows, n), dt) for dt in out_dtypes],
        compiler_params=_cparams(("parallel",)),
    )(*ins2)
    return [r.reshape(shape) for r in res]


def all_reduce_small(buf, name):
    _, R, Ln = buf.shape

    def body(in_ref, out_ref, recv_ref, send_a, recv_a, send_b, recv_b):
        x, y, c = lax.axis_index("x"), lax.axis_index("y"), lax.axis_index("c")
        me = 4 * x + 2 * y + c
        peers = []
        for r in range(1, 8):
            px, py, pc = x ^ ((r >> 2) & 1), y ^ ((r >> 1) & 1), c ^ (r & 1)
            peers.append((r, (px, py, pc), 4 * px + 2 * py + pc))
        cps = [_rcopy(in_ref.at[idx], recv_ref.at[r], send_a, recv_a, r, to) for r, to, idx in peers]
        for cp in cps:
            cp.start()
        for cp in cps:
            cp.wait()
        acc = in_ref[me]
        for r in range(1, 8):
            acc = acc + recv_ref[r]
        out_ref[me] = acc
        cps = [_rcopy(out_ref.at[me], out_ref.at[me], send_b, recv_b, r, to) for r, to, idx in peers]
        for cp in cps:
            cp.start()
        for (r, to, idx), cp in zip(peers, cps):
            cp.wait_send()
            _rcopy(out_ref.at[idx], out_ref.at[idx], send_b, recv_b, r, to).wait_recv()

    vm = pl.BlockSpec(memory_space=pltpu.VMEM)
    return pl.pallas_call(
        body, name=name, in_specs=[vm], out_specs=vm,
        out_shape=jax.ShapeDtypeStruct(buf.shape, F32),
        scratch_shapes=[pltpu.VMEM(buf.shape, F32)] + [pltpu.SemaphoreType.DMA((8,))] * 4,
        compiler_params=pltpu.CompilerParams(vmem_limit_bytes=VMEM_LIMIT_BYTES),
    )(buf)


def _adamw_math(w, g, m, v):
    m = ADAM_B1 * m + (1.0 - ADAM_B1) * g
    v = ADAM_B2 * v + (1.0 - ADAM_B2) * (g * g)
    m_hat = m / (1.0 - ADAM_B1 ** ADAM_STEP)
    v_hat = v / (1.0 - ADAM_B2 ** ADAM_STEP)
    delta = -ADAM_LR * (m_hat / (jnp.sqrt(v_hat) + ADAM_EPS) + ADAM_WD * w)
    return [delta, m, v]


def adamw(w, g, m, v, name):
    return _ew_call(_adamw_math, [w, g, m, v], [F32, F32, F32], name)


_WEIGHTS = ['norm_w', 'w_in', 's5_lambda_re', 's5_lambda_im', 's5_log_step', 's5_b_re', 's5_b_im', 's5_c_re', 's5_c_im',
            's5_d', 's5_w_glu', 's5_b_glu', 'rg_conv_w', 'rg_conv_b', 'rg_w_a', 'rg_b_a', 'rg_w_x', 'rg_b_x', 'rg_lambda',
            'hg_lower_bounds', 'hg_norm_w', 'w_branch', 'w_out', 'final_norm_w']
_BIG = ('w_in', 's5_w_glu', 'w_branch', 'w_out')
_SMALL = [n for n in _WEIGHTS if n not in _BIG]
_LANES = 128
_N_DEV = 8


def _pack(arrs):
    flat = jnp.concatenate([a.reshape(-1) for a in arrs])
    unit = _N_DEV * 8 * _LANES
    total = -(-flat.shape[0] // unit) * unit
    flat = jnp.pad(flat, (0, total - flat.shape[0]))
    return flat.reshape(_N_DEV, total // (_N_DEV * _LANES), _LANES)


def _unpack(buf, shapes):
    flat = buf.reshape(-1)
    out, off = [], 0
    for s in shapes:
        n = math.prod(s)
        out.append(flat[off:off + n].reshape(s))
        off += n
    return out


def _step(a):
    x_idx, y_idx, c_idx = lax.axis_index("x"), lax.axis_index("y"), lax.axis_index("c")
    chip = 2 * x_idx + y_idx
    L = a["norm_w"].shape[0]
    W = a["s5_d"].shape[1]
    cw = a["rg_conv_w"]
    wc = cw.shape[2]
    placed = lax.dynamic_update_slice(jnp.zeros((L, RG_CONV, W), F32), cw, (0, 0, chip * wc))
    placed = placed * (c_idx == 0).astype(F32)
    conv_full = _unpack(all_reduce_small(_pack([placed]), "gather_conv"), [(L, RG_CONV, W)])[0]
    wts = {n: a[n] for n in _SMALL}
    wts["rg_conv_w"] = conv_full
    lbs, vjp_lbs = jax.vjp(_lower_bounds, a["hg_lower_bounds"])
    D = a["w_in"].shape[1]
    w_in = a["w_in"].astype(BF16).reshape(L, 2, D // 2, a["w_in"].shape[2])
    rest = [_pack_rest(a["s5_w_glu"][l].astype(BF16), a["w_branch"][l].astype(BF16), a["w_out"][l].astype(BF16))
            for l in range(L)]
    x = a["x"][0]
    land_w = gather_rows(w_in[0], "gather_w_in_first")
    land_r = gather_rows(rest[0], "gather_rest_first")
    saved = []
    for l in range(L):
        last = l + 1 == L
        h, vjp_rms = jax.vjp(_rms_bf16, x, a["norm_w"][l])
        w_l = _place_own(land_w, w_in[l])
        if last:
            z = _mm_nn(h, w_l, None, "w_in_fwd", BF16)
        else:
            z, land_w = _mm_nn(h, w_l, None, "w_in_fwd_gather", BF16, side=_gather_side(w_in[l + 1]))
        lw = {n: wts[n][l] for n in _LAYER_SMALL}
        lw["lbs"] = lbs[l]
        r_l = lax.dynamic_update_slice(land_r, rest[l][None], (chip, 0, 0, 0))
        x_next, vjp_tail, land_r = jax.vjp(functools.partial(_layer_tail, nxt=None if last else rest[l + 1]),
                                           z, x, lw, r_l, has_aux=True)
        saved.append((h, w_l, vjp_rms, vjp_tail))
        x = x_next
    rows, vjp_loss = jax.vjp(_loss_rows, x, a["final_norm_w"], a["loss_target"][0])
    loss = lax.psum(jnp.sum(rows), ("x", "y", "c"))
    dx, d_final, _ = vjp_loss(jnp.ones_like(rows))
    gw = {n: [None] * L for n in _LAYER_SMALL + ["lbs", "norm_w"]}
    pairs = {"w": [None] * L, "r": [None] * L}
    recvs = {"w": [None] * L, "r": [None] * L}

    def pair_sum(g, tag):
        got = _swap_rows(g, "reduce_" + tag + "_swap")
        mine = lax.dynamic_index_in_dim(g, c_idx, 1, keepdims=False)
        return _ew_call(lambda p, q: [p.astype(F32) + q.astype(F32)], [mine, got], [BF16], "reduce_" + tag + "_pair")[0]

    for l in reversed(range(L)):
        h, w_l, vjp_rms, vjp_tail = saved[l]
        dz, dx_res, dlw, d_rest = vjp_tail(dx)
        if l + 1 == L:
            dh = _mm_nt(dz, w_l, h.dtype, "w_in_bwd_a")
            dw = _mm_tn(h, dz, N_CHIPS, BF16, "w_in_bwd_w")
        else:
            dh, recvs["r"][l + 1] = _mm_nt(dz, w_l, h.dtype, "w_in_bwd_a_scatter", side=_scatter_side(pairs["r"][l + 1]))
            dw, recvs["w"][l + 1] = _mm_tn(h, dz, N_CHIPS, BF16, "w_in_bwd_w_scatter", side=_scatter_side(pairs["w"][l + 1]))
        dx_rms, gw["norm_w"][l] = vjp_rms(dh)
        dx = dx_rms + dx_res
        for n in _LAYER_SMALL + ["lbs"]:
            gw[n][l] = dlw[n]
        pairs["w"][l] = pair_sum(dw.reshape(N_CHIPS, 2, D // 2, dw.shape[2]), "w_in")
        pairs["r"][l] = pair_sum(d_rest, "rest")
    recvs["w"][0] = scatter_rows(pairs["w"][0], "reduce_w_in_scatter")
    recvs["r"][0] = scatter_rows(pairs["r"][0], "reduce_rest_scatter")
    south = c_idx == 0

    def finish(tag, name):
        halves = []
        for l in range(L):
            own = lax.dynamic_index_in_dim(pairs[tag][l], chip, 0, keepdims=False)
            r = recvs[tag][l]
            halves.append(_ew_call(
                lambda p, r0, r1, r2: [((p.astype(F32) + r0.astype(F32)) + r1.astype(F32)) + r2.astype(F32)],
                [own, r[0], r[1], r[2]], [F32], "reduce_" + name + "_sum")[0])
        gh = jnp.stack(halves, 0)
        other = _swap_whole(gh, "reduce_" + name + "_join")
        return jnp.stack([jnp.where(south, gh, other), jnp.where(south, other, gh)], 1)

    grads = {"w_in": finish("w", "w_in").reshape(a["w_in"].shape)}
    g_rest = _unpack_rest(finish("r", "rest"), W, D)
    grads["w_branch"] = jnp.stack(g_rest["w_branch"], 1)
    grads["w_out"] = g_rest["w_out"]
    grads["s5_w_glu"] = g_rest["s5_w_glu"]
    gx = dx
    (d_lb,) = vjp_lbs(jnp.stack(gw["lbs"], 0))
    gw = {n: jnp.stack(v, 0) for n, v in gw.items()}
    gw["hg_lower_bounds"] = d_lb
    gw["final_norm_w"] = d_final
    small_shapes = [gw[n].shape for n in _SMALL]
    red = _unpack(all_reduce_small(_pack([gw[n].astype(F32) for n in _SMALL]), "reduce_small"), small_shapes)
    for n, g in zip(_SMALL, red):
        grads[n] = g
    grads["rg_conv_w"] = lax.dynamic_slice_in_dim(grads["rg_conv_w"], chip * wc, wc, 2)
    delta, new_m, new_v = {}, {}, {}
    for n in _BIG:
        delta[n], new_m[n], new_v[n] = adamw(a[n], grads[n], a["m_" + n], a["v_" + n], "adamw_" + n)
    shapes = [a[n].shape for n in _SMALL]
    packed = [_pack([t[n] for n in _SMALL]) for t in
              (a, grads, {n: a["m_" + n] for n in _SMALL}, {n: a["v_" + n] for n in _SMALL})]
    for dst, buf in zip((delta, new_m, new_v), adamw(*packed, "adamw_small")):
        for n, t in zip(_SMALL, _unpack(buf, shapes)):
            dst[n] = t
    return (loss, gx[None], *[grads[n] for n in _WEIGHTS], *[delta[n] for n in _WEIGHTS],
            *[new_m[n] for n in _WEIGHTS], *[new_v[n] for n in _WEIGHTS])


_ARG_NAMES = ["x"] + _WEIGHTS + ["loss_target"] + ["m_" + n for n in _WEIGHTS] + ["v_" + n for n in _WEIGHTS]


def kernel(x, norm_w, w_in, s5_lambda_re, s5_lambda_im, s5_log_step, s5_b_re, s5_b_im, s5_c_re, s5_c_im, s5_d, s5_w_glu, s5_b_glu, rg_conv_w, rg_conv_b, rg_w_a, rg_b_a, rg_w_x, rg_b_x, rg_lambda, hg_lower_bounds, hg_norm_w, w_branch, w_out, final_norm_w, loss_target, m_norm_w, m_w_in, m_s5_lambda_re, m_s5_lambda_im, m_s5_log_step, m_s5_b_re, m_s5_b_im, m_s5_c_re, m_s5_c_im, m_s5_d, m_s5_w_glu, m_s5_b_glu, m_rg_conv_w, m_rg_conv_b, m_rg_w_a, m_rg_b_a, m_rg_w_x, m_rg_b_x, m_rg_lambda, m_hg_lower_bounds, m_hg_norm_w, m_w_branch, m_w_out, m_final_norm_w, v_norm_w, v_w_in, v_s5_lambda_re, v_s5_lambda_im, v_s5_log_step, v_s5_b_re, v_s5_b_im, v_s5_c_re, v_s5_c_im, v_s5_d, v_s5_w_glu, v_s5_b_glu, v_rg_conv_w, v_rg_conv_b, v_rg_w_a, v_rg_b_a, v_rg_w_x, v_rg_b_x, v_rg_lambda, v_hg_lower_bounds, v_hg_norm_w, v_w_branch, v_w_out, v_final_norm_w):
    vals = (x, norm_w, w_in, s5_lambda_re, s5_lambda_im, s5_log_step, s5_b_re, s5_b_im, s5_c_re, s5_c_im, s5_d, s5_w_glu, s5_b_glu, rg_conv_w, rg_conv_b, rg_w_a, rg_b_a, rg_w_x, rg_b_x, rg_lambda, hg_lower_bounds, hg_norm_w, w_branch, w_out, final_norm_w, loss_target, m_norm_w, m_w_in, m_s5_lambda_re, m_s5_lambda_im, m_s5_log_step, m_s5_b_re, m_s5_b_im, m_s5_c_re, m_s5_c_im, m_s5_d, m_s5_w_glu, m_s5_b_glu, m_rg_conv_w, m_rg_conv_b, m_rg_w_a, m_rg_b_a, m_rg_w_x, m_rg_b_x, m_rg_lambda, m_hg_lower_bounds, m_hg_norm_w, m_w_branch, m_w_out, m_final_norm_w, v_norm_w, v_w_in, v_s5_lambda_re, v_s5_lambda_im, v_s5_log_step, v_s5_b_re, v_s5_b_im, v_s5_c_re, v_s5_c_im, v_s5_d, v_s5_w_glu, v_s5_b_glu, v_rg_conv_w, v_rg_conv_b, v_rg_w_a, v_rg_b_a, v_rg_w_x, v_rg_b_x, v_rg_lambda, v_hg_lower_bounds, v_hg_norm_w, v_w_branch, v_w_out, v_final_norm_w)
    return _step(dict(zip(_ARG_NAMES, vals)))
```

```python
import functools
import math

import jax
import jax.numpy as jnp
from jax import lax
from jax.experimental import pallas as pl
from jax.experimental.pallas import tpu as pltpu

F32 = jnp.float32
BF16 = jnp.bfloat16
EPS = 1e-6
RG_C = 8.0
S5_GROUP = 16
S5_STATE = 64
RG_BLOCKS = 16
RG_CONV = 4
HG_HEADS = 8
N_CHIPS = 4
VMEM_LIMIT_BYTES = 56 * 1024 * 1024
NEG_BIG = -1e30

ADAM_LR = 0.001
ADAM_B1 = 0.9
ADAM_B2 = 0.999
ADAM_EPS = 1e-08
ADAM_WD = 0.01
ADAM_STEP = 10

MESH = pl.DeviceIdType.MESH


def _cparams(sem):
    return pltpu.CompilerParams(dimension_semantics=sem, vmem_limit_bytes=VMEM_LIMIT_BYTES)


_DOT_DIMS = {"nn": (((1,), (0,)), ((), ())), "nt": (((1,), (1,)), ((), ())), "tn": (((0,), (0,)), ((), ()))}


def _bdot_raw(a, b, form):
    return lax.dot_general(a.astype(BF16), b.astype(BF16), _DOT_DIMS[form], preferred_element_type=F32)


@functools.partial(jax.custom_vjp, nondiff_argnums=(2,))
def bdot(a, b, form):
    return _bdot_raw(a, b, form)


def _bdot_fwd(a, b, form):
    return _bdot_raw(a, b, form), (a, b)


def _bdot_bwd(form, res, g):
    a, b = res
    if form == "nn":
        da, db = _bdot_raw(g, b, "nt"), _bdot_raw(a, g, "tn")
    elif form == "nt":
        da, db = _bdot_raw(g, b, "nn"), _bdot_raw(g, a, "tn")
    else:
        da, db = _bdot_raw(b, g, "nt"), _bdot_raw(a, g, "nn")
    return da.astype(a.dtype), db.astype(b.dtype)


bdot.defvjp(_bdot_fwd, _bdot_bwd)


@functools.partial(jax.custom_vjp, nondiff_argnums=(1,))
def sroll(x, d):
    return pltpu.roll(x, d, 0)


def _sroll_fwd(x, d):
    return pltpu.roll(x, d, 0), None


def _sroll_bwd(d, _, g):
    return (pltpu.roll(g, g.shape[0] - d, 0),)


sroll.defvjp(_sroll_fwd, _sroll_bwd)


def _row_iota(n):
    return lax.broadcasted_iota(jnp.int32, (n, 1), 0)


def _last_row(x, ti):
    return jnp.sum(jnp.where(ti == x.shape[0] - 1, x, 0.0), axis=0, keepdims=True)


def _silu(x):
    return x * jax.nn.sigmoid(x)


class _Tiling:
    def __init__(self, CB, tT, nT, ncb=None, bmap=None):
        self.CB, self.tT, self.nT = CB, tT, nT
        self.ncb = CB if ncb is None else ncb
        self.bmap = (lambda cb, t: (t, cb)) if bmap is None else bmap


def _tiled_specs(params, views, carry_shapes, out_defs, tl, rev):
    nT = tl.nT
    tmap = (lambda t: nT - 1 - t) if rev else (lambda t: t)

    def rspec(col0, width):
        bw = width // tl.ncb
        off = col0 // bw

        def imap(cb, t):
            rb, cbk = tl.bmap(cb, tmap(t))
            return (rb, off + cbk)

        return pl.BlockSpec((tl.tT, bw), imap)

    p_specs = [pl.BlockSpec((None,) + p.shape[1:], lambda cb, t: (cb, 0, 0)) for p in params]
    r_specs = [rspec(c0, w) for (c0, w) in views]
    dr_specs = [rspec(0, w) for (_, w) in views]
    o_specs = [rspec(0, w) for (w, _) in out_defs]
    s_specs = [pl.BlockSpec((None, r, c), lambda cb, t: (tmap(t), 0, cb)) for (r, c) in carry_shapes]
    return p_specs, r_specs, dr_specs, o_specs, s_specs


def _tiled_fwd(f, name, params, rows, views, carry_shapes, out_defs, tl, side=None):
    T = rows[0].shape[0]
    CB, tT, nT = tl.CB, tl.tT, tl.nT
    n_p, n_r, n_o, n_c = len(params), len(rows), len(out_defs), len(carry_shapes)
    p_specs, r_specs, _, o_specs, s_specs = _tiled_specs(params, views, carry_shapes, out_defs, tl, False)
    x_ins, x_outs, x_sems = _side_parts(side)
    n_xi, n_xo = len(x_ins), len(x_outs)

    def body(*refs):
        i = 0
        p_refs = refs[i:i + n_p]; i += n_p
        r_refs = refs[i:i + n_r]; i += n_r
        xi_refs = refs[i:i + n_xi]; i += n_xi
        o_refs = refs[i:i + n_o]; i += n_o
        s_refs = refs[i:i + n_c]; i += n_c
        xo_refs = refs[i:i + n_xo]; i += n_xo
        c_refs = refs[i:i + n_c]
        cb, t = pl.program_id(0), pl.program_id(1)
        if side is not None:
            @pl.when((cb == 0) & (t == 0))
            def _():
                side.start(xi_refs, xo_refs, refs[-2], refs[-1])

        @pl.when(t == 0)
        def _():
            for c in c_refs:
                c[...] = jnp.zeros_like(c)

        carries = [c[...] for c in c_refs]
        for s, cv in zip(s_refs, carries):
            s[...] = cv
        outs, newc = f([p[...] for p in p_refs], [r[...].astype(F32) for r in r_refs], carries, t * tT)
        for o, v in zip(o_refs, outs):
            o[...] = v.astype(o.dtype)
        for c, v in zip(c_refs, newc):
            c[...] = v
        if side is not None:
            @pl.when((cb == CB - 1) & (t == nT - 1))
            def _():
                side.finish(xi_refs, xo_refs, refs[-2], refs[-1])

    out_shape = [jax.ShapeDtypeStruct((T, w), dt) for (w, dt) in out_defs]
    out_shape += [jax.ShapeDtypeStruct((nT, r, c * CB), F32) for (r, c) in carry_shapes]
    res = pl.pallas_call(
        body, name=name + "_fwd", grid=(CB, nT),
        in_specs=p_specs + r_specs + [_ANY] * n_xi, out_specs=o_specs + s_specs + [_ANY] * n_xo,
        out_shape=out_shape + x_outs,
        scratch_shapes=[pltpu.VMEM((r, c), F32) for (r, c) in carry_shapes] + x_sems,
        compiler_params=_cparams(("arbitrary", "arbitrary")),
    )(*params, *rows, *x_ins)
    return list(res[:n_o]), list(res[n_o:n_o + n_c]), list(res[n_o + n_c:])


def _tiled_bwd(f, name, params, rows, views, saved, douts, carry_shapes, out_defs, tl):
    T = rows[0].shape[0]
    CB, tT, nT = tl.CB, tl.tT, tl.nT
    n_p, n_r, n_o, n_c = len(params), len(rows), len(out_defs), len(carry_shapes)
    p_specs, r_specs, dr_specs, o_specs, s_specs = _tiled_specs(params, views, carry_shapes, out_defs, tl, True)
    out_dtypes = [dt for (_, dt) in out_defs]

    def body(*refs):
        i = 0
        p_refs = refs[i:i + n_p]; i += n_p
        r_refs = refs[i:i + n_r]; i += n_r
        s_refs = refs[i:i + n_c]; i += n_c
        g_refs = refs[i:i + n_o]; i += n_o
        dp_refs = refs[i:i + n_p]; i += n_p
        dr_refs = refs[i:i + n_r]; i += n_r
        dc_refs = refs[i:]
        t = pl.program_id(1)

        @pl.when(t == 0)
        def _():
            for c in dc_refs:
                c[...] = jnp.zeros_like(c)
            for d in dp_refs:
                d[...] = jnp.zeros_like(d)

        t0 = (nT - 1 - t) * tT

        def g(P, R, C):
            outs, newc = f(P, R, C, t0)
            return [o.astype(dt) for o, dt in zip(outs, out_dtypes)], list(newc)

        _, vjp = jax.vjp(g, [p[...] for p in p_refs], [r[...].astype(F32) for r in r_refs], [s[...] for s in s_refs])
        dP, dR, dC = vjp(([gr[...] for gr in g_refs], [c[...] for c in dc_refs]))
        for d, v in zip(dp_refs, dP):
            d[...] += v
        for d, v in zip(dr_refs, dR):
            d[...] = v.astype(d.dtype)
        for c, v in zip(dc_refs, dC):
            c[...] = v

    out_shape = [jax.ShapeDtypeStruct(p.shape, F32) for p in params]
    out_shape += [jax.ShapeDtypeStruct((T, w), r.dtype) for r, (_, w) in zip(rows, views)]
    res = pl.pallas_call(
        body, name=name + "_bwd", grid=(CB, nT),
        in_specs=p_specs + r_specs + s_specs + o_specs, out_specs=p_specs + dr_specs, out_shape=out_shape,
        scratch_shapes=[pltpu.VMEM((r, c), F32) for (r, c) in carry_shapes],
        compiler_params=_cparams(("arbitrary", "arbitrary")),
    )(*params, *rows, *saved, *douts)
    return list(res[:n_p]), list(res[n_p:])


def tiled_op(f, name, params, rows, carry_shapes, out_defs, CB, tT, tiling=None, side_of=None):
    arrs = [r[0] if isinstance(r, tuple) else r for r in rows]
    views = [(r[1], r[2]) if isinstance(r, tuple) else (0, r.shape[1]) for r in rows]
    tl = tiling if tiling is not None else _Tiling(CB, tT, arrs[0].shape[0] // tT)
    make_side, side_arrs = side_of if side_of is not None else (None, [])

    def run_fwd(params, arrs, side_arrs):
        side = make_side(*side_arrs) if make_side is not None else None
        return _tiled_fwd(f, name, params, arrs, views, carry_shapes, out_defs, tl, side)

    @jax.custom_vjp
    def op(params, arrs, side_arrs):
        outs, _, extra = run_fwd(params, arrs, side_arrs)
        return outs, extra

    def op_fwd(params, arrs, side_arrs):
        outs, saved, extra = run_fwd(params, arrs, side_arrs)
        return (outs, extra), (params, arrs, saved, side_arrs)

    def op_bwd(res, cts):
        params, arrs, saved, side_arrs = res
        douts, _ = cts
        dP, dR = _tiled_bwd(f, name, params, arrs, views, saved, list(douts), carry_shapes, out_defs, tl)
        dR = [d if w == a.shape[1] else jnp.pad(d, ((0, 0), (c0, a.shape[1] - c0 - w)))
              for d, a, (c0, w) in zip(dR, arrs, views)]
        return dP, dR, [jnp.zeros_like(s) for s in side_arrs]

    op.defvjp(op_fwd, op_bwd)
    outs, extra = op(list(params), arrs, list(side_arrs))
    return outs if side_of is None else (outs, extra)


def _pick(n, pref):
    for t in pref:
        if n % t == 0:
            return t
    return n


class _Side:
    def __init__(self, ins, outs, n_sems, start, finish):
        self.ins, self.outs, self.n_sems, self.start, self.finish = ins, outs, n_sems, start, finish


def _side_parts(side):
    if side is None:
        return [], [], []
    sems = [pltpu.SemaphoreType.DMA((side.n_sems,)), pltpu.SemaphoreType.DMA((side.n_sems,))]
    return list(side.ins), list(side.outs), sems


def _mm_nn(a, w, res, name, out_dtype=F32, side=None):
    M, K = a.shape
    S, _, Ns = w.shape
    tm = _pick(M, (1024, 512, 256, 128))
    tn = _pick(Ns, (512, 256, 128))
    nps = Ns // tn
    has_res = res is not None
    n_main = 3 if has_res else 2
    s_ins, s_outs, s_sems = _side_parts(side)
    ni, nj = M // tm, S * nps

    def body(*refs):
        a_ref, w_ref = refs[0], refs[1]
        o_ref = refs[n_main + len(s_ins)]
        if side is not None:
            si = refs[n_main:n_main + len(s_ins)]
            so = refs[n_main + len(s_ins) + 1:n_main + len(s_ins) + 1 + len(s_outs)]
            send, recv = refs[-2], refs[-1]
            i, j = pl.program_id(0), pl.program_id(1)

            @pl.when((i == 0) & (j == 0))
            def _():
                side.start(si, so, send, recv)

        acc = _bdot_raw(a_ref[...], w_ref[...], "nn")
        if has_res:
            acc = acc + refs[2][...]
        o_ref[...] = acc.astype(o_ref.dtype)
        if side is not None:
            @pl.when((i == ni - 1) & (j == nj - 1))
            def _():
                side.finish(si, so, send, recv)

    in_specs = [pl.BlockSpec((tm, K), lambda i, j: (i, 0)),
                pl.BlockSpec((None, K, tn), lambda i, j: (j // nps, 0, j % nps))]
    args = [a, w]
    if has_res:
        in_specs.append(pl.BlockSpec((tm, tn), lambda i, j: (i, j)))
        args.append(res)
    out = pl.pallas_call(
        body, name=name, grid=(ni, nj), in_specs=in_specs + [_ANY] * len(s_ins),
        out_specs=[pl.BlockSpec((tm, tn), lambda i, j: (i, j))] + [_ANY] * len(s_outs),
        out_shape=[jax.ShapeDtypeStruct((M, S * Ns), out_dtype)] + s_outs,
        scratch_shapes=s_sems,
        compiler_params=_cparams(("arbitrary", "arbitrary")),
    )(*args, *s_ins)
    return out[0] if side is None else out


def _mm_nt(g, w, out_dtype, name, side=None):
    M, N = g.shape
    S, K, Ns = w.shape
    tm = _pick(M, (1024, 512, 256, 128))
    tk = _pick(K, (1024, 512, 256, 128))
    tn = _pick(Ns, (1792, 1024, 512, 256, 128))
    nps = Ns // tn
    nn = S * nps
    ni, nk = M // tm, K // tk
    s_ins, s_outs, s_sems = _side_parts(side)

    def body(*refs):
        g_ref, w_ref = refs[0], refs[1]
        o_ref = refs[2 + len(s_ins)]
        acc_ref = refs[3 + len(s_ins) + len(s_outs)]
        i, k, n = pl.program_id(0), pl.program_id(1), pl.program_id(2)
        if side is not None:
            si = refs[2:2 + len(s_ins)]
            so = refs[3 + len(s_ins):3 + len(s_ins) + len(s_outs)]

            @pl.when((i == 0) & (k == 0) & (n == 0))
            def _():
                side.start(si, so, refs[-2], refs[-1])

        @pl.when(n == 0)
        def _():
            acc_ref[...] = jnp.zeros_like(acc_ref)

        acc_ref[...] += _bdot_raw(g_ref[...], w_ref[...], "nt")

        @pl.when(n == nn - 1)
        def _():
            o_ref[...] = acc_ref[...].astype(o_ref.dtype)

        if side is not None:
            @pl.when((i == ni - 1) & (k == nk - 1) & (n == nn - 1))
            def _():
                side.finish(si, so, refs[-2], refs[-1])

    out = pl.pallas_call(
        body, name=name, grid=(ni, nk, nn),
        in_specs=[pl.BlockSpec((tm, tn), lambda i, k, n: (i, n)),
                  pl.BlockSpec((None, tk, tn), lambda i, k, n: (n // nps, k, n % nps))] + [_ANY] * len(s_ins),
        out_specs=[pl.BlockSpec((tm, tk), lambda i, k, n: (i, k))] + [_ANY] * len(s_outs),
        out_shape=[jax.ShapeDtypeStruct((M, K), out_dtype)] + s_outs,
        scratch_shapes=[pltpu.VMEM((tm, tk), F32)] + s_sems,
        compiler_params=_cparams(("arbitrary", "arbitrary", "arbitrary")),
    )(g, w, *s_ins)
    return out[0] if side is None else out


def _mm_tn(a, g, S, out_dtype, name, side=None):
    T, K = a.shape
    N = g.shape[1]
    Ns = N // S
    tk = _pick(K, (2048, 1024, 512, 256, 128))
    tn = _pick(Ns, (1024, 896, 512, 256, 128))
    tt = _pick(T, (1024, 512, 256, 128))
    nps = Ns // tn
    nt = T // tt
    nk, nj = K // tk, S * nps
    a_t = a.astype(BF16).T
    s_ins, s_outs, s_sems = _side_parts(side)

    def body(*refs):
        a_ref, g_ref = refs[0], refs[1]
        o_ref = refs[2 + len(s_ins)]
        acc_ref = refs[3 + len(s_ins) + len(s_outs)]
        k, j, t = pl.program_id(0), pl.program_id(1), pl.program_id(2)
        if side is not None:
            si = refs[2:2 + len(s_ins)]
            so = refs[3 + len(s_ins):3 + len(s_ins) + len(s_outs)]
            send, recv = refs[-2], refs[-1]

            @pl.when((k == 0) & (j == 0) & (t == 0))
            def _():
                side.start(si, so, send, recv)

        @pl.when(t == 0)
        def _():
            acc_ref[...] = jnp.zeros_like(acc_ref)

        acc_ref[...] += _bdot_raw(a_ref[...], g_ref[...], "nn")

        @pl.when(t == nt - 1)
        def _():
            o_ref[...] = acc_ref[...].astype(o_ref.dtype)

        if side is not None:
            @pl.when((k == nk - 1) & (j == nj - 1) & (t == nt - 1))
            def _():
                side.finish(si, so, send, recv)

    out = pl.pallas_call(
        body, name=name, grid=(nk, nj, nt),
        in_specs=[pl.BlockSpec((tk, tt), lambda k, j, t: (k, t)),
                  pl.BlockSpec((tt, tn), lambda k, j, t: (t, j))] + [_ANY] * len(s_ins),
        out_specs=[pl.BlockSpec((None, tk, tn), lambda k, j, t: (j // nps, k, j % nps))] + [_ANY] * len(s_outs),
        out_shape=[jax.ShapeDtypeStruct((S, K, Ns), out_dtype)] + s_outs,
        scratch_shapes=[pltpu.VMEM((tk, tn), F32)] + s_sems,
        compiler_params=_cparams(("arbitrary", "arbitrary", "arbitrary")),
    )(a_t, g, *s_ins)
    return out[0] if side is None else out


def linear(a, w, name, res=None, out_dtype=F32):
    @jax.custom_vjp
    def op(a, w, res):
        return _mm_nn(a, w, res, name + "_fwd", out_dtype)

    def op_fwd(a, w, res):
        return _mm_nn(a, w, res, name + "_fwd", out_dtype), (a, w)

    def op_bwd(saved, g):
        a, w = saved
        da = _mm_nt(g, w, a.dtype, name + "_bwd_a")
        dw = _mm_tn(a, g, w.shape[0], w.dtype, name + "_bwd_w")
        return da, dw, (None if res is None else g)

    op.defvjp(op_fwd, op_bwd)
    return op(a, w, res)


def _rms_tile(params, rows, carries, t0):
    (w,), (x,) = params, rows
    y = x * lax.rsqrt(jnp.mean(x * x, axis=-1, keepdims=True) + EPS) * w
    return [y], []


def _s5_tile(params, rows, carries, t0):
    t_m, s_re, s_im, r_re, r_im, apow_re, apow_im, d = params
    (u,) = rows
    n = u.shape[0]
    ti = _row_iota(n)
    pi = _row_iota(apow_re.shape[0])
    x_re, x_im = bdot(u, s_re, "nn"), bdot(u, s_im, "nn")
    k = 0
    while (1 << k) < n:
        sh = 1 << k
        p_re, p_im = _row_sel(apow_re, pi, k), _row_sel(apow_im, pi, k)
        q_re, q_im = sroll(x_re, sh), sroll(x_im, sh)
        m = ti >= sh
        x_re, x_im = (x_re + jnp.where(m, p_re * q_re - p_im * q_im, 0.0),
                      x_im + jnp.where(m, p_re * q_im + p_im * q_re, 0.0))
        k += 1
    x_re = jnp.where(ti >= 1, sroll(x_re, 1), 0.0)
    x_im = jnp.where(ti >= 1, sroll(x_im, 1), 0.0)
    y = bdot(u, t_m, "nn") + bdot(x_re, r_re, "nn") + bdot(x_im, r_im, "nn") + d * u
    return [jax.nn.gelu(y)], []


def _glu_tile(params, rows, carries, t0):
    (b,), (y, zg, ga) = params, rows
    return [y * jax.nn.sigmoid(zg + b) * _silu(ga)], []


def _neg_expm1(z):
    small = -(z * (1.0 + z * (0.5 + z * (1.0 / 6.0))))
    return jnp.where(z > -0.01, small, 1.0 - jnp.exp(z))


def _rg_tile(params, rows, carries, t0):
    conv_w, conv_b, w_a, b_a, w_x, b_x, sp = params
    x, gate = rows
    x_prev, h_prev = carries
    tT = x.shape[0]
    ti = _row_iota(tT)
    ci = _row_iota(RG_CONV)
    xc = _row_sel(conv_w, ci, RG_CONV - 1) * x + conv_b
    for k in range(1, RG_CONV):
        xs = jnp.where(ti >= k, sroll(x, k), sroll(x_prev, k))
        xc = xc + _row_sel(conv_w, ci, RG_CONV - 1 - k) * xs
    r = jax.nn.sigmoid(bdot(xc, w_a, "nn") + b_a)
    i = jax.nn.sigmoid(bdot(xc, w_x, "nn") + b_x)
    log_a = -RG_C * r * sp
    a = jnp.exp(log_a)
    mult = jnp.sqrt(_neg_expm1(2.0 * log_a))
    mult = jnp.where(ti + t0 == 0, 1.0, mult)
    b = mult * (i * xc)
    b = b + jnp.where(ti == 0, a * h_prev, 0.0)
    k = 1
    while k < tT:
        m = ti >= k
        b = b + jnp.where(m, a * sroll(b, k), 0.0)
        a = jnp.where(m, a * sroll(a, k), a)
        k *= 2
    return [b * _silu(gate)], [x, _last_row(b, ti)]


def _hg_tile(params, rows, carries, t0):
    lb, nw = params
    q, fl, v, gate = rows
    (st,) = carries
    tT = q.shape[0]
    ti = _row_iota(tT)
    qs = _silu(q)
    f = lb + (1.0 - lb) * jax.nn.sigmoid(fl)
    kk = 1.0 - f
    G = jnp.log(f)
    k = 1
    while k < tT:
        G = G + jnp.where(ti >= k, sroll(G, k), 0.0)
        k *= 2
    inter = bdot(qs * jnp.exp(G), st, "nt")
    tr = lax.broadcasted_iota(jnp.int32, (tT, tT), 0)
    sc = lax.broadcasted_iota(jnp.int32, (tT, tT), 1)
    attn = jnp.zeros((tT, tT), F32)
    blk = tT
    while blk > 8:
        sub = blk // 4
        for j in range(1, 4):
            ref = jnp.zeros_like(G)
            for b in range(tT // blk):
                row = _row_sel(G, ti, b * blk + sub * j - 1)
                ref = ref + jnp.where(_div2(ti, blk) == b, row, 0.0)
            tmask = _div2(_mod2(ti, blk), sub) == j
            smask = _mod2(ti, blk) < sub * j
            qt = qs * jnp.exp(jnp.where(tmask, G - ref, NEG_BIG))
            kt = kk * jnp.exp(jnp.where(smask, ref - G, NEG_BIG))
            aj = bdot(qt, kt, "nt")
            attn = attn + jnp.where(_div2(tr, blk) == _div2(sc, blk), aj, 0.0)
        blk = sub
    intra = bdot(attn, v, "nn")
    for d in range(blk):
        if d == 0:
            kd, gd, vd = kk, G, v
        else:
            kd, gd, vd = sroll(kk, d), sroll(G, d), sroll(v, d)
        m = _mod2(ti, blk) >= d
        w = jnp.sum(qs * kd * jnp.exp(jnp.where(m, G - gd, NEG_BIG)), axis=1, keepdims=True)
        intra = intra + w * vd
    o = inter + intra
    g_last = _last_row(G, ti)
    k_dec = kk * jnp.exp(g_last - G)
    st_new = st * jnp.exp(g_last) + bdot(v, k_dec, "tn")
    o = o * lax.rsqrt(jnp.mean(o * o, axis=-1, keepdims=True) + EPS) * nw
    return [o * _silu(gate)], [st_new]


def _row_sel(x, ti, r):
    return jnp.sum(jnp.where(ti == r, x, 0.0), axis=0, keepdims=True)


def _div2(i, p):
    return lax.shift_right_logical(i, jnp.int32(p.bit_length() - 1))


def _mod2(i, p):
    return lax.bitwise_and(i, jnp.int32(p - 1))


def _merge_tile(params, rows, carries, t0):
    b0, b1, b2, g0, g1, g2 = rows
    m = jax.nn.sigmoid(g0) * b0 + jax.nn.sigmoid(g1) * b1 + jax.nn.sigmoid(g2) * b2
    return [m], []


def _loss_tile(params, rows, carries, t0):
    (w,), (x, tgt) = params, rows
    y = x * lax.rsqrt(jnp.mean(x * x, axis=-1, keepdims=True) + EPS) * w
    e = y - tgt
    return [0.5 * jnp.mean(e * e, axis=-1, keepdims=True)], []


def _block_diag(w, cb):
    n, i, j = w.shape
    g = n // cb
    w4 = w.reshape(cb, g, i, j)
    eye = jnp.eye(g, dtype=w.dtype)
    return jnp.einsum("cgij,gk->cgikj", w4, eye).reshape(cb, g * i, g * j)


def _s5_params(lam_re, lam_im, log_step, b_re, b_im, c_re, c_im, d, levels):
    G, P = lam_re.shape
    step = jnp.exp(log_step)[:, None]
    mag = jnp.exp(lam_re * step)
    ang = lam_im * step
    abar_re = mag * jnp.cos(ang)
    abar_im = mag * jnp.sin(ang)
    num_re = abar_re - 1.0
    num_im = abar_im
    den = lam_re * lam_re + lam_im * lam_im
    coef_re = (num_re * lam_re + num_im * lam_im) / den
    coef_im = (num_im * lam_re - num_re * lam_im) / den
    bbar_re = coef_re[..., None] * b_re - coef_im[..., None] * b_im
    bbar_im = coef_re[..., None] * b_im + coef_im[..., None] * b_re
    H = b_re.shape[2]
    Lc = S5_LC
    hi = lax.Precision.HIGHEST

    def powers(ks):
        ks = jnp.asarray(ks, F32)[:, None, None]
        m = jnp.exp(ks * (lam_re * step))
        return m * jnp.cos(ks * ang), m * jnp.sin(ks * ang)

    pw_re, pw_im = powers(list(range(Lc + 1)))
    ab_re = pw_re[..., None] * bbar_re - pw_im[..., None] * bbar_im
    ab_im = pw_re[..., None] * bbar_im + pw_im[..., None] * bbar_re
    kern = (jnp.einsum("gap,kgph->gkha", c_re, ab_re[:Lc], precision=hi)
            - jnp.einsum("gap,kgph->gkha", c_im, ab_im[:Lc], precision=hi))
    kk = jnp.arange(Lc)[:, None, None]
    jj = jnp.arange(Lc)[None, :, None]
    ii = jnp.arange(Lc)[None, None, :]
    place = (ii - jj == kk).astype(F32)
    t_m = jnp.einsum("gkha,kji->gjhia", kern, place, precision=hi).reshape(G, Lc * H, Lc * H)
    s_re = ab_re[:Lc][::-1].transpose(1, 0, 3, 2).reshape(G, Lc * H, P)
    s_im = ab_im[:Lc][::-1].transpose(1, 0, 3, 2).reshape(G, Lc * H, P)
    m_re = c_re[None] * pw_re[1:, :, None, :] - c_im[None] * pw_im[1:, :, None, :]
    m_im = c_re[None] * pw_im[1:, :, None, :] + c_im[None] * pw_re[1:, :, None, :]
    r_re = m_re.transpose(1, 3, 0, 2).reshape(G, P, Lc * H)
    r_im = -m_im.transpose(1, 3, 0, 2).reshape(G, P, Lc * H)
    rows = max(8, levels)
    ap_re, ap_im = powers([Lc * (1 << k) for k in range(levels)] + [0] * (rows - levels))
    dd = jnp.tile(d.reshape(G, 1, H), (1, 1, Lc))
    return [t_m, s_re, s_im, r_re, r_im, ap_re.transpose(1, 0, 2), ap_im.transpose(1, 0, 2), dd]


_LANE = 128


def _lane_perm_matrix(lc, h):
    n = lc * _LANE
    src = jnp.arange(n).reshape(lc, _LANE // h, h).transpose(1, 0, 2).reshape(n)
    return (jnp.arange(n)[:, None] == src[None, :]).astype(BF16)


def _lane_perm(x, p, p_t, name):
    @jax.custom_vjp
    def op(x):
        return _mm_nn(x, p[None], None, name, BF16)

    def op_fwd(x):
        return _mm_nn(x, p[None], None, name, BF16), None

    def op_bwd(_, g):
        return (_mm_nn(g, p_t[None], None, name + "_t", x.dtype),)

    op.defvjp(op_fwd, op_bwd)
    return op(x)


def _to_chunks(u, lc, h, perm):
    t, w = u.shape
    x = u.reshape(t // lc, lc, w // _LANE, _LANE).transpose(2, 0, 1, 3).reshape((w // _LANE) * (t // lc), lc * _LANE)
    return _lane_perm(x, perm, perm.T, "s5_to_chunks")


def _from_chunks(y, lc, w, perm):
    tiles = w // _LANE
    n_chunks = y.shape[0] // tiles
    x = _lane_perm(y, perm.T, perm, "s5_from_chunks")
    return x.reshape(tiles, n_chunks, lc, _LANE).transpose(1, 2, 0, 3).reshape(n_chunks * lc, w)


def _vec(v, cb):
    return v.reshape(cb, 1, -1)


S5_LC = 16
RG_CB, RG_TT = 4, 256
HG_TT = 128
ROW_TT = 256
MERGE_TT = 128


_LAYER_SMALL = ['s5_lambda_re', 's5_lambda_im', 's5_log_step', 's5_b_re', 's5_b_im', 's5_c_re', 's5_c_im', 's5_d',
                's5_b_glu', 'rg_conv_w', 'rg_conv_b', 'rg_w_a', 'rg_b_a', 'rg_w_x', 'rg_b_x', 'rg_lambda', 'hg_norm_w']


def _lower_bounds(hg_lower_bounds):
    lb_sm = jax.nn.softmax(hg_lower_bounds, axis=0)
    return jnp.cumsum(lb_sm, axis=0) - lb_sm[0]


def _rms_bf16(x, w):
    T, D = x.shape
    return tiled_op(_rms_tile, "rms", [w.reshape(1, 1, D)], [x], [], [(D, BF16)], 1, min(ROW_TT, T))[0]


def _loss_rows(x, w, target):
    T, D = x.shape
    return tiled_op(_loss_tile, "loss", [w.reshape(1, 1, D)], [x, target], [], [(1, F32)], 1, min(ROW_TT, T))[0]


def _layer_tail(z, x, lw, rest, nxt):
    T, D = x.shape
    lw = dict(lw, **_unpack_rest(rest, lw["s5_d"].shape[0], D))
    W = lw["s5_d"].shape[0]
    row_tt = min(ROW_TT, T)
    rg_tt, hg_tt = min(RG_TT, T), min(HG_TT, T)
    n_chunks = T // S5_LC
    gpt = _LANE // S5_GROUP
    perm = _lane_perm_matrix(S5_LC, S5_GROUP)
    s5_tiling = _Tiling(W // S5_GROUP, n_chunks, 1, ncb=gpt, bmap=lambda cb, t: (cb // gpt, cb % gpt))
    g_a, x_b, g_b, q_c, f_c, i_c, g_c = [(z, k * W, W) for k in range(1, 8)]
    gl = [(z, 8 * W + n * D, D) for n in range(3)]
    s5p = _s5_params(lw["s5_lambda_re"], lw["s5_lambda_im"], lw["s5_log_step"], lw["s5_b_re"], lw["s5_b_im"],
                     lw["s5_c_re"], lw["s5_c_im"], lw["s5_d"], int(math.log2(n_chunks)))
    (y1c,) = tiled_op(_s5_tile, "s5", s5p, [_to_chunks(z[:, :W], S5_LC, S5_GROUP, perm)], [],
                      [(S5_LC * _LANE, BF16)], W // S5_GROUP, n_chunks, tiling=s5_tiling)
    y1 = _from_chunks(y1c, S5_LC, W, perm)
    zg = linear(y1, lw["s5_w_glu"].reshape(1, W, W), "w_glu", out_dtype=BF16)
    (y_a,) = tiled_op(_glu_tile, "glu", [lw["s5_b_glu"].reshape(1, 1, W)], [y1, zg, g_a], [], [(W, BF16)], 1, row_tt)
    rgp = [lw["rg_conv_w"].reshape(RG_CONV, RG_CB, W // RG_CB).transpose(1, 0, 2),
           _vec(lw["rg_conv_b"], RG_CB), _block_diag(lw["rg_w_a"], RG_CB), _vec(lw["rg_b_a"], RG_CB),
           _block_diag(lw["rg_w_x"], RG_CB), _vec(lw["rg_b_x"], RG_CB), _vec(jax.nn.softplus(-lw["rg_lambda"]), RG_CB)]
    rc = W // RG_CB
    (y_b,) = tiled_op(_rg_tile, "rg", rgp, [x_b, g_b], [(rg_tt, rc), (1, rc)], [(W, BF16)], RG_CB, rg_tt)
    dk = W // HG_HEADS
    hgp = [_vec(lw["lbs"], HG_HEADS), _vec(lw["hg_norm_w"], HG_HEADS)]
    land = None
    if nxt is None:
        (y_c,) = tiled_op(_hg_tile, "hg", hgp, [q_c, f_c, i_c, g_c], [(dk, dk)], [(W, BF16)], HG_HEADS, hg_tt)
    else:
        (y_c,), (land,) = tiled_op(_hg_tile, "hg_gather", hgp, [q_c, f_c, i_c, g_c], [(dk, dk)], [(W, BF16)],
                                   HG_HEADS, hg_tt, side_of=(_gather_side, [nxt]))
    br = [linear(y, lw["w_branch"][n], "w_br", out_dtype=BF16) for n, y in enumerate((y_a, y_b, y_c))]
    (mg,) = tiled_op(_merge_tile, "merge", [], br + gl, [], [(D, BF16)], 1, min(MERGE_TT, T))
    return linear(mg, lw["w_out"].reshape(1, D, D), "w_out", res=x), land


def _pack_rest(w_glu, w_branch, w_out):
    c = w_branch.shape[-1]
    parts = [w_branch[n].reshape(2, -1, c) for n in range(3)] + [w_out.reshape(2, -1, c), w_glu.reshape(2, -1, c)]
    return jnp.concatenate(parts, 1)


def _rest_rows(W, D):
    c = D // N_CHIPS
    return [W // 2] * 3 + [(D // N_CHIPS) * D // (2 * c), (W // N_CHIPS) * W // (2 * c)]


def _unpack_rest(p, W, D):
    lead = p.shape[:-3]
    out, off = [], 0
    for r in _rest_rows(W, D):
        out.append(p[..., off:off + r, :])
        off += r
    br = [b.reshape(lead + (W, D // N_CHIPS)) for b in out[:3]]
    w_out = out[3].reshape(lead + (D // N_CHIPS, D))
    w_glu = out[4].reshape(lead + (W // N_CHIPS, W))
    return {"w_branch": br, "w_out": w_out, "s5_w_glu": w_glu}


_ANY = pl.BlockSpec(memory_space=pl.ANY)


def _place():
    x, y, c = lax.axis_index("x"), lax.axis_index("y"), lax.axis_index("c")
    chips = [(1 - x, y), (x, 1 - y), (1 - x, 1 - y)]
    return x, y, c, chips


def _rcopy(src, dst, send_sems, recv_sems, k, to):
    return pltpu.make_async_remote_copy(src_ref=src, dst_ref=dst, send_sem=send_sems.at[k], recv_sem=recv_sems.at[k],
                                        device_id=to, device_id_type=MESH)


def _gather_side(w):
    shape = (N_CHIPS,) + w.shape

    def first(src, land, send, recv):
        x, y, c, chips = _place()
        me = 2 * x + y
        return [_rcopy(src.at[c], land.at[me, c], send, recv, j, (*chip, c)) for j, chip in enumerate(chips)]

    def start(ins, outs, send, recv):
        for cp in first(ins[0], outs[0], send, recv):
            cp.start()

    def finish(ins, outs, send, recv):
        land = outs[0]
        x, y, c, chips = _place()
        sibling = (x, y, 1 - c)
        passed = []
        for j, chip in enumerate(chips):
            s = 2 * chip[0] + chip[1]
            _rcopy(land.at[s, c], land.at[s, c], send, recv, j, (*chip, c)).wait_recv()
            cp = _rcopy(land.at[s, c], land.at[s, c], send, recv, 3 + j, sibling)
            cp.start()
            passed.append(cp)
        for j, chip in enumerate(chips):
            s = 2 * chip[0] + chip[1]
            _rcopy(land.at[s, 1 - c], land.at[s, 1 - c], send, recv, 3 + j, sibling).wait_recv()
        for cp in first(ins[0], land, send, recv) + passed:
            cp.wait_send()

    return _Side([w], [jax.ShapeDtypeStruct(shape, w.dtype)], 6, start, finish)


def _place_own(land, w):
    chip = 2 * lax.axis_index("x") + lax.axis_index("y")
    full = lax.dynamic_update_slice(land, w[None], (chip, 0, 0, 0))
    return full.reshape(N_CHIPS, 2 * w.shape[1], w.shape[2])


def gather_rows(w, name):
    side = _gather_side(w)

    def body(w_ref, o_ref, send, recv):
        side.start([w_ref], [o_ref], send, recv)
        side.finish([w_ref], [o_ref], send, recv)

    return pl.pallas_call(
        body, name=name, in_specs=[_ANY], out_specs=_ANY, out_shape=side.outs[0],
        scratch_shapes=[pltpu.SemaphoreType.DMA((6,)), pltpu.SemaphoreType.DMA((6,))],
    )(w)


def _scatter_side(p):
    def copies(src, dst, send, recv):
        x, y, c, chips = _place()
        return [_rcopy(src.at[2 * chip[0] + chip[1]], dst.at[j], send, recv, j, (*chip, c))
                for j, chip in enumerate(chips)]

    def start(ins, outs, send, recv):
        for cp in copies(ins[0], outs[0], send, recv):
            cp.start()

    def finish(ins, outs, send, recv):
        for cp in copies(ins[0], outs[0], send, recv):
            cp.wait()

    return _Side([p], [jax.ShapeDtypeStruct((3,) + p.shape[1:], p.dtype)], 3, start, finish)


def scatter_rows(p, name):
    side = _scatter_side(p)

    def body(p_ref, o_ref, send, recv):
        side.start([p_ref], [o_ref], send, recv)
        side.finish([p_ref], [o_ref], send, recv)

    return pl.pallas_call(
        body, name=name, in_specs=[_ANY], out_specs=_ANY, out_shape=side.outs[0],
        scratch_shapes=[pltpu.SemaphoreType.DMA((3,)), pltpu.SemaphoreType.DMA((3,))],
    )(p)


def _swap_rows(g, name):
    def body(g_ref, o_ref, send_sems, recv_sems):
        x, y, c, _ = _place()
        cp = _rcopy(g_ref.at[:, 1 - c], o_ref, send_sems, recv_sems, 0, (x, y, 1 - c))
        cp.start()
        cp.wait()

    return pl.pallas_call(
        body, name=name, in_specs=[_ANY], out_specs=_ANY,
        out_shape=jax.ShapeDtypeStruct((g.shape[0],) + g.shape[2:], g.dtype),
        scratch_shapes=[pltpu.SemaphoreType.DMA((1,)), pltpu.SemaphoreType.DMA((1,))],
    )(g)


def _swap_whole(g, name):
    def body(g_ref, o_ref, send_sems, recv_sems):
        x, y, c, _ = _place()
        cp = _rcopy(g_ref, o_ref, send_sems, recv_sems, 0, (x, y, 1 - c))
        cp.start()
        cp.wait()

    return pl.pallas_call(
        body, name=name, in_specs=[_ANY], out_specs=_ANY, out_shape=jax.ShapeDtypeStruct(g.shape, g.dtype),
        scratch_shapes=[pltpu.SemaphoreType.DMA((1,)), pltpu.SemaphoreType.DMA((1,))],
    )(g)


def _ew_call(fn, ins, out_dtypes, name):
    shape = ins[0].shape
    n = shape[-1]
    ins2 = [a.reshape(-1, n) for a in ins]
    rows = ins2[0].shape[0]
    tr = _pick(rows, (256, 128, 64, 32, 16, 8)) if n <= 2048 else _pick(rows, (128, 64, 32, 16, 8))
    n_in = len(ins2)

    def body(*refs):
        outs = fn(*[r[...] for r in refs[:n_in]])
        for o, v in zip(refs[n_in:], outs):
            o[...] = v.astype(o.dtype)

    spec = pl.BlockSpec((tr, n), lambda i: (i, 0))
    res = pl.pallas_call(
        body, name=name, grid=(rows // tr,), in_specs=[spec] * n_in, out_specs=[spec] * len(out_dtypes),
        out_shape=[jax.ShapeDtypeStruct((rows, n), dt) for dt in out_dtypes],
        compiler_params=_cparams(("parallel",)),
    )(*ins2)
    return [r.reshape(shape) for r in res]


def all_reduce_small(buf, name):
    _, R, Ln = buf.shape

    def body(in_ref, out_ref, recv_ref, send_a, recv_a, send_b, recv_b):
        x, y, c = lax.axis_index("x"), lax.axis_index("y"), lax.axis_index("c")
        me = 4 * x + 2 * y + c
        peers = []
        for r in range(1, 8):
            px, py, pc = x ^ ((r >> 2) & 1), y ^ ((r >> 1) & 1), c ^ (r & 1)
            peers.append((r, (px, py, pc), 4 * px + 2 * py + pc))
        cps = [_rcopy(in_ref.at[idx], recv_ref.at[r], send_a, recv_a, r, to) for r, to, idx in peers]
        for cp in cps:
            cp.start()
        for cp in cps:
            cp.wait()
        acc = in_ref[me]
        for r in range(1, 8):
            acc = acc + recv_ref[r]
        out_ref[me] = acc
        cps = [_rcopy(out_ref.at[me], out_ref.at[me], send_b, recv_b, r, to) for r, to, idx in peers]
        for cp in cps:
            cp.start()
        for (r, to, idx), cp in zip(peers, cps):
            cp.wait_send()
            _rcopy(out_ref.at[idx], out_ref.at[idx], send_b, recv_b, r, to).wait_recv()

    vm = pl.BlockSpec(memory_space=pltpu.VMEM)
    return pl.pallas_call(
        body, name=name, in_specs=[vm], out_specs=vm,
        out_shape=jax.ShapeDtypeStruct(buf.shape, F32),
        scratch_shapes=[pltpu.VMEM(buf.shape, F32)] + [pltpu.SemaphoreType.DMA((8,))] * 4,
        compiler_params=pltpu.CompilerParams(vmem_limit_bytes=VMEM_LIMIT_BYTES),
    )(buf)


def _adamw_math(w, g, m, v):
    m = ADAM_B1 * m + (1.0 - ADAM_B1) * g
    v = ADAM_B2 * v + (1.0 - ADAM_B2) * (g * g)
    m_hat = m / (1.0 - ADAM_B1 ** ADAM_STEP)
    v_hat = v / (1.0 - ADAM_B2 ** ADAM_STEP)
    delta = -ADAM_LR * (m_hat / (jnp.sqrt(v_hat) + ADAM_EPS) + ADAM_WD * w)
    return [delta, m, v]


def adamw(w, g, m, v, name):
    return _ew_call(_adamw_math, [w, g, m, v], [F32, F32, F32], name)


_WEIGHTS = ['norm_w', 'w_in', 's5_lambda_re', 's5_lambda_im', 's5_log_step', 's5_b_re', 's5_b_im', 's5_c_re', 's5_c_im',
            's5_d', 's5_w_glu', 's5_b_glu', 'rg_conv_w', 'rg_conv_b', 'rg_w_a', 'rg_b_a', 'rg_w_x', 'rg_b_x', 'rg_lambda',
            'hg_lower_bounds', 'hg_norm_w', 'w_branch', 'w_out', 'final_norm_w']
_BIG = ('w_in', 's5_w_glu', 'w_branch', 'w_out')
_SMALL = [n for n in _WEIGHTS if n not in _BIG]
_LANES = 128
_N_DEV = 8


def _pack(arrs):
    flat = jnp.concatenate([a.reshape(-1) for a in arrs])
    unit = _N_DEV * 8 * _LANES
    total = -(-flat.shape[0] // unit) * unit
    flat = jnp.pad(flat, (0, total - flat.shape[0]))
    return flat.reshape(_N_DEV, total // (_N_DEV * _LANES), _LANES)


def _unpack(buf, shapes):
    flat = buf.reshape(-1)
    out, off = [], 0
    for s in shapes:
        n = math.prod(s)
        out.append(flat[off:off + n].reshape(s))
        off += n
    return out


def _step(a):
    x_idx, y_idx, c_idx = lax.axis_index("x"), lax.axis_index("y"), lax.axis_index("c")
    chip = 2 * x_idx + y_idx
    L = a["norm_w"].shape[0]
    W = a["s5_d"].shape[1]
    cw = a["rg_conv_w"]
    wc = cw.shape[2]
    placed = lax.dynamic_update_slice(jnp.zeros((L, RG_CONV, W), F32), cw, (0, 0, chip * wc))
    placed = placed * (c_idx == 0).astype(F32)
    conv_full = _unpack(all_reduce_small(_pack([placed]), "gather_conv"), [(L, RG_CONV, W)])[0]
    wts = {n: a[n] for n in _SMALL}
    wts["rg_conv_w"] = conv_full
    lbs, vjp_lbs = jax.vjp(_lower_bounds, a["hg_lower_bounds"])
    D = a["w_in"].shape[1]
    w_in = a["w_in"].astype(BF16).reshape(L, 2, D // 2, a["w_in"].shape[2])
    rest = [_pack_rest(a["s5_w_glu"][l].astype(BF16), a["w_branch"][l].astype(BF16), a["w_out"][l].astype(BF16))
            for l in range(L)]
    x = a["x"][0]
    land_w = gather_rows(w_in[0], "gather_w_in_first")
    land_r = gather_rows(rest[0], "gather_rest_first")
    saved = []
    for l in range(L):
        last = l + 1 == L
        h, vjp_rms = jax.vjp(_rms_bf16, x, a["norm_w"][l])
        w_l = _place_own(land_w, w_in[l])
        if last:
            z = _mm_nn(h, w_l, None, "w_in_fwd", BF16)
        else:
            z, land_w = _mm_nn(h, w_l, None, "w_in_fwd_gather", BF16, side=_gather_side(w_in[l + 1]))
        lw = {n: wts[n][l] for n in _LAYER_SMALL}
        lw["lbs"] = lbs[l]
        r_l = lax.dynamic_update_slice(land_r, rest[l][None], (chip, 0, 0, 0))
        x_next, vjp_tail, land_r = jax.vjp(functools.partial(_layer_tail, nxt=None if last else rest[l + 1]),
                                           z, x, lw, r_l, has_aux=True)
        saved.append((h, w_l, vjp_rms, vjp_tail))
        x = x_next
    rows, vjp_loss = jax.vjp(_loss_rows, x, a["final_norm_w"], a["loss_target"][0])
    loss = lax.psum(jnp.sum(rows), ("x", "y", "c"))
    dx, d_final, _ = vjp_loss(jnp.ones_like(rows))
    gw = {n: [None] * L for n in _LAYER_SMALL + ["lbs", "norm_w"]}
    pairs = {"w": [None] * L, "r": [None] * L}
    recvs = {"w": [None] * L, "r": [None] * L}

    def pair_sum(g, tag):
        got = _swap_rows(g, "reduce_" + tag + "_swap")
        mine = lax.dynamic_index_in_dim(g, c_idx, 1, keepdims=False)
        return _ew_call(lambda p, q: [p.astype(F32) + q.astype(F32)], [mine, got], [BF16], "reduce_" + tag + "_pair")[0]

    for l in reversed(range(L)):
        h, w_l, vjp_rms, vjp_tail = saved[l]
        dz, dx_res, dlw, d_rest = vjp_tail(dx)
        if l + 1 == L:
            dh = _mm_nt(dz, w_l, h.dtype, "w_in_bwd_a")
            dw = _mm_tn(h, dz, N_CHIPS, BF16, "w_in_bwd_w")
        else:
            dh, recvs["r"][l + 1] = _mm_nt(dz, w_l, h.dtype, "w_in_bwd_a_scatter", side=_scatter_side(pairs["r"][l + 1]))
            dw, recvs["w"][l + 1] = _mm_tn(h, dz, N_CHIPS, BF16, "w_in_bwd_w_scatter", side=_scatter_side(pairs["w"][l + 1]))
        dx_rms, gw["norm_w"][l] = vjp_rms(dh)
        dx = dx_rms + dx_res
        for n in _LAYER_SMALL + ["lbs"]:
            gw[n][l] = dlw[n]
        pairs["w"][l] = pair_sum(dw.reshape(N_CHIPS, 2, D // 2, dw.shape[2]), "w_in")
        pairs["r"][l] = pair_sum(d_rest, "rest")
    recvs["w"][0] = scatter_rows(pairs["w"][0], "reduce_w_in_scatter")
    recvs["r"][0] = scatter_rows(pairs["r"][0], "reduce_rest_scatter")
    south = c_idx == 0

    def finish(tag, name):
        halves = []
        for l in range(L):
            own = lax.dynamic_index_in_dim(pairs[tag][l], chip, 0, keepdims=False)
            r = recvs[tag][l]
            halves.append(_ew_call(
                lambda p, r0, r1, r2: [((p.astype(F32) + r0.astype(F32)) + r1.astype(F32)) + r2.astype(F32)],
                [own, r[0], r[1], r[2]], [F32], "reduce_" + name + "_sum")[0])
        gh = jnp.stack(halves, 0)
        other = _swap_whole(gh, "reduce_" + name + "_join")
        return jnp.stack([jnp.where(south, gh, other), jnp.where(south, other, gh)], 1)

    grads = {"w_in": finish("w", "w_in").reshape(a["w_in"].shape)}
    g_rest = _unpack_rest(finish("r", "rest"), W, D)
    grads["w_branch"] = jnp.stack(g_rest["w_branch"], 1)
    grads["w_out"] = g_rest["w_out"]
    grads["s5_w_glu"] = g_rest["s5_w_glu"]
    gx = dx
    (d_lb,) = vjp_lbs(jnp.stack(gw["lbs"], 0))
    gw = {n: jnp.stack(v, 0) for n, v in gw.items()}
    gw["hg_lower_bounds"] = d_lb
    gw["final_norm_w"] = d_final
    small_shapes = [gw[n].shape for n in _SMALL]
    red = _unpack(all_reduce_small(_pack([gw[n].astype(F32) for n in _SMALL]), "reduce_small"), small_shapes)
    for n, g in zip(_SMALL, red):
        grads[n] = g
    grads["rg_conv_w"] = lax.dynamic_slice_in_dim(grads["rg_conv_w"], chip * wc, wc, 2)
    delta, new_m, new_v = {}, {}, {}
    for n in _BIG:
        delta[n], new_m[n], new_v[n] = adamw(a[n], grads[n], a["m_" + n], a["v_" + n], "adamw_" + n)
    shapes = [a[n].shape for n in _SMALL]
    packed = [_pack([t[n] for n in _SMALL]) for t in
              (a, grads, {n: a["m_" + n] for n in _SMALL}, {n: a["v_" + n] for n in _SMALL})]
    for dst, buf in zip((delta, new_m, new_v), adamw(*packed, "adamw_small")):
        for n, t in zip(_SMALL, _unpack(buf, shapes)):
            dst[n] = t
    return (loss, gx[None], *[grads[n] for n in _WEIGHTS], *[delta[n] for n in _WEIGHTS],
            *[new_m[n] for n in _WEIGHTS], *[new_v[n] for n in _WEIGHTS])


_ARG_NAMES = ["x"] + _WEIGHTS + ["loss_target"] + ["m_" + n for n in _WEIGHTS] + ["v_" + n for n in _WEIGHTS]


def kernel(x, norm_w, w_in, s5_lambda_re, s5_lambda_im, s5_log_step, s5_b_re, s5_b_im, s5_c_re, s5_c_im, s5_d, s5_w_glu, s5_b_glu, rg_conv_w, rg_conv_b, rg_w_a, rg_b_a, rg_w_x, rg_b_x, rg_lambda, hg_lower_bounds, hg_norm_w, w_branch, w_out, final_norm_w, loss_target, m_norm_w, m_w_in, m_s5_lambda_re, m_s5_lambda_im, m_s5_log_step, m_s5_b_re, m_s5_b_im, m_s5_c_re, m_s5_c_im, m_s5_d, m_s5_w_glu, m_s5_b_glu, m_rg_conv_w, m_rg_conv_b, m_rg_w_a, m_rg_b_a, m_rg_w_x, m_rg_b_x, m_rg_lambda, m_hg_lower_bounds, m_hg_norm_w, m_w_branch, m_w_out, m_final_norm_w, v_norm_w, v_w_in, v_s5_lambda_re, v_s5_lambda_im, v_s5_log_step, v_s5_b_re, v_s5_b_im, v_s5_c_re, v_s5_c_im, v_s5_d, v_s5_w_glu, v_s5_b_glu, v_rg_conv_w, v_rg_conv_b, v_rg_w_a, v_rg_b_a, v_rg_w_x, v_rg_b_x, v_rg_lambda, v_hg_lower_bounds, v_hg_norm_w, v_w_branch, v_w_out, v_final_norm_w):
    vals = (x, norm_w, w_in, s5_lambda_re, s5_lambda_im, s5_log_step, s5_b_re, s5_b_im, s5_c_re, s5_c_im, s5_d, s5_w_glu, s5_b_glu, rg_conv_w, rg_conv_b, rg_w_a, rg_b_a, rg_w_x, rg_b_x, rg_lambda, hg_lower_bounds, hg_norm_w, w_branch, w_out, final_norm_w, loss_target, m_norm_w, m_w_in, m_s5_lambda_re, m_s5_lambda_im, m_s5_log_step, m_s5_b_re, m_s5_b_im, m_s5_c_re, m_s5_c_im, m_s5_d, m_s5_w_glu, m_s5_b_glu, m_rg_conv_w, m_rg_conv_b, m_rg_w_a, m_rg_b_a, m_rg_w_x, m_rg_b_x, m_rg_lambda, m_hg_lower_bounds, m_hg_norm_w, m_w_branch, m_w_out, m_final_norm_w, v_norm_w, v_w_in, v_s5_lambda_re, v_s5_lambda_im, v_s5_log_step, v_s5_b_re, v_s5_b_im, v_s5_c_re, v_s5_c_im, v_s5_d, v_s5_w_glu, v_s5_b_glu, v_rg_conv_w, v_rg_conv_b, v_rg_w_a, v_rg_b_a, v_rg_w_x, v_rg_b_x, v_rg_lambda, v_hg_lower_bounds, v_hg_norm_w, v_w_branch, v_w_out, v_final_norm_w)
    return _step(dict(zip(_ARG_NAMES, vals)))
```
